```python
import jax, jax.numpy as jnp
from jax import lax
import numpy as np

D_MODEL = 2048
BATCH = 2
SEQ = 4096
DEPTH = 2
DEC_BATCH = 8
DEC_SEQ = 8
PAST_LEN = 16384
PAGE_SIZE = 128

N_MIXERS = 2
N_NSA = (DEPTH + N_MIXERS - 1) // N_MIXERS
N_CONV = DEPTH // N_MIXERS
N_HEADS = 16
HEAD_DIM = D_MODEL // N_HEADS
N_KV = 4
GQA_R = N_HEADS // N_KV
Q_DIM = N_HEADS * HEAD_DIM
KV_DIM = N_KV * HEAD_DIM
NSA_IN = Q_DIM + 6 * KV_DIM + 3 * N_HEADS
CMP_BLOCK = 32
CMP_STRIDE = 16
CMP_RATIO = CMP_BLOCK // CMP_STRIDE
CMP_HIDDEN = HEAD_DIM
SEL_BLOCK = 64
SEL_SPB = SEL_BLOCK // CMP_STRIDE
TOPK = 16
WINDOW = 512
Q_BLOCK = 64
ROPE_THETA = 10000.0
CONV_W = 31
FF_RAW = -(-8 * D_MODEL // 3)
D_FF = -(-FF_RAW // 256) * 256
PLE_DIM = 256
RMS_EPS = 1e-6
LN_EPS = 1e-5
NEG = -1e30

kernel_name = 'nsa_conformer_hybrid_decode_step'


def rmsnorm(x, g):
    xf = x.astype(jnp.float32)
    y = xf * lax.rsqrt(jnp.mean(xf * xf, axis=-1, keepdims=True) + RMS_EPS)
    return (y * g.astype(jnp.float32)).astype(x.dtype)


def layernorm(x, g, b):
    xf = x.astype(jnp.float32)
    mu = jnp.mean(xf, axis=-1, keepdims=True)
    var = jnp.mean(jnp.square(xf - mu), axis=-1, keepdims=True)
    y = (xf - mu) * lax.rsqrt(var + LN_EPS)
    return (y * g.astype(jnp.float32) + b.astype(jnp.float32)).astype(x.dtype)


def rope(x, pos):
    half = HEAD_DIM // 2
    inv = ROPE_THETA ** (-jnp.arange(half, dtype=jnp.float32) / half)
    ang = pos.astype(jnp.float32)[:, None] * inv[None, :]
    cos = jnp.cos(ang)[None, :, None, :]
    sin = jnp.sin(ang)[None, :, None, :]
    xf = x.astype(jnp.float32)
    x1, x2 = xf[..., :half], xf[..., half:]
    return jnp.concatenate([x1 * cos - x2 * sin, x2 * cos + x1 * sin], axis=-1).astype(x.dtype)


def masked_softmax(s, mask):
    s = jnp.where(mask, s, NEG)
    m = jnp.max(s, axis=-1, keepdims=True)
    e = jnp.where(mask, jnp.exp(s - m), 0.0)
    return e / jnp.maximum(jnp.sum(e, axis=-1, keepdims=True), 1e-30)


def gather_pages(pool, page_table):
    rows = pool[page_table]
    return rows.reshape(page_table.shape[0], page_table.shape[1] * pool.shape[1], N_KV, HEAD_DIM)


def compress(k_full, pe, w1, w2):
    B, L = k_full.shape[0], k_full.shape[1]
    n_chunk = -(-L // CMP_STRIDE)
    n_blk = n_chunk - CMP_RATIO + 1
    kp = jnp.pad(k_full, ((0, 0), (0, n_chunk * CMP_STRIDE - L), (0, 0), (0, 0)))
    chunks = kp.reshape(B, n_chunk, CMP_STRIDE, N_KV, HEAD_DIM).transpose(0, 1, 3, 2, 4)
    chunks = chunks.reshape(B, n_chunk, N_KV, CMP_STRIDE * HEAD_DIM)
    w1r = w1.reshape(CMP_RATIO, CMP_STRIDE * HEAD_DIM, CMP_HIDDEN)
    proj = jnp.einsum('bngc,rch->rbngh', chunks, w1r)
    hid = sum(proj[r, :, r:r + n_blk] for r in range(CMP_RATIO)) + pe.reshape(-1) @ w1
    return jax.nn.silu(hid) @ w2, n_blk


def nsa_mixer(hn, kc_past, vc_past, ks_past, vs_past, kw_past, vw_past,
              w_in, w_out, pe_k, w1_k, w2_k, pe_v, w1_v, w2_v):
    B, T = hn.shape[0], hn.shape[1]
    P = kc_past.shape[1]
    Wp = kw_past.shape[1]
    dt = hn.dtype
    proj = hn @ w_in
    q = proj[..., :Q_DIM].reshape(B, T, N_HEADS, HEAD_DIM)
    kv = proj[..., Q_DIM:Q_DIM + 6 * KV_DIM].reshape(B, T, 6, N_KV, HEAD_DIM)
    gates = jax.nn.sigmoid(proj[..., Q_DIM + 6 * KV_DIM:].astype(jnp.float32)).reshape(B, T, 3, N_KV, GQA_R)
    pos = P + jnp.arange(T)
    q = rope(q, pos)
    k_cmp, v_cmp = rope(kv[:, :, 0], pos), kv[:, :, 1]
    k_sel, v_sel = rope(kv[:, :, 2], pos), kv[:, :, 3]
    k_win, v_win = rope(kv[:, :, 4], pos), kv[:, :, 5]

    L = P + T
    kc_blk, n_blk = compress(jnp.concatenate([kc_past, k_cmp], axis=1), pe_k, w1_k, w2_k)
    vc_blk, _ = compress(jnp.concatenate([vc_past, v_cmp], axis=1), pe_v, w1_v, w2_v)
    cmp_end = jnp.arange(n_blk) * CMP_STRIDE + (CMP_BLOCK - 1)

    n_sel = -(-L // SEL_BLOCK)
    n_ch = n_sel * SEL_SPB
    k_eff = min(TOPK, n_sel)
    pad_sel = ((0, 0), (0, n_sel * SEL_BLOCK - L), (0, 0), (0, 0))
    ks_t = jnp.pad(jnp.concatenate([ks_past, k_sel], axis=1), pad_sel).transpose(0, 2, 1, 3)
    vs_t = jnp.pad(jnp.concatenate([vs_past, v_sel], axis=1), pad_sel).transpose(0, 2, 1, 3)

    qb = min(Q_BLOCK, T)
    nq = -(-T // qb)
    tpad = nq * qb - T
    q_p = jnp.pad(q, ((0, 0), (0, tpad), (0, 0), (0, 0))).reshape(B, nq * qb, N_KV, GQA_R, HEAD_DIM)
    g_p = jnp.pad(gates, ((0, 0), (0, tpad), (0, 0), (0, 0), (0, 0)))
    kw_all = jnp.concatenate([kw_past, k_win], axis=1)
    vw_all = jnp.concatenate([vw_past, v_win], axis=1)
    wpad = ((0, 0), (WINDOW, tpad), (0, 0), (0, 0))
    kw_p = jnp.pad(kw_all, wpad)
    vw_p = jnp.pad(vw_all, wpad)
    scale = HEAD_DIM ** -0.5
    blk = jnp.arange(n_sel)
    b_idx = jnp.arange(B)[:, None, None]
    g_idx = jnp.arange(N_KV)[None, :, None]

    def block(qi):
        qs = qi * qb
        qg = lax.dynamic_slice_in_dim(q_p, qs, qb, axis=1)
        gg = lax.dynamic_slice_in_dim(g_p, qs, qb, axis=1)
        t = P + qs + jnp.arange(qb)
        s = jnp.einsum('bqgrd,bngd->bgrqn', qg, kc_blk).astype(jnp.float32) * scale
        p_c = masked_softmax(s, cmp_end[None, :] <= t[:, None])
        o_c = jnp.einsum('bgrqn,bngd->bqgrd', p_c.astype(dt), vc_blk)
        imp = jnp.pad(jnp.sum(p_c, axis=2), ((0, 0), (0, 0), (0, 0), (CMP_RATIO - 1, n_ch - n_blk)))
        ch = sum(imp[..., CMP_RATIO - 1 - n:CMP_RATIO - 1 - n + n_ch] for n in range(CMP_RATIO))
        blk_imp = ch.reshape(B, N_KV, qb, n_sel, SEL_SPB).sum(-1)
        cur = (t // SEL_BLOCK)[:, None]
        eligible = blk[None, :] * SEL_BLOCK <= t[:, None]
        forced = (blk[None, :] == 0) | (blk[None, :] == cur) | (blk[None, :] == cur - 1)
        score = jnp.where(forced, jnp.inf, jnp.where(eligible, blk_imp, -jnp.inf))
        top_val, top_idx = lax.top_k(score, k_eff)
        tok = top_idx[..., None] * SEL_BLOCK + jnp.arange(SEL_BLOCK)
        tok_mask = (top_val > -jnp.inf)[..., None] & (tok <= t[None, None, :, None, None])
        flat = tok.reshape(B, N_KV, qb * k_eff * SEL_BLOCK)
        kg = ks_t[b_idx, g_idx, flat].reshape(B, N_KV, qb, k_eff * SEL_BLOCK, HEAD_DIM)
        vg = vs_t[b_idx, g_idx, flat].reshape(B, N_KV, qb, k_eff * SEL_BLOCK, HEAD_DIM)
        s = jnp.einsum('bqgrd,bgqkd->bgrqk', qg, kg).astype(jnp.float32) * scale
        p_s = masked_softmax(s, tok_mask.reshape(B, N_KV, 1, qb, k_eff * SEL_BLOCK))
        o_s = jnp.einsum('bgrqk,bgqkd->bqgrd', p_s.astype(dt), vg)
        start = Wp + qs
        kwb = lax.dynamic_slice_in_dim(kw_p, start, WINDOW + qb, axis=1)
        vwb = lax.dynamic_slice_in_dim(vw_p, start, WINDOW + qb, axis=1)
        kidx = start + jnp.arange(WINDOW + qb)
        kpos = kidx - WINDOW + (P - Wp)
        mask_w = (kidx >= WINDOW)[None, :] & (kpos[None, :] <= t[:, None]) & (kpos[None, :] > t[:, None] - WINDOW)
        s = jnp.einsum('bqgrd,bkgd->bgrqk', qg, kwb).astype(jnp.float32) * scale
        p_w = masked_softmax(s, mask_w)
        o_w = jnp.einsum('bgrqk,bkgd->bqgrd', p_w.astype(dt), vwb)
        o = gg[:, :, 0, ..., None] * o_c + gg[:, :, 1, ..., None] * o_s + gg[:, :, 2, ..., None] * o_w
        return o.astype(dt).reshape(B, qb, Q_DIM)

    out = lax.map(block, jnp.arange(nq))
    out = out.transpose(1, 0, 2, 3).reshape(B, nq * qb, Q_DIM)[:, :T] @ w_out
    w_keep = min(WINDOW, Wp + T)
    return out, (k_cmp, v_cmp, k_sel, v_sel, kw_all[:, -w_keep:], vw_all[:, -w_keep:])


def conv_module(hn, conv_past, w_pw1, b_pw1, w_dw, b_dw, ln_g, ln_b, w_pw2, b_pw2):
    a = hn @ w_pw1 + b_pw1
    u = a[..., :D_MODEL] * jax.nn.sigmoid(a[..., D_MODEL:])
    u_all = jnp.concatenate([conv_past, u], axis=1)
    y = lax.conv_general_dilated(u_all, w_dw[:, None, :].astype(u_all.dtype), window_strides=(1,), padding='VALID',
                                 dimension_numbers=('NWC', 'WIO', 'NWC'), feature_group_count=D_MODEL) + b_dw
    y = layernorm(y, ln_g, ln_b)
    return jax.nn.silu(y) @ w_pw2 + b_pw2, u_all[:, -(CONV_W - 1):]


def residual_tail(h, m, p_i, g_mix_post, g_ffn_pre, g_ffn_post, w_gate, w_up, w_down, w_ple, w_ple_gate):
    h = h + rmsnorm(m, g_mix_post)
    f = rmsnorm(h, g_ffn_pre)
    f = (jax.nn.silu(f @ w_gate) * (f @ w_up)) @ w_down
    h = h + rmsnorm(f, g_ffn_post)
    return h + jax.nn.sigmoid(h @ w_ple_gate) * (p_i @ w_ple)


def _stack(rows, k):
    return jnp.stack([r[k] for r in rows])


def setup_inputs(seed: int = 0) -> dict:
    key = jax.random.key(seed)
    ks = iter(jax.random.split(key, 48))
    f32 = jnp.float32

    def nrm(shape, scale):
        return jax.random.normal(next(ks), shape, f32) * scale

    def gain(shape):
        return 1.0 + nrm(shape, 0.05)

    n_pages = PAST_LEN // PAGE_SIZE
    n_used = DEC_BATCH * n_pages
    n_pool = n_used + max(1, n_used // 4)
    w_in_len = min(WINDOW, PAST_LEN)
    page_table = jax.random.permutation(next(ks), n_pool)[:n_used].reshape(DEC_BATCH, n_pages).astype(jnp.int32)
    pshape = (N_NSA, n_pool, PAGE_SIZE, N_KV, HEAD_DIM)
    wshape = (N_NSA, DEC_BATCH, w_in_len, N_KV, HEAD_DIM)
    return {
        'x_prompt': nrm((BATCH, SEQ, D_MODEL), 1.0),
        'x_sample': nrm((DEC_BATCH, DEC_SEQ, D_MODEL), 1.0),
        'p_prompt': nrm((DEPTH, BATCH, SEQ, PLE_DIM), 1.0),
        'p_sample': nrm((DEPTH, DEC_BATCH, DEC_SEQ, PLE_DIM), 1.0),
        'page_table': page_table,
        'cache_k_cmp': nrm(pshape, 1.0),
        'cache_v_cmp': nrm(pshape, 1.0),
        'cache_k_sel': nrm(pshape, 1.0),
        'cache_v_sel': nrm(pshape, 1.0),
        'cache_k_win': nrm(wshape, 1.0),
        'cache_v_win': nrm(wshape, 1.0),
        'state_conv': nrm((N_CONV, DEC_BATCH, CONV_W - 1, D_MODEL), 0.5),
        'norm_mix_pre': gain((DEPTH, D_MODEL)),
        'norm_mix_post': gain((DEPTH, D_MODEL)),
        'norm_ffn_pre': gain((DEPTH, D_MODEL)),
        'norm_ffn_post': gain((DEPTH, D_MODEL)),
        'ffn_w_gate': nrm((DEPTH, D_MODEL, D_FF), D_MODEL ** -0.5),
        'ffn_w_up': nrm((DEPTH, D_MODEL, D_FF), D_MODEL ** -0.5),
        'ffn_w_down': nrm((DEPTH, D_FF, D_MODEL), D_FF ** -0.5),
        'ple_w_proj': nrm((DEPTH, PLE_DIM, D_MODEL), PLE_DIM ** -0.5),
        'ple_w_gate': nrm((DEPTH, D_MODEL, D_MODEL), D_MODEL ** -0.5),
        'nsa_w_in': nrm((N_NSA, D_MODEL, NSA_IN), D_MODEL ** -0.5),
        'nsa_w_out': nrm((N_NSA, Q_DIM, D_MODEL), Q_DIM ** -0.5),
        'nsa_cmp_pe_k': nrm((N_NSA, CMP_BLOCK, HEAD_DIM), 0.1),
        'nsa_cmp_w1_k': nrm((N_NSA, CMP_BLOCK * HEAD_DIM, CMP_HIDDEN), (CMP_BLOCK * HEAD_DIM) ** -0.5),
        'nsa_cmp_w2_k': nrm((N_NSA, CMP_HIDDEN, HEAD_DIM), CMP_HIDDEN ** -0.5),
        'nsa_cmp_pe_v': nrm((N_NSA, CMP_BLOCK, HEAD_DIM), 0.1),
        'nsa_cmp_w1_v': nrm((N_NSA, CMP_BLOCK * HEAD_DIM, CMP_HIDDEN), (CMP_BLOCK * HEAD_DIM) ** -0.5),
        'nsa_cmp_w2_v': nrm((N_NSA, CMP_HIDDEN, HEAD_DIM), CMP_HIDDEN ** -0.5),
        'conv_w_pw1': nrm((N_CONV, D_MODEL, 2 * D_MODEL), D_MODEL ** -0.5),
        'conv_b_pw1': nrm((N_CONV, 2 * D_MODEL), 0.01),
        'conv_w_dw': nrm((N_CONV, CONV_W, D_MODEL), CONV_W ** -0.5),
        'conv_b_dw': nrm((N_CONV, D_MODEL), 0.01),
        'conv_ln_g': gain((N_CONV, D_MODEL)),
        'conv_ln_b': nrm((N_CONV, D_MODEL), 0.01),
        'conv_w_pw2': nrm((N_CONV, D_MODEL, D_MODEL), D_MODEL ** -0.5),
        'conv_b_pw2': nrm((N_CONV, D_MODEL), 0.01),
    }


def reference(x_prompt, x_sample, p_prompt, p_sample, page_table,
              cache_k_cmp, cache_v_cmp, cache_k_sel, cache_v_sel, cache_k_win, cache_v_win, state_conv,
              norm_mix_pre, norm_mix_post, norm_ffn_pre, norm_ffn_post,
              ffn_w_gate, ffn_w_up, ffn_w_down, ple_w_proj, ple_w_gate,
              nsa_w_in, nsa_w_out, nsa_cmp_pe_k, nsa_cmp_w1_k, nsa_cmp_w2_k,
              nsa_cmp_pe_v, nsa_cmp_w1_v, nsa_cmp_w2_v,
              conv_w_pw1, conv_b_pw1, conv_w_dw, conv_b_dw, conv_ln_g, conv_ln_b, conv_w_pw2, conv_b_pw2):
    hp, hs = x_prompt, x_sample
    nsa_p, nsa_s, conv_p, conv_s = [], [], [], []
    for i in range(DEPTH):
        j = i // N_MIXERS
        mp_in = rmsnorm(hp, norm_mix_pre[i])
        ms_in = rmsnorm(hs, norm_mix_pre[i])
        if i % N_MIXERS == 0:
            nw = (nsa_w_in[j], nsa_w_out[j], nsa_cmp_pe_k[j], nsa_cmp_w1_k[j], nsa_cmp_w2_k[j],
                  nsa_cmp_pe_v[j], nsa_cmp_w1_v[j], nsa_cmp_w2_v[j])
            empty = jnp.zeros((hp.shape[0], 0, N_KV, HEAD_DIM), hp.dtype)
            mp, st_p = nsa_mixer(mp_in, empty, empty, empty, empty, empty, empty, *nw)
            ms, st_s = nsa_mixer(ms_in,
                                 gather_pages(cache_k_cmp[j], page_table), gather_pages(cache_v_cmp[j], page_table),
                                 gather_pages(cache_k_sel[j], page_table), gather_pages(cache_v_sel[j], page_table),
                                 cache_k_win[j], cache_v_win[j], *nw)
            nsa_p.append(st_p)
            nsa_s.append(st_s)
        else:
            cw = (conv_w_pw1[j], conv_b_pw1[j], conv_w_dw[j], conv_b_dw[j], conv_ln_g[j], conv_ln_b[j],
                  conv_w_pw2[j], conv_b_pw2[j])
            mp, cp = conv_module(mp_in, jnp.zeros((hp.shape[0], CONV_W - 1, D_MODEL), hp.dtype), *cw)
            ms, cs = conv_module(ms_in, state_conv[j], *cw)
            conv_p.append(cp)
            conv_s.append(cs)
        tail = (norm_mix_post[i], norm_ffn_pre[i], norm_ffn_post[i], ffn_w_gate[i], ffn_w_up[i], ffn_w_down[i],
                ple_w_proj[i], ple_w_gate[i])
        hp = residual_tail(hp, mp, p_prompt[i], *tail)
        hs = residual_tail(hs, ms, p_sample[i], *tail)
    return (hp, hs,
            _stack(nsa_p, 0), _stack(nsa_s, 0), _stack(nsa_p, 1), _stack(nsa_s, 1),
            _stack(nsa_p, 2), _stack(nsa_s, 2), _stack(nsa_p, 3), _stack(nsa_s, 3),
            _stack(nsa_p, 4), _stack(nsa_s, 4), _stack(nsa_p, 5), _stack(nsa_s, 5),
            jnp.stack(conv_p), jnp.stack(conv_s))
```

```python
import functools

import numpy as np
import jax
import jax.numpy as jnp
from jax import lax
from jax.experimental import pallas as pl
from jax.experimental.pallas import tpu as pltpu

F32 = jnp.float32
BF16 = jnp.bfloat16

D_MODEL = 2048
BATCH = 2
SEQ = 4096
DEPTH = 2
DEC_BATCH = 8
DEC_SEQ = 8
PAST_LEN = 16384
PAGE_SIZE = 128
N_HEADS = 16
HEAD_DIM = 128
N_KV = 4
GQA_R = 4
Q_DIM = 2048
KV_DIM = 512
CMP_BLOCK = 32
CMP_STRIDE = 16
SEL_BLOCK = 64
TOPK = 16
WINDOW = 512
ROPE_THETA = 10000.0
CONV_W = 31
D_FF = 5632
PLE_DIM = 256
RMS_EPS = 1e-6
LN_EPS = 1e-5
NEG = -1e30

M_PROMPT = BATCH * SEQ
M_SAMPLE = DEC_BATCH * DEC_SEQ
M_ALL = M_PROMPT + M_SAMPLE
N_PAGES = PAST_LEN // PAGE_SIZE
CHUNKS_PER_PAGE = PAGE_SIZE // CMP_STRIDE
PAGES_PER_STEP = 8

TM = 688
TM_FULL = 688
TN = 512
TK_ATT = 512
WBAND = WINDOW + SEL_BLOCK
VMEM_LIMIT = 56 * 1024 * 1024


def _cparams(sem):
    return pltpu.CompilerParams(dimension_semantics=sem, vmem_limit_bytes=VMEM_LIMIT)


def _rms_scale(x, g):
    ms = jnp.mean(x * x, axis=-1, keepdims=True)
    return x * lax.rsqrt(ms + RMS_EPS) * g


def _dot(a, b):
    return jnp.dot(a, b, preferred_element_type=F32)


def _dot_nt(a, b):
    return lax.dot_general(a, b, (((1,), (1,)), ((), ())), preferred_element_type=F32)


N_QKV_TILES = (Q_DIM + 6 * KV_DIM) // TN
N_Q_TILES = Q_DIM // TN


def _in_proj_kernel(x_ref, g_ref, w_ref, wg_ref, cos_ref, sin_ref, qkv_ref, kvf_ref, gate_ref, xn_ref):
    j = pl.program_id(1)

    @pl.when(j == 0)
    def _():
        xn = _rms_scale(x_ref[...], g_ref[...]).astype(BF16)
        xn_ref[...] = xn
        gate_ref[...] = jax.nn.sigmoid(_dot(xn, wg_ref[...]))

    acc = _dot(xn_ref[...], w_ref[...])
    is_rope = (j < N_Q_TILES + 1) | (j == N_Q_TILES + 2) | (j == N_Q_TILES + 4)
    cos = jnp.where(is_rope, cos_ref[...], 1.0)
    sin = jnp.where(is_rope, sin_ref[...], 0.0)
    parts = []
    for h in range(TN // HEAD_DIM):
        a = acc[:, h * HEAD_DIM:(h + 1) * HEAD_DIM]
        parts.append(a * cos + pltpu.roll(a, HEAD_DIM // 2, axis=1) * sin)
    r = jnp.concatenate(parts, axis=1)
    qkv_ref[...] = r.astype(BF16)

    @pl.when(j >= N_Q_TILES)
    def _():
        kvf_ref[...] = r


def _in_proj(x, g, w, wg, cos, sin):
    m = x.shape[0]
    grid = (m // TM, N_QKV_TILES)
    return pl.pallas_call(
        _in_proj_kernel,
        grid=grid,
        in_specs=[
            pl.BlockSpec((TM, D_MODEL), lambda i, j: (i, 0)),
            pl.BlockSpec((1, D_MODEL), lambda i, j: (0, 0)),
            pl.BlockSpec((D_MODEL, TN), lambda i, j: (0, j)),
            pl.BlockSpec((D_MODEL, HEAD_DIM), lambda i, j: (0, 0)),
            pl.BlockSpec((TM, HEAD_DIM), lambda i, j: (i, 0)),
            pl.BlockSpec((TM, HEAD_DIM), lambda i, j: (i, 0)),
        ],
        out_specs=[
            pl.BlockSpec((TM, TN), lambda i, j: (i, j)),
            pl.BlockSpec((TM, TN), lambda i, j: (i, jnp.maximum(j - N_Q_TILES, 0))),
            pl.BlockSpec((TM, HEAD_DIM), lambda i, j: (i, 0)),
        ],
        out_shape=[
            jax.ShapeDtypeStruct((m, Q_DIM + 6 * KV_DIM), BF16),
            jax.ShapeDtypeStruct((m, 6 * KV_DIM), F32),
            jax.ShapeDtypeStruct((m, HEAD_DIM), F32),
        ],
        scratch_shapes=[pltpu.VMEM((TM, D_MODEL), BF16)],
        compiler_params=_cparams(("parallel", "arbitrary")),
        name="nsa_in_proj",
    )(x, g, w, wg, cos, sin)


def _dual_kernel(x_ref, g_ref, wa_ref, wb_ref, ba_ref, bb_ref, o_ref, xn_ref, *, mode):
    @pl.when(pl.program_id(1) == 0)
    def _():
        xn_ref[...] = _rms_scale(x_ref[...], g_ref[...]).astype(BF16)

    xn = xn_ref[...]
    a = _dot(xn, wa_ref[...]) + ba_ref[...]
    b = _dot(xn, wb_ref[...]) + bb_ref[...]
    if mode == "swiglu":
        o = jax.nn.silu(a) * b
    else:
        o = a * jax.nn.sigmoid(b)
    o_ref[...] = o.astype(o_ref.dtype)


def _dual_proj(x, g, w, n_out, b_off, bias, mode, out_dtype, name):
    m = x.shape[0]
    nb = b_off // TN
    return pl.pallas_call(
        functools.partial(_dual_kernel, mode=mode),
        grid=(m // TM, n_out // TN),
        in_specs=[
            pl.BlockSpec((TM, D_MODEL), lambda i, j: (i, 0)),
            pl.BlockSpec((1, D_MODEL), lambda i, j: (0, 0)),
            pl.BlockSpec((D_MODEL, TN), lambda i, j: (0, j)),
            pl.BlockSpec((D_MODEL, TN), lambda i, j: (0, j + nb)),
            pl.BlockSpec((1, TN), lambda i, j: (0, j)),
            pl.BlockSpec((1, TN), lambda i, j: (0, j + nb)),
        ],
        out_specs=pl.BlockSpec((TM, TN), lambda i, j: (i, j)),
        out_shape=jax.ShapeDtypeStruct((m, n_out), out_dtype),
        scratch_shapes=[pltpu.VMEM((TM, D_MODEL), BF16)],
        compiler_params=_cparams(("parallel", "arbitrary")),
        name=name,
    )(x, g, w, w, bias, bias)


def _proj_norm_res_kernel(a_ref, w_ref, b_ref, g_ref, h_ref, o_ref, acc_ref, *, nk):
    k = pl.program_id(1)
    part = _dot(a_ref[...], w_ref[...])

    @pl.when(k == 0)
    def _():
        acc_ref[...] = part

    @pl.when(k > 0)
    def _():
        acc_ref[...] += part

    @pl.when(k == nk - 1)
    def _():
        mval = acc_ref[...] + b_ref[...]
        o_ref[...] = h_ref[...] + _rms_scale(mval, g_ref[...])


def _proj_norm_res(a, w, b, g, h, nk, name):
    m, kdim = a.shape
    tk = kdim // nk
    return pl.pallas_call(
        functools.partial(_proj_norm_res_kernel, nk=nk),
        grid=(m // TM_FULL, nk),
        in_specs=[
            pl.BlockSpec((TM_FULL, tk), lambda i, k: (i, k)),
            pl.BlockSpec((tk, D_MODEL), lambda i, k: (k, 0)),
            pl.BlockSpec((1, D_MODEL), lambda i, k: (0, 0)),
            pl.BlockSpec((1, D_MODEL), lambda i, k: (0, 0)),
            pl.BlockSpec((TM_FULL, D_MODEL), lambda i, k: (i, 0)),
        ],
        out_specs=pl.BlockSpec((TM_FULL, D_MODEL), lambda i, k: (i, 0)),
        out_shape=jax.ShapeDtypeStruct((m, D_MODEL), F32),
        scratch_shapes=[pltpu.VMEM((TM_FULL, D_MODEL), F32)],
        compiler_params=_cparams(("parallel", "arbitrary")),
        name=name,
    )(a, w, b, g, h)


def _ple_kernel(h_ref, hc_ref, p_ref, wg_ref, wp_ref, o_ref, hb_ref):
    @pl.when(pl.program_id(1) == 0)
    def _():
        hb_ref[...] = h_ref[...].astype(BF16)

    gate = jax.nn.sigmoid(_dot(hb_ref[...], wg_ref[...]))
    o_ref[...] = hc_ref[...] + gate * _dot(p_ref[...], wp_ref[...])


def _ple(h, p, wg, wp, name):
    m = h.shape[0]
    return pl.pallas_call(
        _ple_kernel,
        grid=(m // TM, D_MODEL // TN),
        in_specs=[
            pl.BlockSpec((TM, D_MODEL), lambda i, j: (i, 0)),
            pl.BlockSpec((TM, TN), lambda i, j: (i, j)),
            pl.BlockSpec((TM, PLE_DIM), lambda i, j: (i, 0)),
            pl.BlockSpec((D_MODEL, TN), lambda i, j: (0, j)),
            pl.BlockSpec((PLE_DIM, TN), lambda i, j: (0, j)),
        ],
        out_specs=pl.BlockSpec((TM, TN), lambda i, j: (i, j)),
        out_shape=jax.ShapeDtypeStruct((m, D_MODEL), F32),
        scratch_shapes=[pltpu.VMEM((TM, D_MODEL), BF16)],
        compiler_params=_cparams(("parallel", "arbitrary")),
        name=name,
    )(h, h, p, wg, wp)


def _chunk_rows(ref, g):
    pieces = [ref[pl.ds(N_KV * s + g, CHUNKS_PER_PAGE, stride=N_KV * CMP_STRIDE), :] for s in range(CMP_STRIDE)]
    return jnp.concatenate(pieces, axis=1)


def _compress_kernel(pt_ref, *refs):
    n = PAGES_PER_STEP
    k_refs, v_refs = refs[0:n], refs[n:2 * n]
    wk_ref, wv_ref, pk_ref, pv_ref = refs[2 * n:2 * n + 4]
    rows_per_g = n * CHUNKS_PER_PAGE
    for srcs, w_ref, out_ref in ((k_refs, wk_ref, pk_ref), (v_refs, wv_ref, pv_ref)):
        x = jnp.concatenate([_chunk_rows(srcs[p], g) for g in range(N_KV) for p in range(n)], axis=0)
        proj = _dot(x.astype(BF16), w_ref[...])
        for g in range(N_KV):
            out_ref[0, g] = proj[g * rows_per_g:(g + 1) * rows_per_g]


def _compress_stage1(page_table, k_src, v_src, wk, wv, n_seq, pages_per_seq, name):
    n = PAGES_PER_STEP
    groups = pages_per_seq // n
    n_chunks = pages_per_seq * CHUNKS_PER_PAGE

    def src_spec(slot):
        return pl.BlockSpec(
            (PAGE_SIZE * N_KV, HEAD_DIM), lambda b, t, pt: (pt[b * pages_per_seq + t * n + slot], 0))

    w_spec = pl.BlockSpec((CMP_STRIDE * HEAD_DIM, 2 * HEAD_DIM), lambda b, t, pt: (0, 0))
    out_spec = pl.BlockSpec((1, N_KV, n * CHUNKS_PER_PAGE, 2 * HEAD_DIM), lambda b, t, pt: (b, 0, t, 0))
    out_sds = jax.ShapeDtypeStruct((n_seq, N_KV, n_chunks, 2 * HEAD_DIM), F32)
    return pl.pallas_call(
        _compress_kernel,
        grid_spec=pltpu.PrefetchScalarGridSpec(
            num_scalar_prefetch=1,
            grid=(n_seq, groups),
            in_specs=[src_spec(s) for s in range(n)] * 2 + [w_spec, w_spec],
            out_specs=[out_spec, out_spec],
        ),
        out_shape=[out_sds, out_sds],
        compiler_params=_cparams(("parallel", "arbitrary")),
        name=name,
    )(page_table, *([k_src] * n), *([v_src] * n), wk, wv)


def _compress_finish_kernel(p_ref, pn_ref, pe_ref, w1_ref, w2_ref, o_ref):
    p = p_ref[0, 0]
    c = p.shape[0]
    p0 = p[:, :HEAD_DIM]
    p1 = p[:, HEAD_DIM:]
    nxt = pltpu.roll(p1, c - 1, axis=0)
    last = pn_ref[0, 0][0:1, HEAD_DIM:]
    row = lax.broadcasted_iota(jnp.int32, (c, 1), 0)
    nxt = jnp.where(row == c - 1, last, nxt)
    bias = _dot(pe_ref[...].astype(BF16), w1_ref[...])[0:1]
    hid = p0 + nxt + bias
    o_ref[0, 0] = _dot(jax.nn.silu(hid).astype(BF16), w2_ref[...]).astype(o_ref.dtype)


def _compress_finish(p, p_next, pe8, w1, w2, name):
    n_seq, _, c, _ = p.shape
    return pl.pallas_call(
        _compress_finish_kernel,
        grid=(n_seq, N_KV),
        in_specs=[
            pl.BlockSpec((1, 1, c, 2 * HEAD_DIM), lambda b, g: (b, g, 0, 0)),
            pl.BlockSpec((1, 1, 8, 2 * HEAD_DIM), lambda b, g: (b, g, 0, 0)),
            pl.BlockSpec((8, CMP_BLOCK * HEAD_DIM), lambda b, g: (0, 0)),
            pl.BlockSpec((CMP_BLOCK * HEAD_DIM, HEAD_DIM), lambda b, g: (0, 0)),
            pl.BlockSpec((HEAD_DIM, HEAD_DIM), lambda b, g: (0, 0)),
        ],
        out_specs=pl.BlockSpec((1, 1, c, HEAD_DIM), lambda b, g: (b, g, 0, 0)),
        out_shape=jax.ShapeDtypeStruct((n_seq, N_KV, c, HEAD_DIM), BF16),
        compiler_params=_cparams(("parallel", "parallel")),
        name=name,
    )(p, p_next, pe8, w1, w2)


def _gather_kernel(pt_ref, *refs):
    n = PAGES_PER_STEP
    k_refs, v_refs = refs[0:n], refs[n:2 * n]
    kn_ref, vn_ref, ko_ref, vo_ref = refs[2 * n:2 * n + 4]
    t = pl.program_id(1)
    for srcs, new_ref, out_ref in ((k_refs, kn_ref, ko_ref), (v_refs, vn_ref, vo_ref)):
        for p in range(n):
            page = t * n + p
            for g in range(N_KV):
                old = srcs[p][pl.ds(g, PAGE_SIZE, stride=N_KV), :]
                new = new_ref[pl.ds(g, PAGE_SIZE, stride=N_KV), :]
                val = jnp.where(page < N_PAGES, old, jnp.where(page == N_PAGES, new, 0.0))
                out_ref[0, g, p * PAGE_SIZE:(p + 1) * PAGE_SIZE, :] = val.astype(BF16)


def _gather_pages(page_table, k_pool, v_pool, k_new, v_new, n_out_pages):
    n = PAGES_PER_STEP
    groups = n_out_pages // n
    blk = (PAGE_SIZE * N_KV, HEAD_DIM)

    def src_spec(slot):
        return pl.BlockSpec(
            blk, lambda b, t, pt: (pt[b * N_PAGES + jnp.minimum(t * n + slot, N_PAGES - 1)], 0))

    new_spec = pl.BlockSpec(blk, lambda b, t, pt: (b, 0))
    out_spec = pl.BlockSpec((1, N_KV, n * PAGE_SIZE, HEAD_DIM), lambda b, t, pt: (b, 0, t, 0))
    out_sds = jax.ShapeDtypeStruct((DEC_BATCH, N_KV, n_out_pages * PAGE_SIZE, HEAD_DIM), BF16)
    return pl.pallas_call(
        _gather_kernel,
        grid_spec=pltpu.PrefetchScalarGridSpec(
            num_scalar_prefetch=1,
            grid=(DEC_BATCH, groups),
            in_specs=[src_spec(s) for s in range(n)] * 2 + [new_spec, new_spec],
            out_specs=[out_spec, out_spec],
        ),
        out_shape=[out_sds, out_sds],
        compiler_params=_cparams(("parallel", "arbitrary")),
        name="nsa_gather_sel_pages",
    )(page_table, *([k_pool] * n), *([v_pool] * n), k_new, v_new)


def _masked_softmax(s, mask):
    s = jnp.where(mask, s, NEG)
    m = jnp.max(s, axis=-1, keepdims=True)
    e = jnp.where(mask, jnp.exp(s - m), 0.0)
    return e / jnp.maximum(jnp.sum(e, axis=-1, keepdims=True), 1e-30)


def _split_dot(x, m_bf16):
    out = None
    rem = x
    for _ in range(3):
        part = rem.astype(BF16)
        term = _dot(part, m_bf16)
        out = term if out is None else out + term
        rem = rem - part.astype(F32)
    return out


def _attn_kernel(q_ref, gate_ref, kc_ref, vc_ref, ks_ref, vs_ref, kw_ref, vw_ref, mmat_ref, e_ref, o_ref,
                 *, tq, nb, ns, nsp, pos0, wpos0, q_packed, kv4d, single_block):
    qi = pl.program_id(2)
    rows = GQA_R * tq
    scale = HEAD_DIM ** -0.5
    t0 = pos0 + qi * tq
    if q_packed:
        q = q_ref[0, 0]
    else:
        qq = q_ref[...]
        q = jnp.concatenate([qq[:, r * HEAD_DIM:(r + 1) * HEAD_DIM] for r in range(GQA_R)], axis=0)
    t_row = t0 + (lax.broadcasted_iota(jnp.int32, (rows, 1), 0) & (tq - 1))

    kc = kc_ref[0, 0]
    vc = vc_ref[0, 0]
    s = _dot_nt(q, kc) * scale
    cmp_end = lax.broadcasted_iota(jnp.int32, (1, nb), 1) * CMP_STRIDE + (CMP_BLOCK - 1)
    p_c = _masked_softmax(s, cmp_end <= t_row)
    o_c = _dot(p_c.astype(BF16), vc)
    imp = p_c[0:tq]
    for r in range(1, GQA_R):
        imp = imp + p_c[r * tq:(r + 1) * tq]

    blk_imp = _split_dot(imp, mmat_ref[...])
    jb = lax.broadcasted_iota(jnp.int32, (tq, nsp), 1)
    t_q = t0 + lax.broadcasted_iota(jnp.int32, (tq, 1), 0)
    cur = lax.shift_right_logical(t_q, 6)
    forced = (jb == 0) | (jb == cur) | (jb == cur - 1)
    eligible = jb <= cur
    score = jnp.where(forced, jnp.inf, jnp.where(eligible, blk_imp, -jnp.inf))
    rank = jnp.zeros((tq, nsp), F32)
    for i in range(ns):
        ci = score[:, i:i + 1]
        beats = (ci > score) | ((ci == score) & (jb > i))
        rank = rank + jnp.where(beats, 1.0, 0.0)
    sel = (rank < float(TOPK)) & eligible
    negsel = jnp.where(sel, 0.0, NEG)
    negsel = jnp.concatenate([negsel] * GQA_R, axis=0).astype(BF16)

    if kv4d:
        ks, vs, kw, vw = ks_ref.at[0, 0], vs_ref.at[0, 0], kw_ref.at[0, 0], vw_ref.at[0, 0]
    else:
        ks, vs, kw, vw = ks_ref, vs_ref, kw_ref, vw_ref
    n_tiles = lax.shift_right_logical(t0 + tq - 1, 9) + 1
    tiles_per_group = 128 * SEL_BLOCK // TK_ATT
    carry = (jnp.full((rows, 1), -jnp.inf, F32), jnp.zeros((rows, 1), F32), jnp.zeros((rows, HEAD_DIM), F32))
    for c in range(nsp // 128):
        qa = jnp.concatenate([q, negsel[:, c * 128:(c + 1) * 128]], axis=1)
        lo = c * tiles_per_group

        def body(kt, carry, qa=qa, lo=lo):
            m, l, acc = carry
            r0 = pl.multiple_of(kt * TK_ATT, TK_ATT)
            e0 = pl.multiple_of((kt - lo) * TK_ATT, TK_ATT)
            ka = jnp.concatenate([ks[pl.ds(r0, TK_ATT), :], e_ref[pl.ds(e0, TK_ATT), :]], axis=1)
            sc = _dot_nt(qa, ka) * scale
            kpos = r0 + lax.broadcasted_iota(jnp.int32, (1, TK_ATT), 1)
            sc = jnp.where(kpos <= t_row, sc, NEG)
            m_new = jnp.maximum(m, jnp.max(sc, axis=-1, keepdims=True))
            alpha = jnp.exp(m - m_new)
            p = jnp.exp(sc - m_new)
            l = alpha * l + jnp.sum(p, axis=-1, keepdims=True)
            acc = alpha * acc + _dot(p.astype(BF16), vs[pl.ds(r0, TK_ATT), :])
            return m_new, l, acc

        carry = lax.fori_loop(lo, jnp.minimum(lo + tiles_per_group, n_tiles), body, carry)
    _, l, acc = carry
    o_s = acc / jnp.maximum(l, 1e-30)

    if single_block:
        wstart = pos0 - WINDOW - wpos0
    else:
        wstart = pl.multiple_of(jnp.maximum(t0 - WINDOW - wpos0, 0), SEL_BLOCK)
    kwin = kw[pl.ds(wstart, WBAND), :]
    vwin = vw[pl.ds(wstart, WBAND), :]
    s = _dot_nt(q, kwin) * scale
    kpos = wpos0 + wstart + lax.broadcasted_iota(jnp.int32, (1, WBAND), 1)
    p_w = _masked_softmax(s, (kpos <= t_row) & (kpos > t_row - WINDOW))
    o_w = _dot(p_w.astype(BF16), vwin)

    gt = gate_ref[0]
    outs = []
    for r in range(GQA_R):
        sl = slice(r * tq, (r + 1) * tq)
        outs.append(gt[:, r:r + 1] * o_c[sl] + gt[:, GQA_R + r:GQA_R + r + 1] * o_s[sl]
                    + gt[:, 2 * GQA_R + r:2 * GQA_R + r + 1] * o_w[sl])
    o_ref[...] = jnp.concatenate(outs, axis=1).astype(o_ref.dtype)


def _cmp_to_sel_matrix(nb, nsp):
    n = np.arange(nb)[:, None]
    j = np.arange(nsp)[None, :]
    m = ((n >= 4 * j) & (n <= 4 * j + 3)).astype(np.float32) + ((n >= 4 * j - 1) & (n <= 4 * j + 2)).astype(np.float32)
    return jnp.asarray(m, BF16)


def _block_onehot(n_keys):
    k = np.arange(n_keys)[:, None] // SEL_BLOCK
    return jnp.asarray((k % 128 == np.arange(128)[None, :]).astype(np.float32), BF16)


def _attn_prompt(qkv, gates_t, kc, vc):
    nq = SEQ // SEL_BLOCK
    nb = SEQ // CMP_STRIDE
    ns = SEQ // SEL_BLOCK
    col = lambda base, g: (base // HEAD_DIM) + g
    k_sel0, v_sel0 = Q_DIM + 2 * KV_DIM, Q_DIM + 3 * KV_DIM
    k_win0, v_win0 = Q_DIM + 4 * KV_DIM, Q_DIM + 5 * KV_DIM
    seq_spec = lambda base: pl.BlockSpec((SEQ, HEAD_DIM), lambda b, g, i: (b, col(base, g)))
    blk_spec = pl.BlockSpec((1, 1, nb, HEAD_DIM), lambda b, g, i: (b, g, 0, 0))
    kern = functools.partial(_attn_kernel, tq=SEL_BLOCK, nb=nb, ns=ns, nsp=128, pos0=0, wpos0=0,
                             q_packed=False, kv4d=False, single_block=False)
    return pl.pallas_call(
        kern,
        grid=(BATCH, N_KV, nq),
        in_specs=[
            pl.BlockSpec((SEL_BLOCK, GQA_R * HEAD_DIM), lambda b, g, i: (b * nq + i, g)),
            pl.BlockSpec((1, SEL_BLOCK, HEAD_DIM), lambda b, g, i: (g, b * nq + i, 0)),
            blk_spec, blk_spec,
            seq_spec(k_sel0), seq_spec(v_sel0), seq_spec(k_win0), seq_spec(v_win0),
            pl.BlockSpec((nb, 128), lambda b, g, i: (0, 0)),
            pl.BlockSpec((SEQ, 128), lambda b, g, i: (0, 0)),
        ],
        out_specs=pl.BlockSpec((SEL_BLOCK, GQA_R * HEAD_DIM), lambda b, g, i: (b * nq + i, g)),
        out_shape=jax.ShapeDtypeStruct((M_PROMPT, Q_DIM), BF16),
        compiler_params=_cparams(("parallel", "parallel", "arbitrary")),
        name="nsa_attn_prompt",
    )(qkv, gates_t, kc, vc, qkv, qkv, qkv, qkv, _cmp_to_sel_matrix(nb, 128), _block_onehot(SEQ))


SAMPLE_KEY_PAGES = 136
SAMPLE_KEYS = SAMPLE_KEY_PAGES * PAGE_SIZE


def _attn_sample(q_s, gates_s, kc, vc, ks, vs, kw, vw):
    nb = PAST_LEN // CMP_STRIDE
    ns = -(-(PAST_LEN + DEC_SEQ) // SEL_BLOCK)
    nsp = 384
    kern = functools.partial(_attn_kernel, tq=DEC_SEQ, nb=nb, ns=ns, nsp=nsp, pos0=PAST_LEN,
                             wpos0=PAST_LEN - WINDOW, q_packed=True, kv4d=True, single_block=True)
    head_spec = lambda n: pl.BlockSpec((1, 1, n, HEAD_DIM), lambda b, g, i: (b, g, 0, 0))
    return pl.pallas_call(
        kern,
        grid=(DEC_BATCH, N_KV, 1),
        in_specs=[
            head_spec(GQA_R * DEC_SEQ),
            pl.BlockSpec((1, DEC_SEQ, HEAD_DIM), lambda b, g, i: (g, b, 0)),
            head_spec(nb), head_spec(nb),
            head_spec(SAMPLE_KEYS), head_spec(SAMPLE_KEYS),
            head_spec(WBAND), head_spec(WBAND),
            pl.BlockSpec((nb, nsp), lambda b, g, i: (0, 0)),
            pl.BlockSpec((128 * SEL_BLOCK, 128), lambda b, g, i: (0, 0)),
        ],
        out_specs=pl.BlockSpec((DEC_SEQ, GQA_R * HEAD_DIM), lambda b, g, i: (b, g)),
        out_shape=jax.ShapeDtypeStruct((M_SAMPLE, Q_DIM), F32),
        compiler_params=_cparams(("parallel", "parallel", "arbitrary")),
        name="nsa_attn_sample",
    )(q_s, gates_s, kc, vc, ks, vs, kw, vw, _cmp_to_sel_matrix(nb, nsp), _block_onehot(128 * SEL_BLOCK))


HALO = 32


def _conv_kernel(u_ref, halo_ref, wdw_ref, bdw_ref, lng_ref, lnb_ref, w2_ref, b2_ref, g_ref, h_ref, o_ref,
                 buf_ref, *, t, tiles_per_seq, halo_is_state):
    i = pl.program_id(0)
    if halo_is_state:
        buf_ref[0:HALO, :] = halo_ref[...]
    else:
        first = (i % tiles_per_seq) == 0
        buf_ref[0:HALO, :] = jnp.where(first, 0.0, halo_ref[...])
    buf_ref[HALO:HALO + t, :] = u_ref[...]
    off = HALO - (CONV_W - 1)
    y = jnp.zeros((t, D_MODEL), F32)
    for k in range(CONV_W):
        y = y + buf_ref[pl.ds(off + k, t), :] * wdw_ref[k:k + 1, :]
    y = y + bdw_ref[...]
    mu = jnp.mean(y, axis=-1, keepdims=True)
    yc = y - mu
    var = jnp.mean(yc * yc, axis=-1, keepdims=True)
    z = yc * lax.rsqrt(var + LN_EPS) * lng_ref[...] + lnb_ref[...]
    mval = _dot(jax.nn.silu(z).astype(BF16), w2_ref[...]) + b2_ref[...]
    o_ref[...] = h_ref[...] + _rms_scale(mval, g_ref[...])


def _conv_tail(u, halo_src, wdw, bdw, lng, lnb, w2, b2, g, h, *, t, n_tiles, row0, tiles_per_seq, halo_is_state):
    blk0 = row0 // t
    if halo_is_state:
        halo_spec = pl.BlockSpec((HALO, D_MODEL), lambda i: (i, 0))
    else:
        per = t // HALO
        halo_spec = pl.BlockSpec((HALO, D_MODEL), lambda i: (jnp.maximum((blk0 + i) * per - 1, 0), 0))
    vec = pl.BlockSpec((1, D_MODEL), lambda i: (0, 0))
    kern = functools.partial(_conv_kernel, t=t, tiles_per_seq=tiles_per_seq, halo_is_state=halo_is_state)
    return pl.pallas_call(
        kern,
        grid=(n_tiles,),
        in_specs=[
            pl.BlockSpec((t, D_MODEL), lambda i: (blk0 + i, 0)),
            halo_spec,
            pl.BlockSpec((HALO, D_MODEL), lambda i: (0, 0)),
            vec, vec, vec,
            pl.BlockSpec((D_MODEL, D_MODEL), lambda i: (0, 0)),
            vec, vec,
            pl.BlockSpec((t, D_MODEL), lambda i: (blk0 + i, 0)),
        ],
        out_specs=pl.BlockSpec((t, D_MODEL), lambda i: (i, 0)),
        out_shape=jax.ShapeDtypeStruct((n_tiles * t, D_MODEL), F32),
        scratch_shapes=[pltpu.VMEM((HALO + t, D_MODEL), F32)],
        compiler_params=_cparams(("parallel",)),
        name="conv_tail_state" if halo_is_state else "conv_tail",
    )(u, halo_src, wdw, bdw, lng, lnb, w2, b2, g, h)


def _row(v):
    return v.reshape(1, -1).astype(F32)


def _rope_tables():
    half = HEAD_DIM // 2
    pos = jnp.concatenate([jnp.tile(jnp.arange(SEQ), BATCH), jnp.tile(PAST_LEN + jnp.arange(DEC_SEQ), DEC_BATCH)])
    inv = ROPE_THETA ** (-jnp.arange(half, dtype=F32) / half)
    ang = pos.astype(F32)[:, None] * inv[None, :]
    cos, sin = jnp.cos(ang), jnp.sin(ang)
    return jnp.concatenate([cos, cos], axis=1), jnp.concatenate([-sin, sin], axis=1)


def _cmp_weights(w1):
    w1r = w1.reshape(CMP_BLOCK // CMP_STRIDE, CMP_STRIDE * HEAD_DIM, HEAD_DIM)
    return jnp.concatenate([w1r[0], w1r[1]], axis=1).astype(BF16)


def _residual_tail(h, p, g_pre, g_post, w_gate_up, w_down, w_ple, w_ple_gate, layer):
    zeros_ff = jnp.zeros((1, 2 * D_FF), F32)
    act = _dual_proj(h, _row(g_pre), w_gate_up, D_FF, D_FF, zeros_ff, "swiglu", BF16, f"ffn_up_{layer}")
    h = _proj_norm_res(act, w_down, jnp.zeros((1, D_MODEL), F32), _row(g_post), h, 4, f"ffn_down_{layer}")
    return _ple(h, p, w_ple_gate, w_ple, f"ple_{layer}")


def kernel(x_prompt, x_sample, p_prompt, p_sample, page_table, cache_k_cmp, cache_v_cmp, cache_k_sel, cache_v_sel, cache_k_win, cache_v_win, state_conv, norm_mix_pre, norm_mix_post, norm_ffn_pre, norm_ffn_post, ffn_w_gate, ffn_w_up, ffn_w_down, ple_w_proj, ple_w_gate, nsa_w_in, nsa_w_out, nsa_cmp_pe_k, nsa_cmp_w1_k, nsa_cmp_w2_k, nsa_cmp_pe_v, nsa_cmp_w1_v, nsa_cmp_w2_v, conv_w_pw1, conv_b_pw1, conv_w_dw, conv_b_dw, conv_ln_g, conv_ln_b, conv_w_pw2, conv_b_pw2):
    h = jnp.concatenate([x_prompt.reshape(M_PROMPT, D_MODEL), x_sample.reshape(M_SAMPLE, D_MODEL)], axis=0)
    p_all = jnp.concatenate([p_prompt.reshape(DEPTH, M_PROMPT, PLE_DIM),
                             p_sample.reshape(DEPTH, M_SAMPLE, PLE_DIM)], axis=1).astype(BF16)
    pt_flat = page_table.reshape(-1)

    w_in = nsa_w_in[0]
    n_main = Q_DIM + 6 * KV_DIM
    w_main = w_in[:, :n_main].astype(BF16)
    w_gates = jnp.pad(w_in[:, n_main:], ((0, 0), (0, HEAD_DIM - 3 * N_HEADS))).astype(BF16)
    cos, sin = _rope_tables()
    qkv, kvf, gates = _in_proj(h, _row(norm_mix_pre[0]), w_main, w_gates, cos, sin)

    gates_t = gates[:, :3 * N_HEADS].reshape(M_ALL, 3, N_KV, GQA_R).transpose(2, 0, 1, 3).reshape(N_KV, M_ALL, 3 * GQA_R)
    gates_t = jnp.pad(gates_t, ((0, 0), (0, 0), (0, HEAD_DIM - 3 * GQA_R)))

    kvf_s = kvf[M_PROMPT:].reshape(DEC_BATCH, DEC_SEQ, 6, N_KV, HEAD_DIM)
    qkv_s = qkv[M_PROMPT:]

    wk, wv = _cmp_weights(nsa_cmp_w1_k[0]), _cmp_weights(nsa_cmp_w1_v[0])
    pe_k = jnp.broadcast_to(nsa_cmp_pe_k[0].reshape(1, -1), (8, CMP_BLOCK * HEAD_DIM))
    pe_v = jnp.broadcast_to(nsa_cmp_pe_v[0].reshape(1, -1), (8, CMP_BLOCK * HEAD_DIM))
    w1k, w1v = nsa_cmp_w1_k[0].astype(BF16), nsa_cmp_w1_v[0].astype(BF16)
    w2k, w2v = nsa_cmp_w2_k[0].astype(BF16), nsa_cmp_w2_v[0].astype(BF16)

    pages_prompt = SEQ // PAGE_SIZE
    ident = jnp.arange(BATCH * pages_prompt, dtype=jnp.int32)
    paged = lambda c: kvf[:M_PROMPT, c * KV_DIM:(c + 1) * KV_DIM].reshape(-1, HEAD_DIM)
    pk_p, pv_p = _compress_stage1(ident, paged(0), paged(1), wk, wv, BATCH, pages_prompt, "nsa_cmp_prompt")
    zero_next = jnp.zeros((BATCH, N_KV, 8, 2 * HEAD_DIM), F32)
    kc_p = _compress_finish(pk_p, zero_next, pe_k, w1k, w2k, "nsa_cmp_fin_k_prompt")
    vc_p = _compress_finish(pv_p, zero_next, pe_v, w1v, w2v, "nsa_cmp_fin_v_prompt")

    pool = lambda c: c[0].reshape(-1, HEAD_DIM)
    pk_s, pv_s = _compress_stage1(pt_flat, pool(cache_k_cmp), pool(cache_v_cmp), wk, wv, DEC_BATCH, N_PAGES,
                                  "nsa_cmp_sample")
    new_page = lambda c: jnp.pad(kvf_s[:, :, c], ((0, 0), (0, PAGE_SIZE - DEC_SEQ), (0, 0), (0, 0))).reshape(-1, HEAD_DIM)
    pk_n, pv_n = _compress_stage1(jnp.arange(DEC_BATCH, dtype=jnp.int32), new_page(0), new_page(1), wk, wv,
                                  1, DEC_BATCH, "nsa_cmp_sample_new")
    first_chunk = lambda pn: jnp.pad(
        pn[0].reshape(N_KV, DEC_BATCH, CHUNKS_PER_PAGE, 2 * HEAD_DIM)[:, :, 0:1].transpose(1, 0, 2, 3),
        ((0, 0), (0, 0), (0, 7), (0, 0)))
    kc_s = _compress_finish(pk_s, first_chunk(pk_n), pe_k, w1k, w2k, "nsa_cmp_fin_k_sample")
    vc_s = _compress_finish(pv_s, first_chunk(pv_n), pe_v, w1v, w2v, "nsa_cmp_fin_v_sample")

    o_p = _attn_prompt(qkv, gates_t, kc_p, vc_p)

    ks_s, vs_s = _gather_pages(pt_flat, pool(cache_k_sel), pool(cache_v_sel), new_page(2), new_page(3),
                               SAMPLE_KEY_PAGES)
    kw_all = jnp.concatenate([cache_k_win[0], kvf_s[:, :, 4]], axis=1)
    vw_all = jnp.concatenate([cache_v_win[0], kvf_s[:, :, 5]], axis=1)
    band = lambda w: jnp.pad(w.transpose(0, 2, 1, 3), ((0, 0), (0, 0), (0, WBAND - w.shape[1]), (0, 0))).astype(BF16)
    q_s = qkv_s[:, :Q_DIM].reshape(DEC_BATCH, DEC_SEQ, N_KV, GQA_R, HEAD_DIM).transpose(0, 2, 3, 1, 4)
    q_s = q_s.reshape(DEC_BATCH, N_KV, GQA_R * DEC_SEQ, HEAD_DIM)
    o_s = _attn_sample(q_s, gates_t[:, M_PROMPT:], kc_s, vc_s, ks_s, vs_s, band(kw_all), band(vw_all))

    o_all = jnp.concatenate([o_p, o_s.astype(BF16)], axis=0)
    h = _proj_norm_res(o_all, nsa_w_out[0].astype(BF16), jnp.zeros((1, D_MODEL), F32), _row(norm_mix_post[0]), h, 2,
                       "nsa_out_proj")
    w_gu = jnp.concatenate([ffn_w_gate[0], ffn_w_up[0]], axis=1).astype(BF16)
    h = _residual_tail(h, p_all[0], norm_ffn_pre[0], norm_ffn_post[0], w_gu, ffn_w_down[0].astype(BF16),
                       ple_w_proj[0].astype(BF16), ple_w_gate[0].astype(BF16), 0)

    u = _dual_proj(h, _row(norm_mix_pre[1]), conv_w_pw1[0].astype(BF16), D_MODEL, D_MODEL, _row(conv_b_pw1[0]),
                   "glu", F32, "conv_pw1_glu")
    wdw = jnp.pad(conv_w_dw[0], ((0, HALO - CONV_W), (0, 0)))
    conv_args = (wdw, _row(conv_b_dw[0]), _row(conv_ln_g[0]), _row(conv_ln_b[0]), conv_w_pw2[0].astype(BF16),
                 _row(conv_b_pw2[0]), _row(norm_mix_post[1]))
    t_p = 256
    h_p = _conv_tail(u, u, *conv_args, h, t=t_p, n_tiles=M_PROMPT // t_p, row0=0, tiles_per_seq=SEQ // t_p,
                     halo_is_state=False)
    state = jnp.pad(state_conv[0], ((0, 0), (HALO - (CONV_W - 1), 0), (0, 0))).reshape(DEC_BATCH * HALO, D_MODEL)
    h_s = _conv_tail(u, state, *conv_args, h, t=DEC_SEQ, n_tiles=DEC_BATCH, row0=M_PROMPT, tiles_per_seq=1,
                     halo_is_state=True)
    h = jnp.concatenate([h_p, h_s], axis=0)
    w_gu = jnp.concatenate([ffn_w_gate[1], ffn_w_up[1]], axis=1).astype(BF16)
    h = _residual_tail(h, p_all[1], norm_ffn_pre[1], norm_ffn_post[1], w_gu, ffn_w_down[1].astype(BF16),
                       ple_w_proj[1].astype(BF16), ple_w_gate[1].astype(BF16), 1)

    y_prompt = h[:M_PROMPT].reshape(BATCH, SEQ, D_MODEL)
    y_sample = h[M_PROMPT:].reshape(DEC_BATCH, DEC_SEQ, D_MODEL)
    kv_p = kvf[:M_PROMPT].reshape(BATCH, SEQ, 6, N_KV, HEAD_DIM)
    outs = [y_prompt, y_sample]
    for c in range(4):
        outs.append(kv_p[None, :, :, c])
        outs.append(kvf_s[None, :, :, c])
    w_keep = min(WINDOW, SEQ)
    outs.append(kv_p[None, :, SEQ - w_keep:, 4])
    outs.append(kw_all[None, :, -WINDOW:])
    outs.append(kv_p[None, :, SEQ - w_keep:, 5])
    outs.append(vw_all[None, :, -WINDOW:])
    u_p = u[:M_PROMPT].reshape(BATCH, SEQ, D_MODEL)
    u_s = u[M_PROMPT:].reshape(DEC_BATCH, DEC_SEQ, D_MODEL)
    outs.append(u_p[None, :, SEQ - (CONV_W - 1):])
    outs.append(jnp.concatenate([state_conv[0], u_s], axis=1)[None, :, -(CONV_W - 1):])
    return tuple(outs)
```

```python
import functools
import math

import numpy as np
import jax
import jax.numpy as jnp
from jax import lax
from jax.experimental import pallas as pl
from jax.experimental.pallas import tpu as pltpu

F32 = jnp.float32
BF16 = jnp.bfloat16

D_MODEL = 2048
BATCH = 2
SEQ = 4096
DEPTH = 2
DEC_BATCH = 8
DEC_SEQ = 8
PAST_LEN = 16384
PAGE_SIZE = 128
N_HEADS = 16
HEAD_DIM = 128
N_KV = 4
GQA_R = 4
Q_DIM = 2048
KV_DIM = 512
CMP_BLOCK = 32
CMP_STRIDE = 16
SEL_BLOCK = 64
TOPK = 16
WINDOW = 512
ROPE_THETA = 10000.0
CONV_W = 31
D_FF = 5632
PLE_DIM = 256
RMS_EPS = 1e-6
LN_EPS = 1e-5
NEG = -1e30

M_PROMPT = BATCH * SEQ
M_SAMPLE = DEC_BATCH * DEC_SEQ
M_ALL = M_PROMPT + M_SAMPLE
N_PAGES = PAST_LEN // PAGE_SIZE
PAGE_ROWS = PAGE_SIZE * N_KV
CHUNKS_PER_PAGE = PAGE_SIZE // CMP_STRIDE
PAGES_PER_STEP = 8

TM = 688
TN = 512
TK_ATT = 512
TQ_PROMPT = 128
WBAND_PROMPT = WINDOW + TQ_PROMPT
WBAND_SAMPLE = WINDOW + SEL_BLOCK
VMEM_LIMIT = 56 * 1024 * 1024

Q_PRESCALE = HEAD_DIM ** -0.5 * math.log2(math.e)


def _cparams(sem):
    return pltpu.CompilerParams(dimension_semantics=sem, vmem_limit_bytes=VMEM_LIMIT)


def _rms_scale(x, g):
    ms = jnp.mean(x * x, axis=-1, keepdims=True)
    return x * lax.rsqrt(ms + RMS_EPS) * g


def _dot(a, b):
    return jnp.dot(a, b, preferred_element_type=F32)


def _dot_nt(a, b):
    return lax.dot_general(a, b, (((1,), (1,)), ((), ())), preferred_element_type=F32)


N_QKV_TILES = (Q_DIM + 6 * KV_DIM) // TN
N_Q_TILES = Q_DIM // TN


def _in_proj_kernel(x_ref, g_ref, w_ref, wg_ref, cos_ref, sin_ref, qkv_ref, kvp_ref, gate_ref, xn_ref):
    j = pl.program_id(1)

    @pl.when(j == 0)
    def _():
        xn = _rms_scale(x_ref[...], g_ref[...]).astype(BF16)
        xn_ref[...] = xn
        gate_ref[...] = jax.nn.sigmoid(_dot(xn, wg_ref[...]))

    acc = _dot(xn_ref[...], w_ref[...])
    is_rope = (j < N_Q_TILES + 1) | (j == N_Q_TILES + 2) | (j == N_Q_TILES + 4)
    post = jnp.where(j < N_Q_TILES, Q_PRESCALE, 1.0)
    cos = jnp.where(is_rope, cos_ref[...], 1.0) * post
    sin = jnp.where(is_rope, sin_ref[...], 0.0) * post
    parts = []
    for h in range(TN // HEAD_DIM):
        a = acc[:, h * HEAD_DIM:(h + 1) * HEAD_DIM]
        parts.append(a * cos + pltpu.roll(a, HEAD_DIM // 2, axis=1) * sin)
    qkv_ref[...] = jnp.concatenate(parts, axis=1).astype(BF16)

    @pl.when(j >= N_Q_TILES)
    def _():
        for g in range(N_KV):
            kvp_ref[pl.ds(g, TM, stride=N_KV), :] = parts[g]


def _in_proj(x, g, w, wg, cos, sin):
    m = x.shape[0]
    return pl.pallas_call(
        _in_proj_kernel,
        grid=(m // TM, N_QKV_TILES),
        in_specs=[
            pl.BlockSpec((TM, D_MODEL), lambda i, j: (i, 0)),
            pl.BlockSpec((1, D_MODEL), lambda i, j: (0, 0)),
            pl.BlockSpec((None, D_MODEL, TN), lambda i, j: (0, 0, j)),
            pl.BlockSpec((D_MODEL, HEAD_DIM), lambda i, j: (0, 0)),
            pl.BlockSpec((TM, HEAD_DIM), lambda i, j: (i, 0)),
            pl.BlockSpec((TM, HEAD_DIM), lambda i, j: (i, 0)),
        ],
        out_specs=[
            pl.BlockSpec((TM, TN), lambda i, j: (i, j)),
            pl.BlockSpec((None, TM * N_KV, HEAD_DIM), lambda i, j: (jnp.maximum(j - N_Q_TILES, 0), i, 0)),
            pl.BlockSpec((TM, HEAD_DIM), lambda i, j: (i, 0)),
        ],
        out_shape=[
            jax.ShapeDtypeStruct((m, Q_DIM + 6 * KV_DIM), BF16),
            jax.ShapeDtypeStruct((6, m * N_KV, HEAD_DIM), F32),
            jax.ShapeDtypeStruct((m, HEAD_DIM), F32),
        ],
        scratch_shapes=[pltpu.VMEM((TM, D_MODEL), BF16)],
        compiler_params=_cparams(("parallel", "arbitrary")),
        name="nsa_in_proj",
    )(x, g, w, wg, cos, sin)


def _dual_kernel(*refs, mode, has_bias):
    if has_bias:
        x_ref, g_ref, wa_ref, wb_ref, ba_ref, bb_ref, o_ref, xn_ref = refs
    else:
        x_ref, g_ref, wa_ref, wb_ref, o_ref, xn_ref = refs

    @pl.when(pl.program_id(1) == 0)
    def _():
        xn_ref[...] = _rms_scale(x_ref[...], g_ref[...]).astype(BF16)

    xn = xn_ref[...]
    a = _dot(xn, wa_ref[...])
    b = _dot(xn, wb_ref[...])
    if has_bias:
        a = a + ba_ref[...]
        b = b + bb_ref[...]
    if mode == "swiglu":
        o = jax.nn.silu(a) * b
    else:
        o = a * jax.nn.sigmoid(b)
    o_ref[...] = o.astype(o_ref.dtype)


def _dual_proj(x, g, wa, wb, layer, n_out, b_col0, bias, mode, out_dtype, name):
    m = x.shape[0]
    nb = b_col0 // TN
    in_specs = [
        pl.BlockSpec((TM, D_MODEL), lambda i, j: (i, 0)),
        pl.BlockSpec((1, D_MODEL), lambda i, j: (0, 0)),
        pl.BlockSpec((None, D_MODEL, TN), lambda i, j: (layer, 0, j)),
        pl.BlockSpec((None, D_MODEL, TN), lambda i, j: (layer, 0, j + nb)),
    ]
    args = [x, g, wa, wb]
    if bias is not None:
        in_specs += [pl.BlockSpec((1, TN), lambda i, j: (0, j)), pl.BlockSpec((1, TN), lambda i, j: (0, j + nb))]
        args += [bias, bias]
    return pl.pallas_call(
        functools.partial(_dual_kernel, mode=mode, has_bias=bias is not None),
        grid=(m // TM, n_out // TN),
        in_specs=in_specs,
        out_specs=pl.BlockSpec((TM, TN), lambda i, j: (i, j)),
        out_shape=jax.ShapeDtypeStruct((m, n_out), out_dtype),
        scratch_shapes=[pltpu.VMEM((TM, D_MODEL), BF16)],
        compiler_params=_cparams(("parallel", "arbitrary")),
        name=name,
    )(*args)


def _proj_norm_res_kernel(a_ref, w_ref, g_ref, h_ref, o_ref, acc_ref, *, nk):
    k = pl.program_id(1)
    part = _dot(a_ref[...], w_ref[...])

    @pl.when(k == 0)
    def _():
        acc_ref[...] = part

    @pl.when(k > 0)
    def _():
        acc_ref[...] += part

    @pl.when(k == nk - 1)
    def _():
        o_ref[...] = h_ref[...] + _rms_scale(acc_ref[...], g_ref[...])


def _proj_norm_res(a, w, layer, g, h, nk, name):
    m, kdim = a.shape
    tk = kdim // nk
    return pl.pallas_call(
        functools.partial(_proj_norm_res_kernel, nk=nk),
        grid=(m // TM, nk),
        in_specs=[
            pl.BlockSpec((TM, tk), lambda i, k: (i, k)),
            pl.BlockSpec((None, tk, D_MODEL), lambda i, k: (layer, k, 0)),
            pl.BlockSpec((1, D_MODEL), lambda i, k: (0, 0)),
            pl.BlockSpec((TM, D_MODEL), lambda i, k: (i, 0)),
        ],
        out_specs=pl.BlockSpec((TM, D_MODEL), lambda i, k: (i, 0)),
        out_shape=jax.ShapeDtypeStruct((m, D_MODEL), F32),
        scratch_shapes=[pltpu.VMEM((TM, D_MODEL), F32)],
        compiler_params=_cparams(("parallel", "arbitrary")),
        name=name,
    )(a, w, g, h)


def _ple_kernel(h_ref, hc_ref, p_ref, wg_ref, wp_ref, o_ref, hb_ref):
    @pl.when(pl.program_id(1) == 0)
    def _():
        hb_ref[...] = h_ref[...].astype(BF16)

    gate = jax.nn.sigmoid(_dot(hb_ref[...], wg_ref[...]))
    o_ref[...] = hc_ref[...] + gate * _dot(p_ref[...], wp_ref[...])


def _ple(h, p, wg, wp, layer, name):
    m = h.shape[0]
    return pl.pallas_call(
        _ple_kernel,
        grid=(m // TM, D_MODEL // TN),
        in_specs=[
            pl.BlockSpec((TM, D_MODEL), lambda i, j: (i, 0)),
            pl.BlockSpec((TM, TN), lambda i, j: (i, j)),
            pl.BlockSpec((None, TM, PLE_DIM), lambda i, j: (layer, i, 0)),
            pl.BlockSpec((None, D_MODEL, TN), lambda i, j: (layer, 0, j)),
            pl.BlockSpec((None, PLE_DIM, TN), lambda i, j: (layer, 0, j)),
        ],
        out_specs=pl.BlockSpec((TM, TN), lambda i, j: (i, j)),
        out_shape=jax.ShapeDtypeStruct((m, D_MODEL), F32),
        scratch_shapes=[pltpu.VMEM((TM, D_MODEL), BF16)],
        compiler_params=_cparams(("parallel", "arbitrary")),
        name=name,
    )(h, h, p, wg, wp)


def _chunk_rows(ref, g):
    pieces = [ref[pl.ds(N_KV * s + g, CHUNKS_PER_PAGE, stride=N_KV * CMP_STRIDE), :] for s in range(CMP_STRIDE)]
    return jnp.concatenate(pieces, axis=1)


def _compress_kernel(pt_ref, *refs):
    n = PAGES_PER_STEP
    k_refs, v_refs = refs[0:n], refs[n:2 * n]
    wk_ref, wv_ref, pk_ref, pv_ref = refs[2 * n:2 * n + 4]
    rows_per_g = n * CHUNKS_PER_PAGE
    for srcs, w_ref, out_ref in ((k_refs, wk_ref, pk_ref), (v_refs, wv_ref, pv_ref)):
        x = jnp.concatenate([_chunk_rows(srcs[p], g) for g in range(N_KV) for p in range(n)], axis=0)
        proj = _dot(x.astype(BF16), w_ref[...])
        for g in range(N_KV):
            out_ref[0, g] = proj[g * rows_per_g:(g + 1) * rows_per_g]


def _compress_stage1(page_table, k_src, k_type, v_src, v_type, wk, wv, n_seq, pages_per_seq, name):
    n = PAGES_PER_STEP
    groups = pages_per_seq // n
    n_chunks = pages_per_seq * CHUNKS_PER_PAGE

    def src_spec(slot, typ):
        return pl.BlockSpec(
            (None, PAGE_ROWS, HEAD_DIM), lambda b, t, pt: (typ, pt[b * pages_per_seq + t * n + slot], 0))

    w_spec = pl.BlockSpec((CMP_STRIDE * HEAD_DIM, 2 * HEAD_DIM), lambda b, t, pt: (0, 0))
    out_spec = pl.BlockSpec((1, N_KV, n * CHUNKS_PER_PAGE, 2 * HEAD_DIM), lambda b, t, pt: (b, 0, t, 0))
    out_sds = jax.ShapeDtypeStruct((n_seq, N_KV, n_chunks, 2 * HEAD_DIM), F32)
    return pl.pallas_call(
        _compress_kernel,
        grid_spec=pltpu.PrefetchScalarGridSpec(
            num_scalar_prefetch=1,
            grid=(n_seq, groups),
            in_specs=[src_spec(s, k_type) for s in range(n)] + [src_spec(s, v_type) for s in range(n)]
            + [w_spec, w_spec],
            out_specs=[out_spec, out_spec],
        ),
        out_shape=[out_sds, out_sds],
        compiler_params=_cparams(("parallel", "arbitrary")),
        name=name,
    )(page_table, *([k_src] * n), *([v_src] * n), wk, wv)


def _compress_finish_kernel(p_ref, pn_ref, pe_ref, w1_ref, w2_ref, o_ref):
    p = p_ref[0, 0]
    c = p.shape[0]
    p0 = p[:, :HEAD_DIM]
    p1 = p[:, HEAD_DIM:]
    nxt = pltpu.roll(p1, c - 1, axis=0)
    last = pn_ref[0, 0][0:1, HEAD_DIM:]
    row = lax.broadcasted_iota(jnp.int32, (c, 1), 0)
    nxt = jnp.where(row == c - 1, last, nxt)
    bias = _dot(pe_ref[...].astype(BF16), w1_ref[...])[0:1]
    hid = p0 + nxt + bias
    o_ref[0, 0] = _dot(jax.nn.silu(hid).astype(BF16), w2_ref[...]).astype(o_ref.dtype)


def _compress_finish(p, p_next, pe8, w1, w2, name):
    n_seq, _, c, _ = p.shape
    return pl.pallas_call(
        _compress_finish_kernel,
        grid=(n_seq, N_KV),
        in_specs=[
            pl.BlockSpec((1, 1, c, 2 * HEAD_DIM), lambda b, g: (b, g, 0, 0)),
            pl.BlockSpec((1, 1, 8, 2 * HEAD_DIM), lambda b, g: (b, g, 0, 0)),
            pl.BlockSpec((8, CMP_BLOCK * HEAD_DIM), lambda b, g: (0, 0)),
            pl.BlockSpec((CMP_BLOCK * HEAD_DIM, HEAD_DIM), lambda b, g: (0, 0)),
            pl.BlockSpec((HEAD_DIM, HEAD_DIM), lambda b, g: (0, 0)),
        ],
        out_specs=pl.BlockSpec((1, 1, c, HEAD_DIM), lambda b, g: (b, g, 0, 0)),
        out_shape=jax.ShapeDtypeStruct((n_seq, N_KV, c, HEAD_DIM), BF16),
        compiler_params=_cparams(("parallel", "parallel")),
        name=name,
    )(p, p_next, pe8, w1, w2)


def _gather_kernel(pt_ref, *refs):
    n = PAGES_PER_STEP
    k_refs, v_refs = refs[0:n], refs[n:2 * n]
    kn_ref, vn_ref, ko_ref, vo_ref = refs[2 * n:2 * n + 4]
    t = pl.program_id(1)
    for srcs, new_ref, out_ref in ((k_refs, kn_ref, ko_ref), (v_refs, vn_ref, vo_ref)):
        for p in range(n):
            page = t * n + p
            for g in range(N_KV):
                old = srcs[p][pl.ds(g, PAGE_SIZE, stride=N_KV), :]
                new = new_ref[pl.ds(g, PAGE_SIZE, stride=N_KV), :]
                val = jnp.where(page < N_PAGES, old, jnp.where(page == N_PAGES, new, 0.0))
                out_ref[0, g, p * PAGE_SIZE:(p + 1) * PAGE_SIZE, :] = val.astype(BF16)


def _gather_pages(page_table, k_pool, v_pool, k_new, v_new, n_out_pages):
    n = PAGES_PER_STEP
    groups = n_out_pages // n
    blk = (PAGE_ROWS, HEAD_DIM)

    def src_spec(slot):
        return pl.BlockSpec(
            blk, lambda b, t, pt: (pt[b * N_PAGES + jnp.minimum(t * n + slot, N_PAGES - 1)], 0))

    new_spec = pl.BlockSpec(blk, lambda b, t, pt: (b, 0))
    out_spec = pl.BlockSpec((1, N_KV, n * PAGE_SIZE, HEAD_DIM), lambda b, t, pt: (b, 0, t, 0))
    out_sds = jax.ShapeDtypeStruct((DEC_BATCH, N_KV, n_out_pages * PAGE_SIZE, HEAD_DIM), BF16)
    return pl.pallas_call(
        _gather_kernel,
        grid_spec=pltpu.PrefetchScalarGridSpec(
            num_scalar_prefetch=1,
            grid=(DEC_BATCH, groups),
            in_specs=[src_spec(s) for s in range(n)] * 2 + [new_spec, new_spec],
            out_specs=[out_spec, out_spec],
        ),
        out_shape=[out_sds, out_sds],
        compiler_params=_cparams(("parallel", "arbitrary")),
        name="nsa_gather_sel_pages",
    )(page_table, *([k_pool] * n), *([v_pool] * n), k_new, v_new)


def _masked_softmax2(s, mask):
    s = jnp.where(mask, s, NEG)
    m = jnp.max(s, axis=-1, keepdims=True)
    e = jnp.where(mask, jnp.exp2(s - m), 0.0)
    return e / jnp.maximum(jnp.sum(e, axis=-1, keepdims=True), 1e-30)


def _split3(x):
    parts = []
    rem = x
    for _ in range(3):
        part = rem.astype(BF16)
        parts.append(part)
        rem = rem - part.astype(F32)
    return parts


def _sel_tile(kt, carry, qa, t_row, ks, vs, e_ref, e_row0, tk, masked):
    m, l, acc = carry
    r0 = pl.multiple_of(kt * tk, tk)
    e0 = pl.multiple_of(kt * tk - e_row0, tk)
    ka = jnp.concatenate([ks[pl.ds(r0, tk), :], e_ref[pl.ds(e0, tk), :]], axis=1)
    sc = _dot_nt(qa, ka)
    if masked:
        kpos = r0 + lax.broadcasted_iota(jnp.int32, (1, tk), 1)
        sc = jnp.where(kpos <= t_row, sc, NEG)
    m_new = jnp.maximum(m, jnp.max(sc, axis=-1, keepdims=True))
    alpha = jnp.exp2(m - m_new)
    p = jnp.exp2(sc - m_new)
    l = alpha * l + jnp.sum(p, axis=-1, keepdims=True)
    acc = alpha * acc + _dot(p.astype(BF16), vs[pl.ds(r0, tk), :])
    return m_new, l, acc


def _sel_init(rows):
    return (jnp.full((rows, 1), -jnp.inf, F32), jnp.zeros((rows, 1), F32), jnp.zeros((rows, HEAD_DIM), F32))


def _gated_merge(gt, o_c, o_s, o_w, tq):
    outs = []
    for r in range(GQA_R):
        sl = slice(r * tq, (r + 1) * tq)
        outs.append(gt[:, r:r + 1] * o_c[sl] + gt[:, GQA_R + r:GQA_R + r + 1] * o_s[sl]
                    + gt[:, 2 * GQA_R + r:2 * GQA_R + r + 1] * o_w[sl])
    return jnp.concatenate(outs, axis=1)


def _cmp_to_sel_matrix(nb, nsp):
    n = np.arange(nb)[:, None]
    j = np.arange(nsp)[None, :]
    return ((n >= 4 * j) & (n <= 4 * j + 3)).astype(np.float32) + ((n >= 4 * j - 1) & (n <= 4 * j + 2)).astype(np.float32)


def _block_onehot(n_keys):
    k = np.arange(n_keys)[:, None] // SEL_BLOCK
    return jnp.asarray((k % 128 == np.arange(128)[None, :]).astype(np.float32), BF16)


def _window_bias_prompt():
    q = np.arange(TQ_PROMPT)[:, None]
    kb = np.arange(WBAND_PROMPT)[None, :]
    pats = [kb <= TQ_PROMPT * i + q for i in range(WINDOW // TQ_PROMPT)]
    pats.append((kb > q) & (kb <= q + WINDOW))
    return jnp.asarray(np.where(np.stack(pats), 0.0, NEG), F32)


def _attn_prompt_kernel(q_ref, gate_ref, kc_ref, vc_ref, ks_ref, vs_ref, kw_ref, vw_ref, mmt_ref, e_ref, wb_ref,
                        o_ref, *, nb, ns):
    tq = TQ_PROMPT
    rows = GQA_R * tq
    qi = pl.program_id(2)
    t0 = qi * tq
    qq = q_ref[...]
    q = jnp.concatenate([qq[:, r * HEAD_DIM:(r + 1) * HEAD_DIM] for r in range(GQA_R)], axis=0)
    t_row = t0 + (lax.broadcasted_iota(jnp.int32, (rows, 1), 0) & (tq - 1))

    s = _dot_nt(q, kc_ref[0, 0])
    cmp_end = lax.broadcasted_iota(jnp.int32, (1, nb), 1) * CMP_STRIDE + (CMP_BLOCK - 1)
    p_c = _masked_softmax2(s, cmp_end <= t_row)
    o_c = _dot(p_c.astype(BF16), vc_ref[0, 0])
    imp = p_c[0:tq]
    for r in range(1, GQA_R):
        imp = imp + p_c[r * tq:(r + 1) * tq]

    mmt = mmt_ref[...]
    blk_imp = None
    for part in _split3(imp):
        term = _dot_nt(mmt, part)
        blk_imp = term if blk_imp is None else blk_imp + term
    jb = lax.broadcasted_iota(jnp.int32, (ns, tq), 0)
    cur = lax.shift_right_logical(t0 + lax.broadcasted_iota(jnp.int32, (1, tq), 1), 6)
    forced = (jb == 0) | (jb == cur) | (jb == cur - 1)
    eligible = jb <= cur
    score = jnp.where(forced, jnp.inf, jnp.where(eligible, blk_imp, -jnp.inf))
    groups = ns // 8
    sgrp = [score[8 * v:8 * v + 8] for v in range(groups)]
    jgrp = [jb[8 * v:8 * v + 8] for v in range(groups)]
    rank = [jnp.zeros((8, tq), F32) for _ in range(groups)]
    for i in range(ns):
        ci = score[i:i + 1]
        for v in range(groups):
            if 8 * v > i:
                beats = ci >= sgrp[v]
            elif 8 * v + 7 <= i:
                beats = ci > sgrp[v]
            else:
                beats = (ci > sgrp[v]) | ((ci == sgrp[v]) & (jgrp[v] > i))
            rank[v] = rank[v] + jnp.where(beats, 1.0, 0.0)
    rank = jnp.concatenate(rank, axis=0)
    sel_t = jnp.where((rank < float(TOPK)) & eligible, 1.0, 0.0)
    sel_t = jnp.concatenate([sel_t, jnp.zeros((128 - ns, tq), F32)], axis=0).astype(BF16)
    eye = jnp.where(lax.broadcasted_iota(jnp.int32, (tq, tq), 0) == lax.broadcasted_iota(jnp.int32, (tq, tq), 1),
                    1.0, 0.0).astype(BF16)
    sel = _dot_nt(eye, sel_t)
    negsel = jnp.where(sel > 0.5, 0.0, NEG).astype(BF16)
    qa = jnp.concatenate([q, jnp.concatenate([negsel] * GQA_R, axis=0)], axis=1)

    n_tiles = lax.shift_right_logical(t0 + tq - 1, 9) + 1
    tile = functools.partial(_sel_tile, qa=qa, t_row=t_row, ks=ks_ref, vs=vs_ref, e_ref=e_ref, e_row0=0,
                             tk=TK_ATT)
    carry = lax.fori_loop(0, n_tiles - 1, functools.partial(tile, masked=False), _sel_init(rows))
    _, l, acc = tile(n_tiles - 1, carry, masked=True)
    o_s = acc / jnp.maximum(l, 1e-30)

    wstart = pl.multiple_of(jnp.maximum(t0 - WINDOW, 0), tq)
    s = _dot_nt(q, kw_ref[pl.ds(wstart, WBAND_PROMPT), :])
    bias = wb_ref[jnp.minimum(qi, WINDOW // tq)]
    s = s + jnp.concatenate([bias] * GQA_R, axis=0)
    e = jnp.exp2(s - jnp.max(s, axis=-1, keepdims=True))
    p_w = e / jnp.sum(e, axis=-1, keepdims=True)
    o_w = _dot(p_w.astype(BF16), vw_ref[pl.ds(wstart, WBAND_PROMPT), :])

    o_ref[...] = _gated_merge(gate_ref[0], o_c, o_s, o_w, tq).astype(o_ref.dtype)


def _attn_prompt(qkv, gates_t, kc, vc):
    tq = TQ_PROMPT
    nq = SEQ // tq
    nb = SEQ // CMP_STRIDE
    ns = SEQ // SEL_BLOCK
    col = lambda base, g: (base // HEAD_DIM) + g
    k_sel0, v_sel0 = Q_DIM + 2 * KV_DIM, Q_DIM + 3 * KV_DIM
    k_win0, v_win0 = Q_DIM + 4 * KV_DIM, Q_DIM + 5 * KV_DIM
    seq_spec = lambda base: pl.BlockSpec((SEQ, HEAD_DIM), lambda b, g, i: (b, col(base, g)))
    blk_spec = pl.BlockSpec((1, 1, nb, HEAD_DIM), lambda b, g, i: (b, g, 0, 0))
    mmt = jnp.asarray(_cmp_to_sel_matrix(nb, ns).T, BF16)
    wbias = _window_bias_prompt()
    return pl.pallas_call(
        functools.partial(_attn_prompt_kernel, nb=nb, ns=ns),
        grid=(BATCH, N_KV, nq),
        in_specs=[
            pl.BlockSpec((tq, GQA_R * HEAD_DIM), lambda b, g, i: (b * nq + i, g)),
            pl.BlockSpec((1, tq, HEAD_DIM), lambda b, g, i: (g, b * nq + i, 0)),
            blk_spec, blk_spec,
            seq_spec(k_sel0), seq_spec(v_sel0), seq_spec(k_win0), seq_spec(v_win0),
            pl.BlockSpec((ns, nb), lambda b, g, i: (0, 0)),
            pl.BlockSpec((SEQ, 128), lambda b, g, i: (0, 0)),
            pl.BlockSpec(wbias.shape, lambda b, g, i: (0, 0, 0)),
        ],
        out_specs=pl.BlockSpec((tq, GQA_R * HEAD_DIM), lambda b, g, i: (b * nq + i, g)),
        out_shape=jax.ShapeDtypeStruct((M_ALL, Q_DIM), BF16),
        compiler_params=_cparams(("parallel", "parallel", "arbitrary")),
        name="nsa_attn_prompt",
    )(qkv, gates_t, kc, vc, qkv, qkv, qkv, qkv, mmt, _block_onehot(SEQ), wbias)


TK_SAMPLE = 2048
SAMPLE_KEY_PAGES = 144
SAMPLE_KEYS = SAMPLE_KEY_PAGES * PAGE_SIZE
SAMPLE_NB = PAST_LEN // CMP_STRIDE
SAMPLE_NS = -(-(PAST_LEN + DEC_SEQ) // SEL_BLOCK)
SAMPLE_NSP = 384


def _attn_sample_kernel(q_ref, gate_ref, kc_ref, vc_ref, ks_ref, vs_ref, kw_ref, vw_ref, mmat_ref, e_ref, o_ref):
    tq, nb, ns, nsp = DEC_SEQ, SAMPLE_NB, SAMPLE_NS, SAMPLE_NSP
    rows = GQA_R * tq
    t0 = PAST_LEN
    q = q_ref[0, 0]
    t_row = t0 + (lax.broadcasted_iota(jnp.int32, (rows, 1), 0) & (tq - 1))

    s = _dot_nt(q, kc_ref[0, 0])
    cmp_end = lax.broadcasted_iota(jnp.int32, (1, nb), 1) * CMP_STRIDE + (CMP_BLOCK - 1)
    p_c = _masked_softmax2(s, cmp_end <= t_row)
    o_c = _dot(p_c.astype(BF16), vc_ref[0, 0])
    imp = p_c[0:tq]
    for r in range(1, GQA_R):
        imp = imp + p_c[r * tq:(r + 1) * tq]

    mmat = mmat_ref[...]
    blk_imp = None
    for part in _split3(imp):
        term = _dot(part, mmat)
        blk_imp = term if blk_imp is None else blk_imp + term
    jb = lax.broadcasted_iota(jnp.int32, (tq, nsp), 1)
    cur = lax.shift_right_logical(t0 + lax.broadcasted_iota(jnp.int32, (tq, 1), 0), 6)
    forced = (jb == 0) | (jb == cur) | (jb == cur - 1)
    eligible = jb <= cur
    score = jnp.where(forced, jnp.inf, jnp.where(eligible, blk_imp, -jnp.inf))
    rank = jnp.zeros((tq, nsp), F32)
    for i in range(ns):
        ci = score[:, i:i + 1]
        beats = (ci > score) | ((ci == score) & (jb > i))
        rank = rank + jnp.where(beats, 1.0, 0.0)
    negsel = jnp.where((rank < float(TOPK)) & eligible, 0.0, NEG)
    negsel = jnp.concatenate([negsel] * GQA_R, axis=0).astype(BF16)

    ks, vs = ks_ref.at[0, 0], vs_ref.at[0, 0]
    tk = TK_SAMPLE
    n_tiles = (t0 + tq - 1) // tk + 1
    tiles_per_group = 128 * SEL_BLOCK // tk
    carry = _sel_init(rows)
    for c in range(nsp // 128):
        qa = jnp.concatenate([q, negsel[:, c * 128:(c + 1) * 128]], axis=1)
        lo = c * tiles_per_group
        hi = min(lo + tiles_per_group, n_tiles - 1)
        tile = functools.partial(_sel_tile, qa=qa, t_row=t_row, ks=ks, vs=vs, e_ref=e_ref, e_row0=lo * tk, tk=tk)
        carry = lax.fori_loop(lo, hi, functools.partial(tile, masked=False), carry)
        if lo <= n_tiles - 1 < lo + tiles_per_group:
            carry = tile(n_tiles - 1, carry, masked=True)
    _, l, acc = carry
    o_s = acc / jnp.maximum(l, 1e-30)

    s = _dot_nt(q, kw_ref[0, 0])
    kpos = (PAST_LEN - WINDOW) + lax.broadcasted_iota(jnp.int32, (1, WBAND_SAMPLE), 1)
    p_w = _masked_softmax2(s, (kpos <= t_row) & (kpos > t_row - WINDOW))
    o_w = _dot(p_w.astype(BF16), vw_ref[0, 0])

    o_ref[...] = _gated_merge(gate_ref[0], o_c, o_s, o_w, tq).astype(o_ref.dtype)


def _attn_sample(q_s, gates_s, kc, vc, ks, vs, kw, vw):
    head_spec = lambda n: pl.BlockSpec((1, 1, n, HEAD_DIM), lambda b, g: (b, g, 0, 0))
    mmat = jnp.asarray(_cmp_to_sel_matrix(SAMPLE_NB, SAMPLE_NSP), BF16)
    return pl.pallas_call(
        _attn_sample_kernel,
        grid=(DEC_BATCH, N_KV),
        in_specs=[
            head_spec(GQA_R * DEC_SEQ),
            pl.BlockSpec((1, DEC_SEQ, HEAD_DIM), lambda b, g: (g, b, 0)),
            head_spec(SAMPLE_NB), head_spec(SAMPLE_NB),
            head_spec(SAMPLE_KEYS), head_spec(SAMPLE_KEYS),
            head_spec(WBAND_SAMPLE), head_spec(WBAND_SAMPLE),
            pl.BlockSpec((SAMPLE_NB, SAMPLE_NSP), lambda b, g: (0, 0)),
            pl.BlockSpec((128 * SEL_BLOCK, 128), lambda b, g: (0, 0)),
        ],
        out_specs=pl.BlockSpec((DEC_SEQ, GQA_R * HEAD_DIM), lambda b, g: (b, g)),
        out_shape=jax.ShapeDtypeStruct((M_SAMPLE, Q_DIM), F32),
        compiler_params=_cparams(("parallel", "parallel")),
        name="nsa_attn_sample",
    )(q_s, gates_s, kc, vc, ks, vs, kw, vw, mmat, _block_onehot(128 * SEL_BLOCK))


HALO = 32


def _conv_kernel(u_ref, halo_ref, wdw_ref, bdw_ref, lng_ref, lnb_ref, w2_ref, b2_ref, g_ref, h_ref, o_ref,
                 buf_ref, y_ref, *, t, tiles_per_seq, halo_is_state):
    i = pl.program_id(0)
    if halo_is_state:
        buf_ref[0:HALO, :] = halo_ref[...]
    else:
        first = (i % tiles_per_seq) == 0
        buf_ref[0:HALO, :] = jnp.where(first, 0.0, halo_ref[...])
    buf_ref[HALO:HALO + t, :] = u_ref[...]
    off = HALO - (CONV_W - 1)
    for c in range(D_MODEL // HEAD_DIM):
        cs = slice(c * HEAD_DIM, (c + 1) * HEAD_DIM)
        base = buf_ref[:, cs]
        w = wdw_ref[:, cs]
        y = bdw_ref[:, cs]
        for s in range(8):
            taps = [k for k in range(CONV_W) if (off + k) % 8 == s]
            span = max(off + k - s for k in taps) + t
            shifted = base[s:s + span]
            for k in taps:
                a = off + k - s
                y = y + shifted[a:a + t] * w[k:k + 1]
        y_ref[:, cs] = y
    y = y_ref[...]
    mu = jnp.mean(y, axis=-1, keepdims=True)
    yc = y - mu
    var = jnp.mean(yc * yc, axis=-1, keepdims=True)
    z = yc * lax.rsqrt(var + LN_EPS) * lng_ref[...] + lnb_ref[...]
    mval = _dot(jax.nn.silu(z).astype(BF16), w2_ref[...]) + b2_ref[...]
    o_ref[...] = h_ref[...] + _rms_scale(mval, g_ref[...])


def _conv_tail(u, halo_src, wdw, bdw, lng, lnb, w2, b2, g, h, *, t, n_tiles, row0, out_rows, tiles_per_seq,
               halo_is_state):
    blk0 = row0 // t
    if halo_is_state:
        halo_spec = pl.BlockSpec((HALO, D_MODEL), lambda i: (i, 0))
    else:
        per = t // HALO
        halo_spec = pl.BlockSpec((HALO, D_MODEL), lambda i: (jnp.maximum((blk0 + i) * per - 1, 0), 0))
    vec = pl.BlockSpec((1, D_MODEL), lambda i: (0, 0))
    kern = functools.partial(_conv_kernel, t=t, tiles_per_seq=tiles_per_seq, halo_is_state=halo_is_state)
    return pl.pallas_call(
        kern,
        grid=(n_tiles,),
        in_specs=[
            pl.BlockSpec((t, D_MODEL), lambda i: (blk0 + i, 0)),
            halo_spec,
            pl.BlockSpec((HALO, D_MODEL), lambda i: (0, 0)),
            vec, vec, vec,
            pl.BlockSpec((None, D_MODEL, D_MODEL), lambda i: (0, 0, 0)),
            vec, vec,
            pl.BlockSpec((t, D_MODEL), lambda i: (blk0 + i, 0)),
        ],
        out_specs=pl.BlockSpec((t, D_MODEL), lambda i: (i, 0)),
        out_shape=jax.ShapeDtypeStruct((out_rows, D_MODEL), F32),
        scratch_shapes=[pltpu.VMEM((HALO + t, D_MODEL), F32), pltpu.VMEM((t, D_MODEL), F32)],
        compiler_params=_cparams(("parallel",)),
        name="conv_tail_state" if halo_is_state else "conv_tail",
    )(u, halo_src, wdw, bdw, lng, lnb, w2, b2, g, h)


def _row(v):
    return v.reshape(1, -1).astype(F32)


def _rope_tables():
    half = HEAD_DIM // 2
    pos = jnp.concatenate([jnp.tile(jnp.arange(SEQ), BATCH), jnp.tile(PAST_LEN + jnp.arange(DEC_SEQ), DEC_BATCH)])
    inv = ROPE_THETA ** (-jnp.arange(half, dtype=F32) / half)
    ang = pos.astype(F32)[:, None] * inv[None, :]
    cos, sin = jnp.cos(ang), jnp.sin(ang)
    return jnp.concatenate([cos, cos], axis=1), jnp.concatenate([-sin, sin], axis=1)


def _cmp_weights(w1):
    w1r = w1.reshape(CMP_BLOCK // CMP_STRIDE, CMP_STRIDE * HEAD_DIM, HEAD_DIM)
    return jnp.concatenate([w1r[0], w1r[1]], axis=1).astype(BF16)


def kernel(x_prompt, x_sample, p_prompt, p_sample, page_table, cache_k_cmp, cache_v_cmp, cache_k_sel, cache_v_sel, cache_k_win, cache_v_win, state_conv, norm_mix_pre, norm_mix_post, norm_ffn_pre, norm_ffn_post, ffn_w_gate, ffn_w_up, ffn_w_down, ple_w_proj, ple_w_gate, nsa_w_in, nsa_w_out, nsa_cmp_pe_k, nsa_cmp_w1_k, nsa_cmp_w2_k, nsa_cmp_pe_v, nsa_cmp_w1_v, nsa_cmp_w2_v, conv_w_pw1, conv_b_pw1, conv_w_dw, conv_b_dw, conv_ln_g, conv_ln_b, conv_w_pw2, conv_b_pw2):
    h = jnp.concatenate([x_prompt.reshape(M_PROMPT, D_MODEL), x_sample.reshape(M_SAMPLE, D_MODEL)], axis=0)
    p_all = jnp.concatenate([p_prompt.reshape(DEPTH, M_PROMPT, PLE_DIM),
                             p_sample.reshape(DEPTH, M_SAMPLE, PLE_DIM)], axis=1).astype(BF16)
    pt_flat = page_table.reshape(-1)
    w_gate, w_up, w_down = ffn_w_gate.astype(BF16), ffn_w_up.astype(BF16), ffn_w_down.astype(BF16)
    w_ple_gate, w_ple = ple_w_gate.astype(BF16), ple_w_proj.astype(BF16)

    def residual_tail(h, layer):
        act = _dual_proj(h, _row(norm_ffn_pre[layer]), w_gate, w_up, layer, D_FF, 0, None, "swiglu", BF16,
                         f"ffn_up_{layer}")
        h = _proj_norm_res(act, w_down, layer, _row(norm_ffn_post[layer]), h, 4, f"ffn_down_{layer}")
        return _ple(h, p_all, w_ple_gate, w_ple, layer, f"ple_{layer}")

    n_main = Q_DIM + 6 * KV_DIM
    w_gates = jnp.pad(nsa_w_in[0][:, n_main:], ((0, 0), (0, HEAD_DIM - 3 * N_HEADS))).astype(BF16)
    cos, sin = _rope_tables()
    qkv, kvp, gates = _in_proj(h, _row(norm_mix_pre[0]), nsa_w_in.astype(BF16), w_gates, cos, sin)

    gates_t = gates[:, :3 * N_HEADS].reshape(M_ALL, 3, N_KV, GQA_R).transpose(2, 0, 1, 3).reshape(N_KV, M_ALL, 3 * GQA_R)
    gates_t = jnp.pad(gates_t, ((0, 0), (0, 0), (0, HEAD_DIM - 3 * GQA_R)))

    kv_s = kvp[:, M_PROMPT * N_KV:].reshape(6, DEC_BATCH, DEC_SEQ, N_KV, HEAD_DIM)
    new_page = lambda c: jnp.pad(kv_s[c], ((0, 0), (0, PAGE_SIZE - DEC_SEQ), (0, 0), (0, 0))).reshape(-1, HEAD_DIM)

    wk, wv = _cmp_weights(nsa_cmp_w1_k[0]), _cmp_weights(nsa_cmp_w1_v[0])
    pe_k = jnp.broadcast_to(nsa_cmp_pe_k[0].reshape(1, -1), (8, CMP_BLOCK * HEAD_DIM))
    pe_v = jnp.broadcast_to(nsa_cmp_pe_v[0].reshape(1, -1), (8, CMP_BLOCK * HEAD_DIM))
    w1k, w1v = nsa_cmp_w1_k[0].astype(BF16), nsa_cmp_w1_v[0].astype(BF16)
    w2k, w2v = nsa_cmp_w2_k[0].astype(BF16), nsa_cmp_w2_v[0].astype(BF16)

    pages_prompt = SEQ // PAGE_SIZE
    ident = jnp.arange(BATCH * pages_prompt, dtype=jnp.int32)
    pk_p, pv_p = _compress_stage1(ident, kvp, 0, kvp, 1, wk, wv, BATCH, pages_prompt, "nsa_cmp_prompt")
    zero_next = jnp.zeros((BATCH, N_KV, 8, 2 * HEAD_DIM), F32)
    kc_p = _compress_finish(pk_p, zero_next, pe_k, w1k, w2k, "nsa_cmp_fin_k_prompt")
    vc_p = _compress_finish(pv_p, zero_next, pe_v, w1v, w2v, "nsa_cmp_fin_v_prompt")

    pool3 = lambda c: c.reshape(1, -1, HEAD_DIM)
    pk_s, pv_s = _compress_stage1(pt_flat, pool3(cache_k_cmp), 0, pool3(cache_v_cmp), 0, wk, wv, DEC_BATCH, N_PAGES,
                                  "nsa_cmp_sample")
    pk_n, pv_n = _compress_stage1(jnp.arange(DEC_BATCH, dtype=jnp.int32), new_page(0)[None], 0, new_page(1)[None], 0,
                                  wk, wv, 1, DEC_BATCH, "nsa_cmp_sample_new")
    first_chunk = lambda pn: jnp.pad(
        pn[0].reshape(N_KV, DEC_BATCH, CHUNKS_PER_PAGE, 2 * HEAD_DIM)[:, :, 0:1].transpose(1, 0, 2, 3),
        ((0, 0), (0, 0), (0, 7), (0, 0)))
    kc_s = _compress_finish(pk_s, first_chunk(pk_n), pe_k, w1k, w2k, "nsa_cmp_fin_k_sample")
    vc_s = _compress_finish(pv_s, first_chunk(pv_n), pe_v, w1v, w2v, "nsa_cmp_fin_v_sample")

    o_all = _attn_prompt(qkv, gates_t, kc_p, vc_p)

    pool2 = lambda c: c.reshape(-1, HEAD_DIM)
    ks_s, vs_s = _gather_pages(pt_flat, pool2(cache_k_sel), pool2(cache_v_sel), new_page(2), new_page(3),
                               SAMPLE_KEY_PAGES)
    kw_all = jnp.concatenate([cache_k_win[0], kv_s[4]], axis=1)
    vw_all = jnp.concatenate([cache_v_win[0], kv_s[5]], axis=1)
    band = lambda w: jnp.pad(w.transpose(0, 2, 1, 3),
                             ((0, 0), (0, 0), (0, WBAND_SAMPLE - w.shape[1]), (0, 0))).astype(BF16)
    q_s = qkv[M_PROMPT:, :Q_DIM].reshape(DEC_BATCH, DEC_SEQ, N_KV, GQA_R, HEAD_DIM).transpose(0, 2, 3, 1, 4)
    q_s = q_s.reshape(DEC_BATCH, N_KV, GQA_R * DEC_SEQ, HEAD_DIM)
    o_s = _attn_sample(q_s, gates_t[:, M_PROMPT:], kc_s, vc_s, ks_s, vs_s, band(kw_all), band(vw_all))
    o_all = lax.dynamic_update_slice(o_all, o_s.astype(BF16), (M_PROMPT, 0))

    h = _proj_norm_res(o_all, nsa_w_out.astype(BF16), 0, _row(norm_mix_post[0]), h, 2, "nsa_out_proj")
    h = residual_tail(h, 0)

    w_pw1 = conv_w_pw1.astype(BF16)
    u = _dual_proj(h, _row(norm_mix_pre[1]), w_pw1, w_pw1, 0, D_MODEL, D_MODEL, _row(conv_b_pw1[0]), "glu", F32,
                   "conv_pw1_glu")
    wdw = jnp.pad(conv_w_dw[0], ((0, HALO - CONV_W), (0, 0)))
    conv_args = (wdw, _row(conv_b_dw[0]), _row(conv_ln_g[0]), _row(conv_ln_b[0]), conv_w_pw2.astype(BF16),
                 _row(conv_b_pw2[0]), _row(norm_mix_post[1]))
    t_p = 256
    h_new = _conv_tail(u, u, *conv_args, h, t=t_p, n_tiles=M_PROMPT // t_p, row0=0, out_rows=M_ALL,
                       tiles_per_seq=SEQ // t_p, halo_is_state=False)
    state = jnp.pad(state_conv[0], ((0, 0), (HALO - (CONV_W - 1), 0), (0, 0))).reshape(DEC_BATCH * HALO, D_MODEL)
    h_s = _conv_tail(u, state, *conv_args, h, t=DEC_SEQ, n_tiles=DEC_BATCH, row0=M_PROMPT, out_rows=M_SAMPLE,
                     tiles_per_seq=1, halo_is_state=True)
    h = lax.dynamic_update_slice(h_new, h_s, (M_PROMPT, 0))
    h = residual_tail(h, 1)

    y_prompt = h[:M_PROMPT].reshape(BATCH, SEQ, D_MODEL)
    y_sample = h[M_PROMPT:].reshape(DEC_BATCH, DEC_SEQ, D_MODEL)
    kv_p = kvp[:, :M_PROMPT * N_KV].reshape(6, 1, BATCH, SEQ, N_KV, HEAD_DIM)
    outs = [y_prompt, y_sample]
    for c in range(4):
        outs.append(kv_p[c])
        outs.append(kv_s[c][None])
    w_keep = min(WINDOW, SEQ)
    outs.append(kv_p[4][:, :, SEQ - w_keep:])
    outs.append(kw_all[None, :, -WINDOW:])
    outs.append(kv_p[5][:, :, SEQ - w_keep:])
    outs.append(vw_all[None, :, -WINDOW:])
    n_keep = CONV_W - 1
    outs.append(jnp.stack([u[b * SEQ + SEQ - n_keep:(b + 1) * SEQ] for b in range(BATCH)])[None])
    u_s = u[M_PROMPT:].reshape(DEC_BATCH, DEC_SEQ, D_MODEL)
    outs.append(jnp.concatenate([state_conv[0], u_s], axis=1)[None, :, -n_keep:])
    return tuple(outs)
```

```python
import functools
import math

import numpy as np
import jax
import jax.numpy as jnp
from jax import lax
from jax.experimental import pallas as pl
from jax.experimental.pallas import tpu as pltpu

F32 = jnp.float32
BF16 = jnp.bfloat16

D_MODEL = 2048
BATCH = 2
SEQ = 4096
DEPTH = 2
DEC_BATCH = 8
DEC_SEQ = 8
PAST_LEN = 16384
PAGE_SIZE = 128
N_HEADS = 16
HEAD_DIM = 128
N_KV = 4
GQA_R = 4
Q_DIM = 2048
KV_DIM = 512
CMP_BLOCK = 32
CMP_STRIDE = 16
SEL_BLOCK = 64
TOPK = 16
WINDOW = 512
ROPE_THETA = 10000.0
CONV_W = 31
D_FF = 5632
PLE_DIM = 256
RMS_EPS = 1e-6
LN_EPS = 1e-5
NEG = -1e30

M_PROMPT = BATCH * SEQ
M_SAMPLE = DEC_BATCH * DEC_SEQ
M_ALL = M_PROMPT + M_SAMPLE
N_PAGES = PAST_LEN // PAGE_SIZE
PAGE_ROWS = PAGE_SIZE * N_KV
CHUNKS_PER_PAGE = PAGE_SIZE // CMP_STRIDE
PAGES_PER_STEP = 8

TM = 688
TN = 512
TK_ATT = 512
TQ_PROMPT = 128
ATT_HEADS_PER_STEP = 2
WBAND_PROMPT = WINDOW + TQ_PROMPT
WBAND_SAMPLE = WINDOW + SEL_BLOCK
VMEM_LIMIT = 56 * 1024 * 1024

Q_PRESCALE = HEAD_DIM ** -0.5 * math.log2(math.e)


def _cparams(sem):
    return pltpu.CompilerParams(dimension_semantics=sem, vmem_limit_bytes=VMEM_LIMIT)


def _rms_scale(x, g):
    ms = jnp.mean(x * x, axis=-1, keepdims=True)
    return x * lax.rsqrt(ms + RMS_EPS) * g


def _dot(a, b):
    return jnp.dot(a, b, preferred_element_type=F32)


def _dot_nt(a, b):
    return lax.dot_general(a, b, (((1,), (1,)), ((), ())), preferred_element_type=F32)


N_QKV_TILES = (Q_DIM + 6 * KV_DIM) // TN
N_Q_TILES = Q_DIM // TN


def _in_proj_kernel(x_ref, g_ref, w_ref, wg_ref, cos_ref, sin_ref, qkv_ref, kvp_ref, kvs_ref, gate_ref, xn_ref):
    j = pl.program_id(1)

    @pl.when(j == 0)
    def _():
        xn = _rms_scale(x_ref[...], g_ref[...]).astype(BF16)
        xn_ref[...] = xn
        gate_ref[...] = jax.nn.sigmoid(_dot(xn, wg_ref[...]))

    acc = _dot(xn_ref[...], w_ref[...].astype(BF16))
    is_rope = (j < N_Q_TILES + 1) | (j == N_Q_TILES + 2) | (j == N_Q_TILES + 4)
    post = jnp.where(j < N_Q_TILES, Q_PRESCALE, 1.0)
    cos = jnp.where(is_rope, cos_ref[...], 1.0) * post
    sin = jnp.where(is_rope, sin_ref[...], 0.0) * post
    parts = []
    for h in range(TN // HEAD_DIM):
        a = acc[:, h * HEAD_DIM:(h + 1) * HEAD_DIM]
        parts.append(a * cos + pltpu.roll(a, HEAD_DIM // 2, axis=1) * sin)
    qkv_ref[...] = jnp.concatenate(parts, axis=1).astype(BF16)

    @pl.when(j >= N_Q_TILES)
    def _():
        for g in range(N_KV):
            kvp_ref[pl.ds(g, TM, stride=N_KV), :] = parts[g]
            kvs_ref[pl.ds(g, M_SAMPLE, stride=N_KV), :] = parts[g][TM - M_SAMPLE:]


def _in_proj(x, g, w, wg, cos, sin):
    m = x.shape[0]
    assert m == M_ALL and (m // TM - 1) * TM + (TM - M_SAMPLE) == M_PROMPT
    return pl.pallas_call(
        _in_proj_kernel,
        grid=(m // TM, N_QKV_TILES),
        in_specs=[
            pl.BlockSpec((TM, D_MODEL), lambda i, j: (i, 0)),
            pl.BlockSpec((1, D_MODEL), lambda i, j: (0, 0)),
            pl.BlockSpec((None, D_MODEL, TN), lambda i, j: (0, 0, j)),
            pl.BlockSpec((D_MODEL, HEAD_DIM), lambda i, j: (0, 0)),
            pl.BlockSpec((TM, HEAD_DIM), lambda i, j: (i, 0)),
            pl.BlockSpec((TM, HEAD_DIM), lambda i, j: (i, 0)),
        ],
        out_specs=[
            pl.BlockSpec((TM, TN), lambda i, j: (i, j)),
            pl.BlockSpec((None, TM * N_KV, HEAD_DIM), lambda i, j: (jnp.maximum(j - N_Q_TILES, 0), i, 0)),
            pl.BlockSpec((None, M_SAMPLE * N_KV, HEAD_DIM), lambda i, j: (jnp.maximum(j - N_Q_TILES, 0), i, 0)),
            pl.BlockSpec((TM, HEAD_DIM), lambda i, j: (i, 0)),
        ],
        out_shape=[
            jax.ShapeDtypeStruct((m, Q_DIM + 6 * KV_DIM), BF16),
            jax.ShapeDtypeStruct((6, M_PROMPT * N_KV, HEAD_DIM), F32),
            jax.ShapeDtypeStruct((6, (m // TM) * M_SAMPLE * N_KV, HEAD_DIM), F32),
            jax.ShapeDtypeStruct((m, HEAD_DIM), F32),
        ],
        scratch_shapes=[pltpu.VMEM((TM, D_MODEL), BF16)],
        compiler_params=_cparams(("parallel", "arbitrary")),
        name="nsa_in_proj",
    )(x, g, w, wg, cos, sin)


def _dual_kernel(*refs, mode, has_bias):
    if has_bias:
        x_ref, g_ref, wa_ref, wb_ref, ba_ref, bb_ref, o_ref, xn_ref = refs
    else:
        x_ref, g_ref, wa_ref, wb_ref, o_ref, xn_ref = refs

    @pl.when(pl.program_id(1) == 0)
    def _():
        xn_ref[...] = _rms_scale(x_ref[...], g_ref[...]).astype(BF16)

    xn = xn_ref[...]
    a = _dot(xn, wa_ref[...].astype(BF16))
    b = _dot(xn, wb_ref[...].astype(BF16))
    if has_bias:
        a = a + ba_ref[...]
        b = b + bb_ref[...]
    if mode == "swiglu":
        o = jax.nn.silu(a) * b
    else:
        o = a * jax.nn.sigmoid(b)
    o_ref[...] = o.astype(o_ref.dtype)


def _dual_proj(x, g, wa, wb, layer, n_out, b_col0, bias, mode, out_dtype, name):
    m = x.shape[0]
    nb = b_col0 // TN
    in_specs = [
        pl.BlockSpec((TM, D_MODEL), lambda i, j: (i, 0)),
        pl.BlockSpec((1, D_MODEL), lambda i, j: (0, 0)),
        pl.BlockSpec((None, D_MODEL, TN), lambda i, j: (layer, 0, j)),
        pl.BlockSpec((None, D_MODEL, TN), lambda i, j: (layer, 0, j + nb)),
    ]
    args = [x, g, wa, wb]
    if bias is not None:
        in_specs += [pl.BlockSpec((1, TN), lambda i, j: (0, j)), pl.BlockSpec((1, TN), lambda i, j: (0, j + nb))]
        args += [bias, bias]
    return pl.pallas_call(
        functools.partial(_dual_kernel, mode=mode, has_bias=bias is not None),
        grid=(m // TM, n_out // TN),
        in_specs=in_specs,
        out_specs=pl.BlockSpec((TM, TN), lambda i, j: (i, j)),
        out_shape=jax.ShapeDtypeStruct((m, n_out), out_dtype),
        scratch_shapes=[pltpu.VMEM((TM, D_MODEL), BF16)],
        compiler_params=_cparams(("parallel", "arbitrary")),
        name=name,
    )(*args)


def _proj_norm_res_kernel(a_ref, w_ref, g_ref, h_ref, o_ref, acc_ref, *, nk):
    k = pl.program_id(1)
    part = _dot(a_ref[...], w_ref[...].astype(BF16))

    @pl.when(k == 0)
    def _():
        acc_ref[...] = part

    @pl.when(k > 0)
    def _():
        acc_ref[...] += part

    @pl.when(k == nk - 1)
    def _():
        o_ref[...] = h_ref[...] + _rms_scale(acc_ref[...], g_ref[...])


def _proj_norm_res(a, w, layer, g, h, nk, name):
    m, kdim = a.shape
    tk = kdim // nk
    return pl.pallas_call(
        functools.partial(_proj_norm_res_kernel, nk=nk),
        grid=(m // TM, nk),
        in_specs=[
            pl.BlockSpec((TM, tk), lambda i, k: (i, k)),
            pl.BlockSpec((None, tk, D_MODEL), lambda i, k: (layer, k, 0)),
            pl.BlockSpec((1, D_MODEL), lambda i, k: (0, 0)),
            pl.BlockSpec((TM, D_MODEL), lambda i, k: (i, 0)),
        ],
        out_specs=pl.BlockSpec((TM, D_MODEL), lambda i, k: (i, 0)),
        out_shape=jax.ShapeDtypeStruct((m, D_MODEL), F32),
        scratch_shapes=[pltpu.VMEM((TM, D_MODEL), F32)],
        compiler_params=_cparams(("parallel", "arbitrary")),
        name=name,
    )(a, w, g, h)


def _ple_kernel(h_ref, hc_ref, p_ref, wg_ref, wp_ref, o_ref, *rest, split):
    hb_ref = rest[-1]

    @pl.when(pl.program_id(1) == 0)
    def _():
        hb_ref[...] = h_ref[...].astype(BF16)

    gate = jax.nn.sigmoid(_dot(hb_ref[...], wg_ref[...].astype(BF16)))
    out = hc_ref[...] + gate * _dot(p_ref[...], wp_ref[...].astype(BF16))
    o_ref[...] = out
    if split:
        rest[0][...] = out[TM - M_SAMPLE:]


def _ple(h, p, wg, wp, layer, name, split=False):
    m = h.shape[0]
    out_specs = pl.BlockSpec((TM, TN), lambda i, j: (i, j))
    out_shape = jax.ShapeDtypeStruct((m, D_MODEL), F32)
    if split:
        assert (m // TM - 1) * TM + (TM - M_SAMPLE) == M_PROMPT
        out_specs = [out_specs, pl.BlockSpec((M_SAMPLE, TN), lambda i, j: (i, j))]
        out_shape = [jax.ShapeDtypeStruct((M_PROMPT, D_MODEL), F32),
                     jax.ShapeDtypeStruct(((m // TM) * M_SAMPLE, D_MODEL), F32)]
    return pl.pallas_call(
        functools.partial(_ple_kernel, split=split),
        grid=(m // TM, D_MODEL // TN),
        in_specs=[
            pl.BlockSpec((TM, D_MODEL), lambda i, j: (i, 0)),
            pl.BlockSpec((TM, TN), lambda i, j: (i, j)),
            pl.BlockSpec((None, TM, PLE_DIM), lambda i, j: (layer, i, 0)),
            pl.BlockSpec((None, D_MODEL, TN), lambda i, j: (layer, 0, j)),
            pl.BlockSpec((None, PLE_DIM, TN), lambda i, j: (layer, 0, j)),
        ],
        out_specs=out_specs,
        out_shape=out_shape,
        scratch_shapes=[pltpu.VMEM((TM, D_MODEL), BF16)],
        compiler_params=_cparams(("parallel", "arbitrary")),
        name=name,
    )(h, h, p, wg, wp)


def _chunk_rows(ref, g):
    pieces = [ref[pl.ds(N_KV * s + g, CHUNKS_PER_PAGE, stride=N_KV * CMP_STRIDE), :] for s in range(CMP_STRIDE)]
    return jnp.concatenate(pieces, axis=1)


def _compress_kernel(pt_ref, *refs):
    n = PAGES_PER_STEP
    k_refs, v_refs = refs[0:n], refs[n:2 * n]
    wk_ref, wv_ref, pk_ref, pv_ref = refs[2 * n:2 * n + 4]
    rows_per_g = n * CHUNKS_PER_PAGE
    for srcs, w_ref, out_ref in ((k_refs, wk_ref, pk_ref), (v_refs, wv_ref, pv_ref)):
        x = jnp.concatenate([_chunk_rows(srcs[p], g) for g in range(N_KV) for p in range(n)], axis=0)
        proj = _dot(x.astype(BF16), w_ref[...])
        for g in range(N_KV):
            out_ref[0, g] = proj[g * rows_per_g:(g + 1) * rows_per_g]


def _compress_stage1(page_table, k_src, k_type, v_src, v_type, wk, wv, n_seq, pages_per_seq, name):
    n = PAGES_PER_STEP
    groups = pages_per_seq // n
    n_chunks = pages_per_seq * CHUNKS_PER_PAGE

    def src_spec(slot, typ):
        return pl.BlockSpec(
            (None, PAGE_ROWS, HEAD_DIM), lambda b, t, pt: (typ, pt[b * pages_per_seq + t * n + slot], 0))

    w_spec = pl.BlockSpec((CMP_STRIDE * HEAD_DIM, 2 * HEAD_DIM), lambda b, t, pt: (0, 0))
    out_spec = pl.BlockSpec((1, N_KV, n * CHUNKS_PER_PAGE, 2 * HEAD_DIM), lambda b, t, pt: (b, 0, t, 0))
    out_sds = jax.ShapeDtypeStruct((n_seq, N_KV, n_chunks, 2 * HEAD_DIM), F32)
    return pl.pallas_call(
        _compress_kernel,
        grid_spec=pltpu.PrefetchScalarGridSpec(
            num_scalar_prefetch=1,
            grid=(n_seq, groups),
            in_specs=[src_spec(s, k_type) for s in range(n)] + [src_spec(s, v_type) for s in range(n)]
            + [w_spec, w_spec],
            out_specs=[out_spec, out_spec],
        ),
        out_shape=[out_sds, out_sds],
        compiler_params=_cparams(("parallel", "arbitrary")),
        name=name,
    )(page_table, *([k_src] * n), *([v_src] * n), wk, wv)


def _compress_finish_kernel(p_ref, pn_ref, pe_ref, w1_ref, w2_ref, o_ref):
    p = p_ref[0, 0]
    c = p.shape[0]
    p0 = p[:, :HEAD_DIM]
    p1 = p[:, HEAD_DIM:]
    nxt = pltpu.roll(p1, c - 1, axis=0)
    last = pn_ref[0, 0][0:1, HEAD_DIM:]
    row = lax.broadcasted_iota(jnp.int32, (c, 1), 0)
    nxt = jnp.where(row == c - 1, last, nxt)
    bias = _dot(pe_ref[...].astype(BF16), w1_ref[...])[0:1]
    hid = p0 + nxt + bias
    o_ref[0, 0] = _dot(jax.nn.silu(hid).astype(BF16), w2_ref[...]).astype(o_ref.dtype)


def _compress_finish(p, p_next, pe8, w1, w2, name):
    n_seq, _, c, _ = p.shape
    return pl.pallas_call(
        _compress_finish_kernel,
        grid=(n_seq, N_KV),
        in_specs=[
            pl.BlockSpec((1, 1, c, 2 * HEAD_DIM), lambda b, g: (b, g, 0, 0)),
            pl.BlockSpec((1, 1, 8, 2 * HEAD_DIM), lambda b, g: (b, g, 0, 0)),
            pl.BlockSpec((8, CMP_BLOCK * HEAD_DIM), lambda b, g: (0, 0)),
            pl.BlockSpec((CMP_BLOCK * HEAD_DIM, HEAD_DIM), lambda b, g: (0, 0)),
            pl.BlockSpec((HEAD_DIM, HEAD_DIM), lambda b, g: (0, 0)),
        ],
        out_specs=pl.BlockSpec((1, 1, c, HEAD_DIM), lambda b, g: (b, g, 0, 0)),
        out_shape=jax.ShapeDtypeStruct((n_seq, N_KV, c, HEAD_DIM), BF16),
        compiler_params=_cparams(("parallel", "parallel")),
        name=name,
    )(p, p_next, pe8, w1, w2)


def _gather_kernel(pt_ref, *refs):
    n = PAGES_PER_STEP
    k_refs, v_refs = refs[0:n], refs[n:2 * n]
    kn_ref, vn_ref, ko_ref, vo_ref = refs[2 * n:2 * n + 4]
    t = pl.program_id(1)
    for srcs, new_ref, out_ref in ((k_refs, kn_ref, ko_ref), (v_refs, vn_ref, vo_ref)):
        for p in range(n):
            page = t * n + p
            for g in range(N_KV):
                old = srcs[p][pl.ds(g, PAGE_SIZE, stride=N_KV), :]
                new = new_ref[pl.ds(g, PAGE_SIZE, stride=N_KV), :]
                val = jnp.where(page < N_PAGES, old, jnp.where(page == N_PAGES, new, 0.0))
                out_ref[0, g, p * PAGE_SIZE:(p + 1) * PAGE_SIZE, :] = val.astype(BF16)


def _gather_pages(page_table, k_pool, v_pool, k_new, v_new, n_out_pages):
    n = PAGES_PER_STEP
    groups = n_out_pages // n
    blk = (PAGE_ROWS, HEAD_DIM)

    def src_spec(slot):
        return pl.BlockSpec(
            blk, lambda b, t, pt: (pt[b * N_PAGES + jnp.minimum(t * n + slot, N_PAGES - 1)], 0))

    new_spec = pl.BlockSpec(blk, lambda b, t, pt: (b, 0))
    out_spec = pl.BlockSpec((1, N_KV, n * PAGE_SIZE, HEAD_DIM), lambda b, t, pt: (b, 0, t, 0))
    out_sds = jax.ShapeDtypeStruct((DEC_BATCH, N_KV, n_out_pages * PAGE_SIZE, HEAD_DIM), BF16)
    return pl.pallas_call(
        _gather_kernel,
        grid_spec=pltpu.PrefetchScalarGridSpec(
            num_scalar_prefetch=1,
            grid=(DEC_BATCH, groups),
            in_specs=[src_spec(s) for s in range(n)] * 2 + [new_spec, new_spec],
            out_specs=[out_spec, out_spec],
        ),
        out_shape=[out_sds, out_sds],
        compiler_params=_cparams(("parallel", "arbitrary")),
        name="nsa_gather_sel_pages",
    )(page_table, *([k_pool] * n), *([v_pool] * n), k_new, v_new)


def _masked_softmax2(s, mask):
    s = jnp.where(mask, s, NEG)
    m = jnp.max(s, axis=-1, keepdims=True)
    e = jnp.where(mask, jnp.exp2(s - m), 0.0)
    return e / jnp.maximum(jnp.sum(e, axis=-1, keepdims=True), 1e-30)


def _split3(x):
    parts = []
    rem = x
    for _ in range(3):
        part = rem.astype(BF16)
        parts.append(part)
        rem = rem - part.astype(F32)
    return parts


def _sel_tile(kt, carry, qa, t_row, ks, vs, e_ref, e_row0, tk, masked, lanes=slice(0, HEAD_DIM)):
    m, l, acc = carry
    r0 = pl.multiple_of(kt * tk, tk)
    e0 = pl.multiple_of(kt * tk - e_row0, tk)
    ka = jnp.concatenate([ks[pl.ds(r0, tk), lanes], e_ref[pl.ds(e0, tk), :]], axis=1)
    sc = _dot_nt(qa, ka)
    if masked:
        kpos = r0 + lax.broadcasted_iota(jnp.int32, (1, tk), 1)
        sc = jnp.where(kpos <= t_row, sc, NEG)
    m_new = jnp.maximum(m, jnp.max(sc, axis=-1, keepdims=True))
    alpha = jnp.exp2(m - m_new)
    p = jnp.exp2(sc - m_new)
    l = alpha * l + jnp.sum(p, axis=-1, keepdims=True)
    acc = alpha * acc + _dot(p.astype(BF16), vs[pl.ds(r0, tk), lanes])
    return m_new, l, acc


def _sel_init(rows):
    return (jnp.full((rows, 1), -jnp.inf, F32), jnp.zeros((rows, 1), F32), jnp.zeros((rows, HEAD_DIM), F32))


def _gated_merge(gt, o_c, o_s, o_w, tq):
    outs = []
    for r in range(GQA_R):
        sl = slice(r * tq, (r + 1) * tq)
        outs.append(gt[:, r:r + 1] * o_c[sl] + gt[:, GQA_R + r:GQA_R + r + 1] * o_s[sl]
                    + gt[:, 2 * GQA_R + r:2 * GQA_R + r + 1] * o_w[sl])
    return jnp.concatenate(outs, axis=1)


def _cmp_to_sel_matrix(nb, nsp):
    n = np.arange(nb)[:, None]
    j = np.arange(nsp)[None, :]
    return ((n >= 4 * j) & (n <= 4 * j + 3)).astype(np.float32) + ((n >= 4 * j - 1) & (n <= 4 * j + 2)).astype(np.float32)


def _block_onehot(n_keys):
    k = np.arange(n_keys)[:, None] // SEL_BLOCK
    return jnp.asarray((k % 128 == np.arange(128)[None, :]).astype(np.float32), BF16)


def _window_bias_prompt():
    q = np.arange(TQ_PROMPT)[:, None]
    kb = np.arange(WBAND_PROMPT)[None, :]
    pats = [kb <= TQ_PROMPT * i + q for i in range(WINDOW // TQ_PROMPT)]
    pats.append((kb > q) & (kb <= q + WINDOW))
    return jnp.asarray(np.where(np.stack(pats), 0.0, NEG), F32)


def _attn_prompt_kernel(q_ref, gate_ref, kc_ref, vc_ref, ks_ref, vs_ref, kw_ref, vw_ref, mmt_ref, e_ref, wb_ref,
                        o_ref, *, nb, ns):
    tq = TQ_PROMPT
    rows = GQA_R * tq
    qi = pl.program_id(2)
    t0 = qi * tq
    t_row = t0 + (lax.broadcasted_iota(jnp.int32, (rows, 1), 0) & (tq - 1))
    cmp_end = lax.broadcasted_iota(jnp.int32, (1, nb), 1) * CMP_STRIDE + (CMP_BLOCK - 1)
    jb = lax.broadcasted_iota(jnp.int32, (ns, tq), 0)
    cur = lax.shift_right_logical(t0 + lax.broadcasted_iota(jnp.int32, (1, tq), 1), 6)
    forced = (jb == 0) | (jb == cur) | (jb == cur - 1)
    eligible = jb <= cur
    groups = ns // 8
    jgrp = [jb[8 * v:8 * v + 8] for v in range(groups)]
    eye = jnp.where(lax.broadcasted_iota(jnp.int32, (tq, tq), 0) == lax.broadcasted_iota(jnp.int32, (tq, tq), 1),
                    1.0, 0.0).astype(BF16)
    mmt = mmt_ref[...]

    heads = range(ATT_HEADS_PER_STEP)
    qs, o_cs, qas = [], [], []
    for h in heads:
        qq = q_ref[:, h * GQA_R * HEAD_DIM:(h + 1) * GQA_R * HEAD_DIM]
        q = jnp.concatenate([qq[:, r * HEAD_DIM:(r + 1) * HEAD_DIM] for r in range(GQA_R)], axis=0)

        s = _dot_nt(q, kc_ref[0, h])
        p_c = _masked_softmax2(s, cmp_end <= t_row)
        o_cs.append(_dot(p_c.astype(BF16), vc_ref[0, h]))
        imp = p_c[0:tq]
        for r in range(1, GQA_R):
            imp = imp + p_c[r * tq:(r + 1) * tq]

        blk_imp = None
        for part in _split3(imp):
            term = _dot_nt(mmt, part)
            blk_imp = term if blk_imp is None else blk_imp + term
        score = jnp.where(forced, jnp.inf, jnp.where(eligible, blk_imp, -jnp.inf))
        sgrp = [score[8 * v:8 * v + 8] for v in range(groups)]
        rank = [jnp.zeros((8, tq), F32) for _ in range(groups)]
        for i in range(ns):
            ci = score[i:i + 1]
            for v in range(groups):
                if 8 * v > i:
                    beats = ci >= sgrp[v]
                elif 8 * v + 7 <= i:
                    beats = ci > sgrp[v]
                else:
                    beats = (ci > sgrp[v]) | ((ci == sgrp[v]) & (jgrp[v] > i))
                rank[v] = rank[v] + jnp.where(beats, 1.0, 0.0)
        rank = jnp.concatenate(rank, axis=0)
        sel_t = jnp.where((rank < float(TOPK)) & eligible, 1.0, 0.0)
        sel_t = jnp.concatenate([sel_t, jnp.zeros((128 - ns, tq), F32)], axis=0).astype(BF16)
        sel = _dot_nt(eye, sel_t)
        negsel = jnp.where(sel > 0.5, 0.0, NEG).astype(BF16)
        qs.append(q)
        qas.append(jnp.concatenate([q, jnp.concatenate([negsel] * GQA_R, axis=0)], axis=1))

    n_tiles = lax.shift_right_logical(t0 + tq - 1, TK_ATT.bit_length() - 1) + 1

    def tiles(kt, carries, masked):
        return tuple(
            _sel_tile(kt, carries[h], qa=qas[h], t_row=t_row, ks=ks_ref, vs=vs_ref, e_ref=e_ref, e_row0=0,
                      tk=TK_ATT, masked=masked, lanes=slice(h * HEAD_DIM, (h + 1) * HEAD_DIM))
            for h in heads)

    carries = lax.fori_loop(0, n_tiles - 1, functools.partial(tiles, masked=False),
                            tuple(_sel_init(rows) for _ in heads))
    carries = tiles(n_tiles - 1, carries, masked=True)

    wstart = pl.multiple_of(jnp.maximum(t0 - WINDOW, 0), tq)
    bias = wb_ref[jnp.minimum(qi, WINDOW // tq)]
    bias = jnp.concatenate([bias] * GQA_R, axis=0)
    for h in heads:
        lanes = slice(h * HEAD_DIM, (h + 1) * HEAD_DIM)
        _, l, acc = carries[h]
        o_s = acc / jnp.maximum(l, 1e-30)
        s = _dot_nt(qs[h], kw_ref[pl.ds(wstart, WBAND_PROMPT), lanes]) + bias
        e = jnp.exp2(s - jnp.max(s, axis=-1, keepdims=True))
        p_w = e / jnp.sum(e, axis=-1, keepdims=True)
        o_w = _dot(p_w.astype(BF16), vw_ref[pl.ds(wstart, WBAND_PROMPT), lanes])
        o_ref[:, h * GQA_R * HEAD_DIM:(h + 1) * GQA_R * HEAD_DIM] = _gated_merge(
            gate_ref[h], o_cs[h], o_s, o_w, tq).astype(o_ref.dtype)


def _attn_prompt(qkv, gates_t, kc, vc):
    tq = TQ_PROMPT
    hp = ATT_HEADS_PER_STEP
    nq = SEQ // tq
    nb = SEQ // CMP_STRIDE
    ns = SEQ // SEL_BLOCK
    k_sel0, v_sel0 = Q_DIM + 2 * KV_DIM, Q_DIM + 3 * KV_DIM
    k_win0, v_win0 = Q_DIM + 4 * KV_DIM, Q_DIM + 5 * KV_DIM
    seq_spec = lambda base: pl.BlockSpec((SEQ, hp * HEAD_DIM), lambda b, g, i: (b, base // (hp * HEAD_DIM) + g))
    blk_spec = pl.BlockSpec((1, hp, nb, HEAD_DIM), lambda b, g, i: (b, g, 0, 0))
    mmt = jnp.asarray(_cmp_to_sel_matrix(nb, ns).T, BF16)
    wbias = _window_bias_prompt()
    return pl.pallas_call(
        functools.partial(_attn_prompt_kernel, nb=nb, ns=ns),
        grid=(BATCH, N_KV // hp, nq),
        in_specs=[
            pl.BlockSpec((tq, hp * GQA_R * HEAD_DIM), lambda b, g, i: (b * nq + i, g)),
            pl.BlockSpec((hp, tq, HEAD_DIM), lambda b, g, i: (g, b * nq + i, 0)),
            blk_spec, blk_spec,
            seq_spec(k_sel0), seq_spec(v_sel0), seq_spec(k_win0), seq_spec(v_win0),
            pl.BlockSpec((ns, nb), lambda b, g, i: (0, 0)),
            pl.BlockSpec((SEQ, 128), lambda b, g, i: (0, 0)),
            pl.BlockSpec(wbias.shape, lambda b, g, i: (0, 0, 0)),
        ],
        out_specs=pl.BlockSpec((tq, hp * GQA_R * HEAD_DIM), lambda b, g, i: (b * nq + i, g)),
        out_shape=jax.ShapeDtypeStruct((M_ALL, Q_DIM), BF16),
        compiler_params=_cparams(("parallel", "parallel", "arbitrary")),
        name="nsa_attn_prompt",
    )(qkv, gates_t, kc, vc, qkv, qkv, qkv, qkv, mmt, _block_onehot(SEQ), wbias)


TK_SAMPLE = 2048
SAMPLE_KEY_PAGES = 144
SAMPLE_KEYS = SAMPLE_KEY_PAGES * PAGE_SIZE
SAMPLE_NB = PAST_LEN // CMP_STRIDE
SAMPLE_NS = -(-(PAST_LEN + DEC_SEQ) // SEL_BLOCK)
SAMPLE_NSP = 384


def _attn_sample_kernel(q_ref, gate_ref, kc_ref, vc_ref, ks_ref, vs_ref, kw_ref, vw_ref, mmat_ref, e_ref, o_ref):
    tq, nb, ns, nsp = DEC_SEQ, SAMPLE_NB, SAMPLE_NS, SAMPLE_NSP
    rows = GQA_R * tq
    t0 = PAST_LEN
    q = q_ref[0, 0]
    t_row = t0 + (lax.broadcasted_iota(jnp.int32, (rows, 1), 0) & (tq - 1))

    s = _dot_nt(q, kc_ref[0, 0])
    cmp_end = lax.broadcasted_iota(jnp.int32, (1, nb), 1) * CMP_STRIDE + (CMP_BLOCK - 1)
    p_c = _masked_softmax2(s, cmp_end <= t_row)
    o_c = _dot(p_c.astype(BF16), vc_ref[0, 0])
    imp = p_c[0:tq]
    for r in range(1, GQA_R):
        imp = imp + p_c[r * tq:(r + 1) * tq]

    mmat = mmat_ref[...]
    blk_imp = None
    for part in _split3(imp):
        term = _dot(part, mmat)
        blk_imp = term if blk_imp is None else blk_imp + term
    jb = lax.broadcasted_iota(jnp.int32, (tq, nsp), 1)
    cur = lax.shift_right_logical(t0 + lax.broadcasted_iota(jnp.int32, (tq, 1), 0), 6)
    forced = (jb == 0) | (jb == cur) | (jb == cur - 1)
    eligible = jb <= cur
    score = jnp.where(forced, jnp.inf, jnp.where(eligible, blk_imp, -jnp.inf))
    rank = jnp.zeros((tq, nsp), F32)
    for i in range(ns):
        ci = score[:, i:i + 1]
        beats = (ci > score) | ((ci == score) & (jb > i))
        rank = rank + jnp.where(beats, 1.0, 0.0)
    negsel = jnp.where((rank < float(TOPK)) & eligible, 0.0, NEG)
    negsel = jnp.concatenate([negsel] * GQA_R, axis=0).astype(BF16)

    ks, vs = ks_ref.at[0, 0], vs_ref.at[0, 0]
    tk = TK_SAMPLE
    n_tiles = (t0 + tq - 1) // tk + 1
    tiles_per_group = 128 * SEL_BLOCK // tk
    carry = _sel_init(rows)
    for c in range(nsp // 128):
        qa = jnp.concatenate([q, negsel[:, c * 128:(c + 1) * 128]], axis=1)
        lo = c * tiles_per_group
        hi = min(lo + tiles_per_group, n_tiles - 1)
        tile = functools.partial(_sel_tile, qa=qa, t_row=t_row, ks=ks, vs=vs, e_ref=e_ref, e_row0=lo * tk, tk=tk)
        carry = lax.fori_loop(lo, hi, functools.partial(tile, masked=False), carry)
        if lo <= n_tiles - 1 < lo + tiles_per_group:
            carry = tile(n_tiles - 1, carry, masked=True)
    _, l, acc = carry
    o_s = acc / jnp.maximum(l, 1e-30)

    s = _dot_nt(q, kw_ref[0, 0])
    kpos = (PAST_LEN - WINDOW) + lax.broadcasted_iota(jnp.int32, (1, WBAND_SAMPLE), 1)
    p_w = _masked_softmax2(s, (kpos <= t_row) & (kpos > t_row - WINDOW))
    o_w = _dot(p_w.astype(BF16), vw_ref[0, 0])

    o_ref[...] = _gated_merge(gate_ref[0], o_c, o_s, o_w, tq).astype(o_ref.dtype)


def _attn_sample(q_s, gates_s, kc, vc, ks, vs, kw, vw):
    head_spec = lambda n: pl.BlockSpec((1, 1, n, HEAD_DIM), lambda b, g: (b, g, 0, 0))
    mmat = jnp.asarray(_cmp_to_sel_matrix(SAMPLE_NB, SAMPLE_NSP), BF16)
    return pl.pallas_call(
        _attn_sample_kernel,
        grid=(DEC_BATCH, N_KV),
        in_specs=[
            head_spec(GQA_R * DEC_SEQ),
            pl.BlockSpec((1, DEC_SEQ, HEAD_DIM), lambda b, g: (g, b, 0)),
            head_spec(SAMPLE_NB), head_spec(SAMPLE_NB),
            head_spec(SAMPLE_KEYS), head_spec(SAMPLE_KEYS),
            head_spec(WBAND_SAMPLE), head_spec(WBAND_SAMPLE),
            pl.BlockSpec((SAMPLE_NB, SAMPLE_NSP), lambda b, g: (0, 0)),
            pl.BlockSpec((128 * SEL_BLOCK, 128), lambda b, g: (0, 0)),
        ],
        out_specs=pl.BlockSpec((DEC_SEQ, GQA_R * HEAD_DIM), lambda b, g: (b, g)),
        out_shape=jax.ShapeDtypeStruct((M_SAMPLE, Q_DIM), F32),
        compiler_params=_cparams(("parallel", "parallel")),
        name="nsa_attn_sample",
    )(q_s, gates_s, kc, vc, ks, vs, kw, vw, mmat, _block_onehot(128 * SEL_BLOCK))


HALO = 32


def _conv_kernel(u_ref, halo_ref, wdw_ref, bdw_ref, lng_ref, lnb_ref, w2_ref, b2_ref, g_ref, h_ref, o_ref,
                 buf_ref, y_ref, *, t, tiles_per_seq, halo_is_state):
    i = pl.program_id(0)
    if halo_is_state:
        buf_ref[0:HALO, :] = halo_ref[...]
    else:
        first = (i % tiles_per_seq) == 0
        buf_ref[0:HALO, :] = jnp.where(first, 0.0, halo_ref[...])
    buf_ref[HALO:HALO + t, :] = u_ref[...]
    off = HALO - (CONV_W - 1)
    for c in range(D_MODEL // HEAD_DIM):
        cs = slice(c * HEAD_DIM, (c + 1) * HEAD_DIM)
        base = buf_ref[:, cs]
        w = wdw_ref[:, cs]
        y = bdw_ref[:, cs]
        n_rows = HALO + t
        for s in range(8):
            taps = [k for k in range(CONV_W) if (off + k) % 8 == s]
            shifted = base if s == 0 else pltpu.roll(base, n_rows - s, axis=0)
            for k in taps:
                a = off + k - s
                y = y + shifted[a:a + t] * w[k:k + 1]
        y_ref[:, cs] = y
    y = y_ref[...]
    mu = jnp.mean(y, axis=-1, keepdims=True)
    yc = y - mu
    var = jnp.mean(yc * yc, axis=-1, keepdims=True)
    z = yc * lax.rsqrt(var + LN_EPS) * lng_ref[...] + lnb_ref[...]
    mval = _dot(jax.nn.silu(z).astype(BF16), w2_ref[...]) + b2_ref[...]
    o_ref[...] = h_ref[...] + _rms_scale(mval, g_ref[...])


def _conv_tail(u, halo_src, wdw, bdw, lng, lnb, w2, b2, g, h, *, t, n_tiles, row0, out_rows, tiles_per_seq,
               halo_is_state):
    blk0 = row0 // t
    if halo_is_state:
        halo_spec = pl.BlockSpec((HALO, D_MODEL), lambda i: (i, 0))
    else:
        per = t // HALO
        halo_spec = pl.BlockSpec((HALO, D_MODEL), lambda i: (jnp.maximum((blk0 + i) * per - 1, 0), 0))
    vec = pl.BlockSpec((1, D_MODEL), lambda i: (0, 0))
    kern = functools.partial(_conv_kernel, t=t, tiles_per_seq=tiles_per_seq, halo_is_state=halo_is_state)
    return pl.pallas_call(
        kern,
        grid=(n_tiles,),
        in_specs=[
            pl.BlockSpec((t, D_MODEL), lambda i: (blk0 + i, 0)),
            halo_spec,
            pl.BlockSpec((HALO, D_MODEL), lambda i: (0, 0)),
            vec, vec, vec,
            pl.BlockSpec((None, D_MODEL, D_MODEL), lambda i: (0, 0, 0)),
            vec, vec,
            pl.BlockSpec((t, D_MODEL), lambda i: (blk0 + i, 0)),
        ],
        out_specs=pl.BlockSpec((t, D_MODEL), lambda i: (i, 0)),
        out_shape=jax.ShapeDtypeStruct((out_rows, D_MODEL), F32),
        scratch_shapes=[pltpu.VMEM((HALO + t, D_MODEL), F32), pltpu.VMEM((t, D_MODEL), F32)],
        compiler_params=_cparams(("parallel",)),
        name="conv_tail_state" if halo_is_state else "conv_tail",
    )(u, halo_src, wdw, bdw, lng, lnb, w2, b2, g, h)


def _row(v):
    return v.reshape(1, -1).astype(F32)


def _rope_tables():
    half = HEAD_DIM // 2
    pos = jnp.concatenate([jnp.tile(jnp.arange(SEQ), BATCH), jnp.tile(PAST_LEN + jnp.arange(DEC_SEQ), DEC_BATCH)])
    inv = ROPE_THETA ** (-jnp.arange(half, dtype=F32) / half)
    ang = pos.astype(F32)[:, None] * inv[None, :]
    cos, sin = jnp.cos(ang), jnp.sin(ang)
    return jnp.concatenate([cos, cos], axis=1), jnp.concatenate([-sin, sin], axis=1)


def _cmp_weights(w1):
    w1r = w1.reshape(CMP_BLOCK // CMP_STRIDE, CMP_STRIDE * HEAD_DIM, HEAD_DIM)
    return jnp.concatenate([w1r[0], w1r[1]], axis=1).astype(BF16)


def kernel(x_prompt, x_sample, p_prompt, p_sample, page_table, cache_k_cmp, cache_v_cmp, cache_k_sel, cache_v_sel, cache_k_win, cache_v_win, state_conv, norm_mix_pre, norm_mix_post, norm_ffn_pre, norm_ffn_post, ffn_w_gate, ffn_w_up, ffn_w_down, ple_w_proj, ple_w_gate, nsa_w_in, nsa_w_out, nsa_cmp_pe_k, nsa_cmp_w1_k, nsa_cmp_w2_k, nsa_cmp_pe_v, nsa_cmp_w1_v, nsa_cmp_w2_v, conv_w_pw1, conv_b_pw1, conv_w_dw, conv_b_dw, conv_ln_g, conv_ln_b, conv_w_pw2, conv_b_pw2):
    h = jnp.concatenate([x_prompt.reshape(M_PROMPT, D_MODEL), x_sample.reshape(M_SAMPLE, D_MODEL)], axis=0)
    p_all = jnp.concatenate([p_prompt.reshape(DEPTH, M_PROMPT, PLE_DIM),
                             p_sample.reshape(DEPTH, M_SAMPLE, PLE_DIM)], axis=1).astype(BF16)
    pt_flat = page_table.reshape(-1)
    w_gate, w_up, w_down = ffn_w_gate, ffn_w_up, ffn_w_down.astype(BF16)
    w_ple_gate, w_ple = ple_w_gate, ple_w_proj.astype(BF16)

    def residual_tail(h, layer, split=False):
        act = _dual_proj(h, _row(norm_ffn_pre[layer]), w_gate, w_up, layer, D_FF, 0, None, "swiglu", BF16,
                         f"ffn_up_{layer}")
        h = _proj_norm_res(act, w_down, layer, _row(norm_ffn_post[layer]), h, 4, f"ffn_down_{layer}")
        return _ple(h, p_all, w_ple_gate, w_ple, layer, f"ple_{layer}", split=split)

    n_main = Q_DIM + 6 * KV_DIM
    w_gates = jnp.pad(nsa_w_in[0][:, n_main:], ((0, 0), (0, HEAD_DIM - 3 * N_HEADS))).astype(BF16)
    cos, sin = _rope_tables()
    qkv, kvp, kvs, gates = _in_proj(h, _row(norm_mix_pre[0]), nsa_w_in, w_gates, cos, sin)

    gates_t = gates[:, :3 * N_HEADS].reshape(M_ALL, 3, N_KV, GQA_R).transpose(2, 0, 1, 3).reshape(N_KV, M_ALL, 3 * GQA_R)
    gates_t = jnp.pad(gates_t, ((0, 0), (0, 0), (0, HEAD_DIM - 3 * GQA_R)))

    kv_s = kvs[:, -M_SAMPLE * N_KV:].reshape(6, DEC_BATCH, DEC_SEQ, N_KV, HEAD_DIM)
    new_page = lambda c: jnp.pad(kv_s[c], ((0, 0), (0, PAGE_SIZE - DEC_SEQ), (0, 0), (0, 0))).reshape(-1, HEAD_DIM)

    wk, wv = _cmp_weights(nsa_cmp_w1_k[0]), _cmp_weights(nsa_cmp_w1_v[0])
    pe_k = jnp.broadcast_to(nsa_cmp_pe_k[0].reshape(1, -1), (8, CMP_BLOCK * HEAD_DIM))
    pe_v = jnp.broadcast_to(nsa_cmp_pe_v[0].reshape(1, -1), (8, CMP_BLOCK * HEAD_DIM))
    w1k, w1v = nsa_cmp_w1_k[0].astype(BF16), nsa_cmp_w1_v[0].astype(BF16)
    w2k, w2v = nsa_cmp_w2_k[0].astype(BF16), nsa_cmp_w2_v[0].astype(BF16)

    pages_prompt = SEQ // PAGE_SIZE
    ident = jnp.arange(BATCH * pages_prompt, dtype=jnp.int32)
    pk_p, pv_p = _compress_stage1(ident, kvp, 0, kvp, 1, wk, wv, BATCH, pages_prompt, "nsa_cmp_prompt")
    zero_next = jnp.zeros((BATCH, N_KV, 8, 2 * HEAD_DIM), F32)
    kc_p = _compress_finish(pk_p, zero_next, pe_k, w1k, w2k, "nsa_cmp_fin_k_prompt")
    vc_p = _compress_finish(pv_p, zero_next, pe_v, w1v, w2v, "nsa_cmp_fin_v_prompt")

    pool3 = lambda c: c.reshape(1, -1, HEAD_DIM)
    pk_s, pv_s = _compress_stage1(pt_flat, pool3(cache_k_cmp), 0, pool3(cache_v_cmp), 0, wk, wv, DEC_BATCH, N_PAGES,
                                  "nsa_cmp_sample")
    pk_n, pv_n = _compress_stage1(jnp.arange(DEC_BATCH, dtype=jnp.int32), new_page(0)[None], 0, new_page(1)[None], 0,
                                  wk, wv, 1, DEC_BATCH, "nsa_cmp_sample_new")
    first_chunk = lambda pn: jnp.pad(
        pn[0].reshape(N_KV, DEC_BATCH, CHUNKS_PER_PAGE, 2 * HEAD_DIM)[:, :, 0:1].transpose(1, 0, 2, 3),
        ((0, 0), (0, 0), (0, 7), (0, 0)))
    kc_s = _compress_finish(pk_s, first_chunk(pk_n), pe_k, w1k, w2k, "nsa_cmp_fin_k_sample")
    vc_s = _compress_finish(pv_s, first_chunk(pv_n), pe_v, w1v, w2v, "nsa_cmp_fin_v_sample")

    o_all = _attn_prompt(qkv, gates_t, kc_p, vc_p)

    pool2 = lambda c: c.reshape(-1, HEAD_DIM)
    ks_s, vs_s = _gather_pages(pt_flat, pool2(cache_k_sel), pool2(cache_v_sel), new_page(2), new_page(3),
                               SAMPLE_KEY_PAGES)
    kw_all = jnp.concatenate([cache_k_win[0], kv_s[4]], axis=1)
    vw_all = jnp.concatenate([cache_v_win[0], kv_s[5]], axis=1)
    band = lambda w: jnp.pad(w.transpose(0, 2, 1, 3),
                             ((0, 0), (0, 0), (0, WBAND_SAMPLE - w.shape[1]), (0, 0))).astype(BF16)
    q_s = qkv[M_PROMPT:, :Q_DIM].reshape(DEC_BATCH, DEC_SEQ, N_KV, GQA_R, HEAD_DIM).transpose(0, 2, 3, 1, 4)
    q_s = q_s.reshape(DEC_BATCH, N_KV, GQA_R * DEC_SEQ, HEAD_DIM)
    o_s = _attn_sample(q_s, gates_t[:, M_PROMPT:], kc_s, vc_s, ks_s, vs_s, band(kw_all), band(vw_all))
    o_all = lax.dynamic_update_slice(o_all, o_s.astype(BF16), (M_PROMPT, 0))

    h = _proj_norm_res(o_all, nsa_w_out.astype(BF16), 0, _row(norm_mix_post[0]), h, 2, "nsa_out_proj")
    h = residual_tail(h, 0)

    w_pw1 = conv_w_pw1
    u = _dual_proj(h, _row(norm_mix_pre[1]), w_pw1, w_pw1, 0, D_MODEL, D_MODEL, _row(conv_b_pw1[0]), "glu", F32,
                   "conv_pw1_glu")
    wdw = jnp.pad(conv_w_dw[0], ((0, HALO - CONV_W), (0, 0)))
    conv_args = (wdw, _row(conv_b_dw[0]), _row(conv_ln_g[0]), _row(conv_ln_b[0]), conv_w_pw2.astype(BF16),
                 _row(conv_b_pw2[0]), _row(norm_mix_post[1]))
    t_p = 256
    h_new = _conv_tail(u, u, *conv_args, h, t=t_p, n_tiles=M_PROMPT // t_p, row0=0, out_rows=M_ALL,
                       tiles_per_seq=SEQ // t_p, halo_is_state=False)
    state = jnp.pad(state_conv[0], ((0, 0), (HALO - (CONV_W - 1), 0), (0, 0))).reshape(DEC_BATCH * HALO, D_MODEL)
    h_s = _conv_tail(u, state, *conv_args, h, t=DEC_SEQ, n_tiles=DEC_BATCH, row0=M_PROMPT, out_rows=M_SAMPLE,
                     tiles_per_seq=1, halo_is_state=True)
    h = lax.dynamic_update_slice(h_new, h_s, (M_PROMPT, 0))
    y_p, y_s = residual_tail(h, 1, split=True)

    y_prompt = y_p.reshape(BATCH, SEQ, D_MODEL)
    y_sample = y_s[-M_SAMPLE:].reshape(DEC_BATCH, DEC_SEQ, D_MODEL)
    kv_p = kvp.reshape(6, 1, BATCH, SEQ, N_KV, HEAD_DIM)
    outs = [y_prompt, y_sample]
    for c in range(4):
        outs.append(kv_p[c])
        outs.append(kv_s[c][None])
    w_keep = min(WINDOW, SEQ)
    outs.append(kv_p[4][:, :, SEQ - w_keep:])
    outs.append(kw_all[None, :, -WINDOW:])
    outs.append(kv_p[5][:, :, SEQ - w_keep:])
    outs.append(vw_all[None, :, -WINDOW:])
    n_keep = CONV_W - 1
    outs.append(jnp.stack([u[b * SEQ + SEQ - n_keep:(b + 1) * SEQ] for b in range(BATCH)])[None])
    u_s = u[M_PROMPT:].reshape(DEC_BATCH, DEC_SEQ, D_MODEL)
    outs.append(jnp.concatenate([state_conv[0], u_s], axis=1)[None, :, -n_keep:])
    return tuple(outs)
```

```python
import functools
import math

import numpy as np
import jax
import jax.numpy as jnp
from jax import lax
from jax.experimental import pallas as pl
from jax.experimental.pallas import tpu as pltpu

F32 = jnp.float32
BF16 = jnp.bfloat16

D_MODEL = 2048
BATCH = 2
SEQ = 4096
DEPTH = 2
DEC_BATCH = 8
DEC_SEQ = 8
PAST_LEN = 16384
PAGE_SIZE = 128
N_HEADS = 16
HEAD_DIM = 128
N_KV = 4
GQA_R = 4
Q_DIM = 2048
KV_DIM = 512
CMP_BLOCK = 32
CMP_STRIDE = 16
SEL_BLOCK = 64
TOPK = 16
WINDOW = 512
ROPE_THETA = 10000.0
CONV_W = 31
D_FF = 5632
PLE_DIM = 256
RMS_EPS = 1e-6
LN_EPS = 1e-5
NEG = -1e30

M_PROMPT = BATCH * SEQ
M_SAMPLE = DEC_BATCH * DEC_SEQ
M_ALL = M_PROMPT + M_SAMPLE
N_PAGES = PAST_LEN // PAGE_SIZE
PAGE_ROWS = PAGE_SIZE * N_KV
CHUNKS_PER_PAGE = PAGE_SIZE // CMP_STRIDE
PAGES_PER_STEP = 8

TM = 688
TN = 512
TK_ATT = 512
TQ_PROMPT = 128
ATT_HEADS_PER_STEP = 2
WBAND_PROMPT = WINDOW + TQ_PROMPT
WBAND_SAMPLE = WINDOW + SEL_BLOCK
VMEM_LIMIT = 56 * 1024 * 1024

Q_PRESCALE = HEAD_DIM ** -0.5 * math.log2(math.e)


def _cparams(sem):
    return pltpu.CompilerParams(dimension_semantics=sem, vmem_limit_bytes=VMEM_LIMIT)


def _rms_scale(x, g):
    ms = jnp.mean(x * x, axis=-1, keepdims=True)
    return x * lax.rsqrt(ms + RMS_EPS) * g


def _dot(a, b):
    return jnp.dot(a, b, preferred_element_type=F32)


def _dot_nt(a, b):
    return lax.dot_general(a, b, (((1,), (1,)), ((), ())), preferred_element_type=F32)


N_QKV_TILES = (Q_DIM + 6 * KV_DIM) // TN
N_Q_TILES = Q_DIM // TN


def _in_proj_kernel(x_ref, g_ref, w_ref, wg_ref, cos_ref, sin_ref, qkv_ref, kvp_ref, kvs_ref, gate_ref, xn_ref):
    j = pl.program_id(1)

    @pl.when(j == 0)
    def _():
        xn = _rms_scale(x_ref[...], g_ref[...]).astype(BF16)
        xn_ref[...] = xn
        gate_ref[...] = jax.nn.sigmoid(_dot(xn, wg_ref[...]))

    acc = _dot(xn_ref[...], w_ref[...])
    is_rope = (j < N_Q_TILES + 1) | (j == N_Q_TILES + 2) | (j == N_Q_TILES + 4)
    post = jnp.where(j < N_Q_TILES, Q_PRESCALE, 1.0)
    cos = jnp.where(is_rope, cos_ref[...], 1.0) * post
    sin = jnp.where(is_rope, sin_ref[...], 0.0) * post
    parts = []
    for h in range(TN // HEAD_DIM):
        a = acc[:, h * HEAD_DIM:(h + 1) * HEAD_DIM]
        parts.append(a * cos + pltpu.roll(a, HEAD_DIM // 2, axis=1) * sin)
    qkv_ref[...] = jnp.concatenate(parts, axis=1).astype(BF16)

    @pl.when(j >= N_Q_TILES)
    def _():
        for g in range(N_KV):
            kvp_ref[pl.ds(g, TM, stride=N_KV), :] = parts[g]
            kvs_ref[pl.ds(g, M_SAMPLE, stride=N_KV), :] = parts[g][TM - M_SAMPLE:]


def _in_proj(x, g, w, wg, cos, sin):
    m = x.shape[0]
    assert m == M_ALL and (m // TM - 1) * TM + (TM - M_SAMPLE) == M_PROMPT
    return pl.pallas_call(
        _in_proj_kernel,
        grid=(m // TM, N_QKV_TILES),
        in_specs=[
            pl.BlockSpec((TM, D_MODEL), lambda i, j: (i, 0)),
            pl.BlockSpec((1, D_MODEL), lambda i, j: (0, 0)),
            pl.BlockSpec((None, D_MODEL, TN), lambda i, j: (0, 0, j)),
            pl.BlockSpec((D_MODEL, HEAD_DIM), lambda i, j: (0, 0)),
            pl.BlockSpec((TM, HEAD_DIM), lambda i, j: (i, 0)),
            pl.BlockSpec((TM, HEAD_DIM), lambda i, j: (i, 0)),
        ],
        out_specs=[
            pl.BlockSpec((TM, TN), lambda i, j: (i, j)),
            pl.BlockSpec((None, TM * N_KV, HEAD_DIM), lambda i, j: (jnp.maximum(j - N_Q_TILES, 0), i, 0)),
            pl.BlockSpec((None, M_SAMPLE * N_KV, HEAD_DIM), lambda i, j: (jnp.maximum(j - N_Q_TILES, 0), i, 0)),
            pl.BlockSpec((TM, HEAD_DIM), lambda i, j: (i, 0)),
        ],
        out_shape=[
            jax.ShapeDtypeStruct((m, Q_DIM + 6 * KV_DIM), BF16),
            jax.ShapeDtypeStruct((6, M_PROMPT * N_KV, HEAD_DIM), F32),
            jax.ShapeDtypeStruct((6, (m // TM) * M_SAMPLE * N_KV, HEAD_DIM), F32),
            jax.ShapeDtypeStruct((m, HEAD_DIM), F32),
        ],
        scratch_shapes=[pltpu.VMEM((TM, D_MODEL), BF16)],
        compiler_params=_cparams(("parallel", "arbitrary")),
        name="nsa_in_proj",
    )(x, g, w, wg, cos, sin)


def _dual_kernel(*refs, mode, has_bias):
    if has_bias:
        x_ref, g_ref, wa_ref, wb_ref, ba_ref, bb_ref, o_ref, xn_ref = refs
    else:
        x_ref, g_ref, wa_ref, wb_ref, o_ref, xn_ref = refs

    @pl.when(pl.program_id(1) == 0)
    def _():
        xn_ref[...] = _rms_scale(x_ref[...], g_ref[...]).astype(BF16)

    xn = xn_ref[...]
    a = _dot(xn, wa_ref[...])
    b = _dot(xn, wb_ref[...])
    if has_bias:
        a = a + ba_ref[...]
        b = b + bb_ref[...]
    if mode == "swiglu":
        o = jax.nn.silu(a) * b
    else:
        o = a * jax.nn.sigmoid(b)
    o_ref[...] = o.astype(o_ref.dtype)


def _dual_proj(x, g, wa, wb, layer, n_out, b_col0, bias, mode, out_dtype, name):
    m = x.shape[0]
    nb = b_col0 // TN
    in_specs = [
        pl.BlockSpec((TM, D_MODEL), lambda i, j: (i, 0)),
        pl.BlockSpec((1, D_MODEL), lambda i, j: (0, 0)),
        pl.BlockSpec((None, D_MODEL, TN), lambda i, j: (layer, 0, j)),
        pl.BlockSpec((None, D_MODEL, TN), lambda i, j: (layer, 0, j + nb)),
    ]
    args = [x, g, wa, wb]
    if bias is not None:
        in_specs += [pl.BlockSpec((1, TN), lambda i, j: (0, j)), pl.BlockSpec((1, TN), lambda i, j: (0, j + nb))]
        args += [bias, bias]
    return pl.pallas_call(
        functools.partial(_dual_kernel, mode=mode, has_bias=bias is not None),
        grid=(m // TM, n_out // TN),
        in_specs=in_specs,
        out_specs=pl.BlockSpec((TM, TN), lambda i, j: (i, j)),
        out_shape=jax.ShapeDtypeStruct((m, n_out), out_dtype),
        scratch_shapes=[pltpu.VMEM((TM, D_MODEL), BF16)],
        compiler_params=_cparams(("parallel", "arbitrary")),
        name=name,
    )(*args)


def _proj_norm_res_kernel(a_ref, w_ref, g_ref, h_ref, o_ref, acc_ref, *, nk):
    k = pl.program_id(1)
    part = _dot(a_ref[...], w_ref[...])

    @pl.when(k == 0)
    def _():
        acc_ref[...] = part

    @pl.when(k > 0)
    def _():
        acc_ref[...] += part

    @pl.when(k == nk - 1)
    def _():
        o_ref[...] = h_ref[...] + _rms_scale(acc_ref[...], g_ref[...])


def _proj_norm_res(a, w, layer, g, h, nk, name):
    m, kdim = a.shape
    tk = kdim // nk
    return pl.pallas_call(
        functools.partial(_proj_norm_res_kernel, nk=nk),
        grid=(m // TM, nk),
        in_specs=[
            pl.BlockSpec((TM, tk), lambda i, k: (i, k)),
            pl.BlockSpec((None, tk, D_MODEL), lambda i, k: (layer, k, 0)),
            pl.BlockSpec((1, D_MODEL), lambda i, k: (0, 0)),
            pl.BlockSpec((TM, D_MODEL), lambda i, k: (i, 0)),
        ],
        out_specs=pl.BlockSpec((TM, D_MODEL), lambda i, k: (i, 0)),
        out_shape=jax.ShapeDtypeStruct((m, D_MODEL), F32),
        scratch_shapes=[pltpu.VMEM((TM, D_MODEL), F32)],
        compiler_params=_cparams(("parallel", "arbitrary")),
        name=name,
    )(a, w, g, h)


def _ple_kernel(h_ref, hc_ref, p_ref, wg_ref, wp_ref, o_ref, *rest, split):
    hb_ref = rest[-1]

    @pl.when(pl.program_id(1) == 0)
    def _():
        hb_ref[...] = h_ref[...].astype(BF16)

    gate = jax.nn.sigmoid(_dot(hb_ref[...], wg_ref[...]))
    out = hc_ref[...] + gate * _dot(p_ref[...], wp_ref[...])
    o_ref[...] = out
    if split:
        rest[0][...] = out[TM - M_SAMPLE:]


def _ple(h, p, wg, wp, layer, name, split=False):
    m = h.shape[0]
    out_specs = pl.BlockSpec((TM, TN), lambda i, j: (i, j))
    out_shape = jax.ShapeDtypeStruct((m, D_MODEL), F32)
    if split:
        assert (m // TM - 1) * TM + (TM - M_SAMPLE) == M_PROMPT
        out_specs = [out_specs, pl.BlockSpec((M_SAMPLE, TN), lambda i, j: (i, j))]
        out_shape = [jax.ShapeDtypeStruct((M_PROMPT, D_MODEL), F32),
                     jax.ShapeDtypeStruct(((m // TM) * M_SAMPLE, D_MODEL), F32)]
    return pl.pallas_call(
        functools.partial(_ple_kernel, split=split),
        grid=(m // TM, D_MODEL // TN),
        in_specs=[
            pl.BlockSpec((TM, D_MODEL), lambda i, j: (i, 0)),
            pl.BlockSpec((TM, TN), lambda i, j: (i, j)),
            pl.BlockSpec((None, TM, PLE_DIM), lambda i, j: (layer, i, 0)),
            pl.BlockSpec((None, D_MODEL, TN), lambda i, j: (layer, 0, j)),
            pl.BlockSpec((None, PLE_DIM, TN), lambda i, j: (layer, 0, j)),
        ],
        out_specs=out_specs,
        out_shape=out_shape,
        scratch_shapes=[pltpu.VMEM((TM, D_MODEL), BF16)],
        compiler_params=_cparams(("parallel", "arbitrary")),
        name=name,
    )(h, h, p, wg, wp)


TILES_PER_CHUNK = CMP_STRIDE * N_KV // 8


def _compress_kernel(pt_ref, *refs):
    n = PAGES_PER_STEP
    k_refs, v_refs = refs[0:n], refs[n:2 * n]
    wk_ref, wv_ref, pk0_ref, pk1_ref, pv0_ref, pv1_ref = refs[2 * n:2 * n + 6]
    for srcs, w_ref, o0_ref, o1_ref in ((k_refs, wk_ref, pk0_ref, pk1_ref), (v_refs, wv_ref, pv0_ref, pv1_ref)):
        acc = None
        for jj in range(TILES_PER_CHUNK // 2):
            halves = []
            for j in (2 * jj, 2 * jj + 1):
                rows = [srcs[p][(c * TILES_PER_CHUNK + j) * 8:(c * TILES_PER_CHUNK + j + 1) * 8, :]
                        for p in range(n) for c in range(CHUNKS_PER_PAGE)]
                halves.append(jnp.concatenate(rows, axis=0))
            term = _dot(jnp.concatenate(halves, axis=1).astype(BF16), w_ref[jj])
            acc = term if acc is None else acc + term
        width = 2 * HEAD_DIM
        folded = acc[:, :width] + pltpu.roll(acc[:, width:], acc.shape[0] - N_KV, axis=0)
        o0_ref[0] = folded[:, :HEAD_DIM]
        o1_ref[0] = folded[:, HEAD_DIM:]


def _compress_stage1(page_table, k_src, k_type, v_src, v_type, wk, wv, n_seq, pages_per_seq, name):
    n = PAGES_PER_STEP
    groups = pages_per_seq // n
    n_chunks = pages_per_seq * CHUNKS_PER_PAGE

    def src_spec(slot, typ):
        return pl.BlockSpec(
            (None, PAGE_ROWS, HEAD_DIM), lambda b, t, pt: (typ, pt[b * pages_per_seq + t * n + slot], 0))

    w_spec = pl.BlockSpec(wk.shape, lambda b, t, pt: (0, 0, 0))
    out_spec = pl.BlockSpec((1, n * CHUNKS_PER_PAGE * 8, HEAD_DIM), lambda b, t, pt: (b, t, 0))
    out_sds = jax.ShapeDtypeStruct((n_seq, n_chunks * 8, HEAD_DIM), F32)
    return pl.pallas_call(
        _compress_kernel,
        grid_spec=pltpu.PrefetchScalarGridSpec(
            num_scalar_prefetch=1,
            grid=(n_seq, groups),
            in_specs=[src_spec(s, k_type) for s in range(n)] + [src_spec(s, v_type) for s in range(n)]
            + [w_spec, w_spec],
            out_specs=[out_spec] * 4,
        ),
        out_shape=[out_sds] * 4,
        compiler_params=_cparams(("parallel", "arbitrary")),
        name=name,
    )(page_table, *([k_src] * n), *([v_src] * n), wk, wv)


def _compress_finish_kernel(p0_ref, p1_ref, pn_ref, pe_ref, w1_ref, w2_ref, o_ref):
    c = o_ref.shape[2]
    bias = _dot(pe_ref[...].astype(BF16), w1_ref[...])[0:1]
    row = lax.broadcasted_iota(jnp.int32, (c, 1), 0)
    for g in range(N_KV):
        p0 = p0_ref[0, pl.ds(g, c, stride=8), :]
        p1 = p1_ref[0, pl.ds(g, c, stride=8), :]
        nxt = pltpu.roll(p1, c - 1, axis=0)
        nxt = jnp.where(row == c - 1, pn_ref[g:g + 1, :], nxt)
        hid = p0 + nxt + bias
        o_ref[0, g] = _dot(jax.nn.silu(hid).astype(BF16), w2_ref[...]).astype(o_ref.dtype)


def _compress_finish(p0, p1, p1_next, pe8, w1, w2, name):
    n_seq, rows, _ = p0.shape
    c = rows // 8
    seq_spec = pl.BlockSpec((1, rows, HEAD_DIM), lambda b: (b, 0, 0))
    return pl.pallas_call(
        _compress_finish_kernel,
        grid=(n_seq,),
        in_specs=[
            seq_spec, seq_spec,
            pl.BlockSpec((8, HEAD_DIM), lambda b: (b, 0)),
            pl.BlockSpec((8, CMP_BLOCK * HEAD_DIM), lambda b: (0, 0)),
            pl.BlockSpec((CMP_BLOCK * HEAD_DIM, HEAD_DIM), lambda b: (0, 0)),
            pl.BlockSpec((HEAD_DIM, HEAD_DIM), lambda b: (0, 0)),
        ],
        out_specs=pl.BlockSpec((1, N_KV, c, HEAD_DIM), lambda b: (b, 0, 0, 0)),
        out_shape=jax.ShapeDtypeStruct((n_seq, N_KV, c, HEAD_DIM), BF16),
        compiler_params=_cparams(("parallel",)),
        name=name,
    )(p0, p1, p1_next, pe8, w1, w2)


def _gather_kernel(pt_ref, *refs):
    n = PAGES_PER_STEP
    k_refs, v_refs = refs[0:n], refs[n:2 * n]
    kn_ref, vn_ref, ko_ref, vo_ref = refs[2 * n:2 * n + 4]
    t = pl.program_id(1)
    for srcs, new_ref, out_ref in ((k_refs, kn_ref, ko_ref), (v_refs, vn_ref, vo_ref)):
        for p in range(n):
            page = t * n + p
            for g in range(N_KV):
                old = srcs[p][pl.ds(g, PAGE_SIZE, stride=N_KV), :]
                new = new_ref[pl.ds(g, PAGE_SIZE, stride=N_KV), :]
                val = jnp.where(page < N_PAGES, old, jnp.where(page == N_PAGES, new, 0.0))
                out_ref[0, g, p * PAGE_SIZE:(p + 1) * PAGE_SIZE, :] = val.astype(BF16)


def _gather_pages(page_table, k_pool, v_pool, k_new, v_new, n_out_pages):
    n = PAGES_PER_STEP
    groups = n_out_pages // n
    blk = (PAGE_ROWS, HEAD_DIM)

    def src_spec(slot):
        return pl.BlockSpec(
            blk, lambda b, t, pt: (pt[b * N_PAGES + jnp.minimum(t * n + slot, N_PAGES - 1)], 0))

    new_spec = pl.BlockSpec(blk, lambda b, t, pt: (b, 0))
    out_spec = pl.BlockSpec((1, N_KV, n * PAGE_SIZE, HEAD_DIM), lambda b, t, pt: (b, 0, t, 0))
    out_sds = jax.ShapeDtypeStruct((DEC_BATCH, N_KV, n_out_pages * PAGE_SIZE, HEAD_DIM), BF16)
    return pl.pallas_call(
        _gather_kernel,
        grid_spec=pltpu.PrefetchScalarGridSpec(
            num_scalar_prefetch=1,
            grid=(DEC_BATCH, groups),
            in_specs=[src_spec(s) for s in range(n)] * 2 + [new_spec, new_spec],
            out_specs=[out_spec, out_spec],
        ),
        out_shape=[out_sds, out_sds],
        compiler_params=_cparams(("parallel", "arbitrary")),
        name="nsa_gather_sel_pages",
    )(page_table, *([k_pool] * n), *([v_pool] * n), k_new, v_new)


def _masked_softmax2(s, mask):
    s = jnp.where(mask, s, NEG)
    m = jnp.max(s, axis=-1, keepdims=True)
    e = jnp.where(mask, jnp.exp2(s - m), 0.0)
    return e / jnp.maximum(jnp.sum(e, axis=-1, keepdims=True), 1e-30)


def _split3(x):
    parts = []
    rem = x
    for _ in range(3):
        part = rem.astype(BF16)
        parts.append(part)
        rem = rem - part.astype(F32)
    return parts


def _sel_tile(kt, carry, qa, t_row, ks, vs, e_ref, e_row0, tk, masked, lanes=slice(0, HEAD_DIM)):
    m, l, acc = carry
    r0 = pl.multiple_of(kt * tk, tk)
    e0 = pl.multiple_of(kt * tk - e_row0, tk)
    ka = jnp.concatenate([ks[pl.ds(r0, tk), lanes], e_ref[pl.ds(e0, tk), :]], axis=1)
    sc = _dot_nt(qa, ka)
    if masked:
        kpos = r0 + lax.broadcasted_iota(jnp.int32, (1, tk), 1)
        sc = jnp.where(kpos <= t_row, sc, NEG)
    m_new = jnp.maximum(m, jnp.max(sc, axis=-1, keepdims=True))
    alpha = jnp.exp2(m - m_new)
    p = jnp.exp2(sc - m_new)
    l = alpha * l + jnp.sum(p, axis=-1, keepdims=True)
    acc = alpha * acc + _dot(p.astype(BF16), vs[pl.ds(r0, tk), lanes])
    return m_new, l, acc


def _sel_init(rows):
    return (jnp.full((rows, 1), -jnp.inf, F32), jnp.zeros((rows, 1), F32), jnp.zeros((rows, HEAD_DIM), F32))


def _gated_merge(gt, o_c, o_s, o_w, tq):
    outs = []
    for r in range(GQA_R):
        sl = slice(r * tq, (r + 1) * tq)
        outs.append(gt[:, r:r + 1] * o_c[sl] + gt[:, GQA_R + r:GQA_R + r + 1] * o_s[sl]
                    + gt[:, 2 * GQA_R + r:2 * GQA_R + r + 1] * o_w[sl])
    return jnp.concatenate(outs, axis=1)


def _cmp_to_sel_matrix(nb, nsp):
    n = np.arange(nb)[:, None]
    j = np.arange(nsp)[None, :]
    return ((n >= 4 * j) & (n <= 4 * j + 3)).astype(np.float32) + ((n >= 4 * j - 1) & (n <= 4 * j + 2)).astype(np.float32)


def _block_onehot(n_keys):
    k = np.arange(n_keys)[:, None] // SEL_BLOCK
    return jnp.asarray((k % 128 == np.arange(128)[None, :]).astype(np.float32), BF16)


def _window_bias_prompt():
    q = np.arange(TQ_PROMPT)[:, None]
    kb = np.arange(WBAND_PROMPT)[None, :]
    pats = [kb <= TQ_PROMPT * i + q for i in range(WINDOW // TQ_PROMPT)]
    pats.append((kb > q) & (kb <= q + WINDOW))
    return jnp.asarray(np.where(np.stack(pats), 0.0, NEG), F32)


def _attn_prompt_kernel(q_ref, gate_ref, kc_ref, vc_ref, ks_ref, vs_ref, kw_ref, vw_ref, mmt_ref, e_ref, wb_ref,
                        o_init_ref, o_ref, *, nb, ns):
    del o_init_ref
    tq = TQ_PROMPT
    rows = GQA_R * tq
    qi = pl.program_id(2)
    t0 = qi * tq
    t_row = t0 + (lax.broadcasted_iota(jnp.int32, (rows, 1), 0) & (tq - 1))
    cmp_end = lax.broadcasted_iota(jnp.int32, (1, nb), 1) * CMP_STRIDE + (CMP_BLOCK - 1)
    jb = lax.broadcasted_iota(jnp.int32, (ns, tq), 0)
    cur = lax.shift_right_logical(t0 + lax.broadcasted_iota(jnp.int32, (1, tq), 1), 6)
    forced = (jb == 0) | (jb == cur) | (jb == cur - 1)
    eligible = jb <= cur
    groups = ns // 8
    jgrp = [jb[8 * v:8 * v + 8] for v in range(groups)]
    eye = jnp.where(lax.broadcasted_iota(jnp.int32, (tq, tq), 0) == lax.broadcasted_iota(jnp.int32, (tq, tq), 1),
                    1.0, 0.0).astype(BF16)
    mmt = mmt_ref[...]

    heads = range(ATT_HEADS_PER_STEP)
    qs, o_cs, qas = [], [], []
    for h in heads:
        qq = q_ref[:, h * GQA_R * HEAD_DIM:(h + 1) * GQA_R * HEAD_DIM]
        q = jnp.concatenate([qq[:, r * HEAD_DIM:(r + 1) * HEAD_DIM] for r in range(GQA_R)], axis=0)

        s = _dot_nt(q, kc_ref[0, h])
        p_c = _masked_softmax2(s, cmp_end <= t_row)
        o_cs.append(_dot(p_c.astype(BF16), vc_ref[0, h]))
        imp = p_c[0:tq]
        for r in range(1, GQA_R):
            imp = imp + p_c[r * tq:(r + 1) * tq]

        blk_imp = None
        for part in _split3(imp):
            term = _dot_nt(mmt, part)
            blk_imp = term if blk_imp is None else blk_imp + term
        score = jnp.where(forced, jnp.inf, jnp.where(eligible, blk_imp, -jnp.inf))
        sgrp = [score[8 * v:8 * v + 8] for v in range(groups)]
        rank = [jnp.zeros((8, tq), F32) for _ in range(groups)]
        for i in range(ns):
            ci = score[i:i + 1]
            for v in range(groups):
                if 8 * v > i:
                    beats = ci >= sgrp[v]
                elif 8 * v + 7 <= i:
                    beats = ci > sgrp[v]
                else:
                    beats = (ci > sgrp[v]) | ((ci == sgrp[v]) & (jgrp[v] > i))
                rank[v] = rank[v] + jnp.where(beats, 1.0, 0.0)
        rank = jnp.concatenate(rank, axis=0)
        sel_t = jnp.where((rank < float(TOPK)) & eligible, 1.0, 0.0)
        sel_t = jnp.concatenate([sel_t, jnp.zeros((128 - ns, tq), F32)], axis=0).astype(BF16)
        sel = _dot_nt(eye, sel_t)
        negsel = jnp.where(sel > 0.5, 0.0, NEG).astype(BF16)
        qs.append(q)
        qas.append(jnp.concatenate([q, jnp.concatenate([negsel] * GQA_R, axis=0)], axis=1))

    n_tiles = lax.shift_right_logical(t0 + tq - 1, TK_ATT.bit_length() - 1) + 1

    def tiles(kt, carries, masked):
        return tuple(
            _sel_tile(kt, carries[h], qa=qas[h], t_row=t_row, ks=ks_ref, vs=vs_ref, e_ref=e_ref, e_row0=0,
                      tk=TK_ATT, masked=masked, lanes=slice(h * HEAD_DIM, (h + 1) * HEAD_DIM))
            for h in heads)

    carries = lax.fori_loop(0, n_tiles - 1, functools.partial(tiles, masked=False),
                            tuple(_sel_init(rows) for _ in heads))
    carries = tiles(n_tiles - 1, carries, masked=True)

    wstart = pl.multiple_of(jnp.maximum(t0 - WINDOW, 0), tq)
    bias = wb_ref[jnp.minimum(qi, WINDOW // tq)]
    bias = jnp.concatenate([bias] * GQA_R, axis=0)
    for h in heads:
        lanes = slice(h * HEAD_DIM, (h + 1) * HEAD_DIM)
        _, l, acc = carries[h]
        o_s = acc / jnp.maximum(l, 1e-30)
        s = _dot_nt(qs[h], kw_ref[pl.ds(wstart, WBAND_PROMPT), lanes]) + bias
        e = jnp.exp2(s - jnp.max(s, axis=-1, keepdims=True))
        p_w = e / jnp.sum(e, axis=-1, keepdims=True)
        o_w = _dot(p_w.astype(BF16), vw_ref[pl.ds(wstart, WBAND_PROMPT), lanes])
        o_ref[:, h * GQA_R * HEAD_DIM:(h + 1) * GQA_R * HEAD_DIM] = _gated_merge(
            gate_ref[h], o_cs[h], o_s, o_w, tq).astype(o_ref.dtype)


def _attn_prompt(qkv, gates_t, kc, vc, o_init):
    tq = TQ_PROMPT
    hp = ATT_HEADS_PER_STEP
    nq = SEQ // tq
    nb = SEQ // CMP_STRIDE
    ns = SEQ // SEL_BLOCK
    k_sel0, v_sel0 = Q_DIM + 2 * KV_DIM, Q_DIM + 3 * KV_DIM
    k_win0, v_win0 = Q_DIM + 4 * KV_DIM, Q_DIM + 5 * KV_DIM
    seq_spec = lambda base: pl.BlockSpec((SEQ, hp * HEAD_DIM), lambda b, g, i: (b, base // (hp * HEAD_DIM) + g))
    blk_spec = pl.BlockSpec((1, hp, nb, HEAD_DIM), lambda b, g, i: (b, g, 0, 0))
    mmt = jnp.asarray(_cmp_to_sel_matrix(nb, ns).T, BF16)
    wbias = _window_bias_prompt()
    return pl.pallas_call(
        functools.partial(_attn_prompt_kernel, nb=nb, ns=ns),
        grid=(BATCH, N_KV // hp, nq),
        in_specs=[
            pl.BlockSpec((tq, hp * GQA_R * HEAD_DIM), lambda b, g, i: (b * nq + i, g)),
            pl.BlockSpec((hp, tq, HEAD_DIM), lambda b, g, i: (g, b * nq + i, 0)),
            blk_spec, blk_spec,
            seq_spec(k_sel0), seq_spec(v_sel0), seq_spec(k_win0), seq_spec(v_win0),
            pl.BlockSpec((ns, nb), lambda b, g, i: (0, 0)),
            pl.BlockSpec((SEQ, 128), lambda b, g, i: (0, 0)),
            pl.BlockSpec(wbias.shape, lambda b, g, i: (0, 0, 0)),
            pl.BlockSpec(memory_space=pl.ANY),
        ],
        out_specs=pl.BlockSpec((tq, hp * GQA_R * HEAD_DIM), lambda b, g, i: (b * nq + i, g)),
        out_shape=jax.ShapeDtypeStruct((M_ALL, Q_DIM), BF16),
        input_output_aliases={11: 0},
        compiler_params=_cparams(("parallel", "parallel", "arbitrary")),
        name="nsa_attn_prompt",
    )(qkv, gates_t, kc, vc, qkv, qkv, qkv, qkv, mmt, _block_onehot(SEQ), wbias, o_init)


TK_SAMPLE = 2048
SAMPLE_KEY_PAGES = 144
SAMPLE_KEYS = SAMPLE_KEY_PAGES * PAGE_SIZE
SAMPLE_NB = PAST_LEN // CMP_STRIDE
SAMPLE_NS = -(-(PAST_LEN + DEC_SEQ) // SEL_BLOCK)
SAMPLE_NSP = 384


def _attn_sample_kernel(q_ref, gate_ref, kc_ref, vc_ref, ks_ref, vs_ref, kw_ref, vw_ref, mmat_ref, e_ref, o_ref):
    tq, nb, ns, nsp = DEC_SEQ, SAMPLE_NB, SAMPLE_NS, SAMPLE_NSP
    rows = GQA_R * tq
    t0 = PAST_LEN
    q = q_ref[0, 0]
    t_row = t0 + (lax.broadcasted_iota(jnp.int32, (rows, 1), 0) & (tq - 1))

    s = _dot_nt(q, kc_ref[0, 0])
    cmp_end = lax.broadcasted_iota(jnp.int32, (1, nb), 1) * CMP_STRIDE + (CMP_BLOCK - 1)
    p_c = _masked_softmax2(s, cmp_end <= t_row)
    o_c = _dot(p_c.astype(BF16), vc_ref[0, 0])
    imp = p_c[0:tq]
    for r in range(1, GQA_R):
        imp = imp + p_c[r * tq:(r + 1) * tq]

    mmat = mmat_ref[...]
    blk_imp = None
    for part in _split3(imp):
        term = _dot(part, mmat)
        blk_imp = term if blk_imp is None else blk_imp + term
    jb = lax.broadcasted_iota(jnp.int32, (tq, nsp), 1)
    cur = lax.shift_right_logical(t0 + lax.broadcasted_iota(jnp.int32, (tq, 1), 0), 6)
    forced = (jb == 0) | (jb == cur) | (jb == cur - 1)
    eligible = jb <= cur
    score = jnp.where(forced, jnp.inf, jnp.where(eligible, blk_imp, -jnp.inf))
    rank = jnp.zeros((tq, nsp), F32)
    for i in range(ns):
        ci = score[:, i:i + 1]
        beats = (ci > score) | ((ci == score) & (jb > i))
        rank = rank + jnp.where(beats, 1.0, 0.0)
    negsel = jnp.where((rank < float(TOPK)) & eligible, 0.0, NEG)
    negsel = jnp.concatenate([negsel] * GQA_R, axis=0).astype(BF16)

    ks, vs = ks_ref.at[0, 0], vs_ref.at[0, 0]
    tk = TK_SAMPLE
    n_tiles = (t0 + tq - 1) // tk + 1
    tiles_per_group = 128 * SEL_BLOCK // tk
    carry = _sel_init(rows)
    for c in range(nsp // 128):
        qa = jnp.concatenate([q, negsel[:, c * 128:(c + 1) * 128]], axis=1)
        lo = c * tiles_per_group
        hi = min(lo + tiles_per_group, n_tiles - 1)
        tile = functools.partial(_sel_tile, qa=qa, t_row=t_row, ks=ks, vs=vs, e_ref=e_ref, e_row0=lo * tk, tk=tk)
        carry = lax.fori_loop(lo, hi, functools.partial(tile, masked=False), carry)
        if lo <= n_tiles - 1 < lo + tiles_per_group:
            carry = tile(n_tiles - 1, carry, masked=True)
    _, l, acc = carry
    o_s = acc / jnp.maximum(l, 1e-30)

    s = _dot_nt(q, kw_ref[0, 0])
    kpos = (PAST_LEN - WINDOW) + lax.broadcasted_iota(jnp.int32, (1, WBAND_SAMPLE), 1)
    p_w = _masked_softmax2(s, (kpos <= t_row) & (kpos > t_row - WINDOW))
    o_w = _dot(p_w.astype(BF16), vw_ref[0, 0])

    o_ref[...] = _gated_merge(gate_ref[0], o_c, o_s, o_w, tq).astype(o_ref.dtype)


def _attn_sample(q_s, gates_s, kc, vc, ks, vs, kw, vw):
    head_spec = lambda n: pl.BlockSpec((1, 1, n, HEAD_DIM), lambda b, g: (b, g, 0, 0))
    mmat = jnp.asarray(_cmp_to_sel_matrix(SAMPLE_NB, SAMPLE_NSP), BF16)
    return pl.pallas_call(
        _attn_sample_kernel,
        grid=(DEC_BATCH, N_KV),
        in_specs=[
            head_spec(GQA_R * DEC_SEQ),
            pl.BlockSpec((1, DEC_SEQ, HEAD_DIM), lambda b, g: (g, b, 0)),
            head_spec(SAMPLE_NB), head_spec(SAMPLE_NB),
            head_spec(SAMPLE_KEYS), head_spec(SAMPLE_KEYS),
            head_spec(WBAND_SAMPLE), head_spec(WBAND_SAMPLE),
            pl.BlockSpec((SAMPLE_NB, SAMPLE_NSP), lambda b, g: (0, 0)),
            pl.BlockSpec((128 * SEL_BLOCK, 128), lambda b, g: (0, 0)),
        ],
        out_specs=pl.BlockSpec((DEC_SEQ, GQA_R * HEAD_DIM), lambda b, g: (b, g)),
        out_shape=jax.ShapeDtypeStruct((M_SAMPLE, Q_DIM), F32),
        compiler_params=_cparams(("parallel", "parallel")),
        name="nsa_attn_sample",
    )(q_s, gates_s, kc, vc, ks, vs, kw, vw, mmat, _block_onehot(128 * SEL_BLOCK))


HALO = 32


def _conv_kernel(u_ref, halo_ref, wdw_ref, bdw_ref, lng_ref, lnb_ref, w2_ref, b2_ref, g_ref, h_ref, *rest,
                 t, tiles_per_seq, halo_is_state):
    o_ref, buf_ref, y_ref = rest[-3:]
    i = pl.program_id(0)
    if halo_is_state:
        buf_ref[0:HALO, :] = halo_ref[...]
    else:
        first = (i % tiles_per_seq) == 0
        buf_ref[0:HALO, :] = jnp.where(first, 0.0, halo_ref[...])
    buf_ref[HALO:HALO + t, :] = u_ref[...]
    off = HALO - (CONV_W - 1)
    for c in range(D_MODEL // HEAD_DIM):
        cs = slice(c * HEAD_DIM, (c + 1) * HEAD_DIM)
        base = buf_ref[:, cs]
        w = wdw_ref[:, cs]
        y = bdw_ref[:, cs]
        n_rows = HALO + t
        for s in range(8):
            taps = [k for k in range(CONV_W) if (off + k) % 8 == s]
            shifted = base if s == 0 else pltpu.roll(base, n_rows - s, axis=0)
            for k in taps:
                a = off + k - s
                y = y + shifted[a:a + t] * w[k:k + 1]
        y_ref[:, cs] = y
    y = y_ref[...]
    mu = jnp.mean(y, axis=-1, keepdims=True)
    yc = y - mu
    var = jnp.mean(yc * yc, axis=-1, keepdims=True)
    z = yc * lax.rsqrt(var + LN_EPS) * lng_ref[...] + lnb_ref[...]
    mval = _dot(jax.nn.silu(z).astype(BF16), w2_ref[...]) + b2_ref[...]
    o_ref[...] = h_ref[...] + _rms_scale(mval, g_ref[...])


def _conv_tail(u, halo_src, wdw, bdw, lng, lnb, w2, b2, g, h, *, t, n_tiles, row0, tiles_per_seq, halo_is_state,
               out_init=None):
    blk0 = row0 // t
    extra_specs, extra_args, aliases = [], [], {}
    out_rows = n_tiles * t
    if out_init is not None:
        extra_specs, extra_args, aliases = [pl.BlockSpec(memory_space=pl.ANY)], [out_init], {10: 0}
        out_rows = out_init.shape[0]
    if halo_is_state:
        halo_spec = pl.BlockSpec((HALO, D_MODEL), lambda i: (i, 0))
    else:
        per = t // HALO
        halo_spec = pl.BlockSpec((HALO, D_MODEL), lambda i: (jnp.maximum((blk0 + i) * per - 1, 0), 0))
    vec = pl.BlockSpec((1, D_MODEL), lambda i: (0, 0))
    kern = functools.partial(_conv_kernel, t=t, tiles_per_seq=tiles_per_seq, halo_is_state=halo_is_state)
    return pl.pallas_call(
        kern,
        grid=(n_tiles,),
        in_specs=[
            pl.BlockSpec((t, D_MODEL), lambda i: (blk0 + i, 0)),
            halo_spec,
            pl.BlockSpec((HALO, D_MODEL), lambda i: (0, 0)),
            vec, vec, vec,
            pl.BlockSpec((None, D_MODEL, D_MODEL), lambda i: (0, 0, 0)),
            vec, vec,
            pl.BlockSpec((t, D_MODEL), lambda i: (blk0 + i, 0)),
        ] + extra_specs,
        out_specs=pl.BlockSpec((t, D_MODEL), lambda i: (i, 0)),
        out_shape=jax.ShapeDtypeStruct((out_rows, D_MODEL), F32),
        input_output_aliases=aliases,
        scratch_shapes=[pltpu.VMEM((HALO + t, D_MODEL), F32), pltpu.VMEM((t, D_MODEL), F32)],
        compiler_params=_cparams(("parallel",)),
        name="conv_tail_state" if halo_is_state else "conv_tail",
    )(u, halo_src, wdw, bdw, lng, lnb, w2, b2, g, h, *extra_args)


def _row(v):
    return v.reshape(1, -1).astype(F32)


def _rope_tables():
    half = HEAD_DIM // 2
    pos = jnp.concatenate([jnp.tile(jnp.arange(SEQ), BATCH), jnp.tile(PAST_LEN + jnp.arange(DEC_SEQ), DEC_BATCH)])
    inv = ROPE_THETA ** (-jnp.arange(half, dtype=F32) / half)
    ang = pos.astype(F32)[:, None] * inv[None, :]
    cos, sin = jnp.cos(ang), jnp.sin(ang)
    return jnp.concatenate([cos, cos], axis=1), jnp.concatenate([-sin, sin], axis=1)


def _cmp_weights(w1):
    w1r = w1.reshape(CMP_BLOCK // CMP_STRIDE, CMP_STRIDE * HEAD_DIM, HEAD_DIM)
    per_pos = jnp.concatenate([w1r[0], w1r[1]], axis=1).reshape(CMP_STRIDE, HEAD_DIM, 2 * HEAD_DIM)
    per_tile = jnp.concatenate([per_pos[0::2], per_pos[1::2]], axis=2)
    return per_tile.reshape(TILES_PER_CHUNK // 2, 2 * HEAD_DIM, 4 * HEAD_DIM).astype(BF16)


def kernel(x_prompt, x_sample, p_prompt, p_sample, page_table, cache_k_cmp, cache_v_cmp, cache_k_sel, cache_v_sel, cache_k_win, cache_v_win, state_conv, norm_mix_pre, norm_mix_post, norm_ffn_pre, norm_ffn_post, ffn_w_gate, ffn_w_up, ffn_w_down, ple_w_proj, ple_w_gate, nsa_w_in, nsa_w_out, nsa_cmp_pe_k, nsa_cmp_w1_k, nsa_cmp_w2_k, nsa_cmp_pe_v, nsa_cmp_w1_v, nsa_cmp_w2_v, conv_w_pw1, conv_b_pw1, conv_w_dw, conv_b_dw, conv_ln_g, conv_ln_b, conv_w_pw2, conv_b_pw2):
    h = jnp.concatenate([x_prompt.reshape(M_PROMPT, D_MODEL), x_sample.reshape(M_SAMPLE, D_MODEL)], axis=0)
    p_all = jnp.concatenate([p_prompt.reshape(DEPTH, M_PROMPT, PLE_DIM),
                             p_sample.reshape(DEPTH, M_SAMPLE, PLE_DIM)], axis=1).astype(BF16)
    pt_flat = page_table.reshape(-1)
    w_gate, w_up, w_down = ffn_w_gate.astype(BF16), ffn_w_up.astype(BF16), ffn_w_down.astype(BF16)
    w_ple_gate, w_ple = ple_w_gate.astype(BF16), ple_w_proj.astype(BF16)

    def residual_tail(h, layer, split=False):
        act = _dual_proj(h, _row(norm_ffn_pre[layer]), w_gate, w_up, layer, D_FF, 0, None, "swiglu", BF16,
                         f"ffn_up_{layer}")
        h = _proj_norm_res(act, w_down, layer, _row(norm_ffn_post[layer]), h, 4, f"ffn_down_{layer}")
        return _ple(h, p_all, w_ple_gate, w_ple, layer, f"ple_{layer}", split=split)

    n_main = Q_DIM + 6 * KV_DIM
    w_gates = jnp.pad(nsa_w_in[0][:, n_main:], ((0, 0), (0, HEAD_DIM - 3 * N_HEADS))).astype(BF16)
    cos, sin = _rope_tables()
    qkv, kvp, kvs, gates = _in_proj(h, _row(norm_mix_pre[0]), nsa_w_in.astype(BF16), w_gates, cos, sin)

    gates_t = gates[:, :3 * N_HEADS].reshape(M_ALL, 3, N_KV, GQA_R).transpose(2, 0, 1, 3).reshape(N_KV, M_ALL, 3 * GQA_R)
    gates_t = jnp.pad(gates_t, ((0, 0), (0, 0), (0, HEAD_DIM - 3 * GQA_R)))

    kv_s = kvs[:, -M_SAMPLE * N_KV:].reshape(6, DEC_BATCH, DEC_SEQ, N_KV, HEAD_DIM)
    new_page = lambda c: jnp.pad(kv_s[c], ((0, 0), (0, PAGE_SIZE - DEC_SEQ), (0, 0), (0, 0))).reshape(-1, HEAD_DIM)

    wk, wv = _cmp_weights(nsa_cmp_w1_k[0]), _cmp_weights(nsa_cmp_w1_v[0])
    pe_k = jnp.broadcast_to(nsa_cmp_pe_k[0].reshape(1, -1), (8, CMP_BLOCK * HEAD_DIM))
    pe_v = jnp.broadcast_to(nsa_cmp_pe_v[0].reshape(1, -1), (8, CMP_BLOCK * HEAD_DIM))
    w1k, w1v = nsa_cmp_w1_k[0].astype(BF16), nsa_cmp_w1_v[0].astype(BF16)
    w2k, w2v = nsa_cmp_w2_k[0].astype(BF16), nsa_cmp_w2_v[0].astype(BF16)

    pages_prompt = SEQ // PAGE_SIZE
    ident = jnp.arange(BATCH * pages_prompt, dtype=jnp.int32)
    pk0, pk1, pv0, pv1 = _compress_stage1(ident, kvp, 0, kvp, 1, wk, wv, BATCH, pages_prompt, "nsa_cmp_prompt")
    zero_next = jnp.zeros((BATCH * 8, HEAD_DIM), F32)
    kc_p = _compress_finish(pk0, pk1, zero_next, pe_k, w1k, w2k, "nsa_cmp_fin_k_prompt")
    vc_p = _compress_finish(pv0, pv1, zero_next, pe_v, w1v, w2v, "nsa_cmp_fin_v_prompt")

    pool3 = lambda c: c.reshape(1, -1, HEAD_DIM)
    pk0, pk1, pv0, pv1 = _compress_stage1(pt_flat, pool3(cache_k_cmp), 0, pool3(cache_v_cmp), 0, wk, wv, DEC_BATCH,
                                          N_PAGES, "nsa_cmp_sample")
    _, pk1_n, _, pv1_n = _compress_stage1(jnp.arange(DEC_BATCH, dtype=jnp.int32), new_page(0)[None], 0,
                                          new_page(1)[None], 0, wk, wv, 1, DEC_BATCH, "nsa_cmp_sample_new")
    first_chunk = lambda pn: pn[0].reshape(DEC_BATCH, CHUNKS_PER_PAGE * 8, HEAD_DIM)[:, 0:8].reshape(-1, HEAD_DIM)
    kc_s = _compress_finish(pk0, pk1, first_chunk(pk1_n), pe_k, w1k, w2k, "nsa_cmp_fin_k_sample")
    vc_s = _compress_finish(pv0, pv1, first_chunk(pv1_n), pe_v, w1v, w2v, "nsa_cmp_fin_v_sample")

    pool2 = lambda c: c.reshape(-1, HEAD_DIM)
    ks_s, vs_s = _gather_pages(pt_flat, pool2(cache_k_sel), pool2(cache_v_sel), new_page(2), new_page(3),
                               SAMPLE_KEY_PAGES)
    kw_all = jnp.concatenate([cache_k_win[0], kv_s[4]], axis=1)
    vw_all = jnp.concatenate([cache_v_win[0], kv_s[5]], axis=1)
    band = lambda w: jnp.pad(w.transpose(0, 2, 1, 3),
                             ((0, 0), (0, 0), (0, WBAND_SAMPLE - w.shape[1]), (0, 0))).astype(BF16)
    q_s = qkv[M_PROMPT:, :Q_DIM].reshape(DEC_BATCH, DEC_SEQ, N_KV, GQA_R, HEAD_DIM).transpose(0, 2, 3, 1, 4)
    q_s = q_s.reshape(DEC_BATCH, N_KV, GQA_R * DEC_SEQ, HEAD_DIM)
    o_s = _attn_sample(q_s, gates_t[:, M_PROMPT:], kc_s, vc_s, ks_s, vs_s, band(kw_all), band(vw_all))
    o_all = _attn_prompt(qkv, gates_t, kc_p, vc_p, jnp.pad(o_s.astype(BF16), ((M_PROMPT, 0), (0, 0))))

    h = _proj_norm_res(o_all, nsa_w_out.astype(BF16), 0, _row(norm_mix_post[0]), h, 2, "nsa_out_proj")
    h = residual_tail(h, 0)

    w_pw1 = conv_w_pw1.astype(BF16)
    u = _dual_proj(h, _row(norm_mix_pre[1]), w_pw1, w_pw1, 0, D_MODEL, D_MODEL, _row(conv_b_pw1[0]), "glu", F32,
                   "conv_pw1_glu")
    wdw = jnp.pad(conv_w_dw[0], ((0, HALO - CONV_W), (0, 0)))
    conv_args = (wdw, _row(conv_b_dw[0]), _row(conv_ln_g[0]), _row(conv_ln_b[0]), conv_w_pw2.astype(BF16),
                 _row(conv_b_pw2[0]), _row(norm_mix_post[1]))
    t_p = 256
    state = jnp.pad(state_conv[0], ((0, 0), (HALO - (CONV_W - 1), 0), (0, 0))).reshape(DEC_BATCH * HALO, D_MODEL)
    h_s = _conv_tail(u, state, *conv_args, h, t=DEC_SEQ, n_tiles=DEC_BATCH, row0=M_PROMPT, tiles_per_seq=1,
                     halo_is_state=True)
    h = _conv_tail(u, u, *conv_args, h, t=t_p, n_tiles=M_PROMPT // t_p, row0=0, tiles_per_seq=SEQ // t_p,
                   halo_is_state=False, out_init=jnp.pad(h_s, ((M_PROMPT, 0), (0, 0))))
    y_p, y_s = residual_tail(h, 1, split=True)

    y_prompt = y_p.reshape(BATCH, SEQ, D_MODEL)
    y_sample = y_s[-M_SAMPLE:].reshape(DEC_BATCH, DEC_SEQ, D_MODEL)
    kv_p = kvp.reshape(6, 1, BATCH, SEQ, N_KV, HEAD_DIM)
    outs = [y_prompt, y_sample]
    for c in range(4):
        outs.append(kv_p[c])
        outs.append(kv_s[c][None])
    w_keep = min(WINDOW, SEQ)
    outs.append(kv_p[4][:, :, SEQ - w_keep:])
    outs.append(kw_all[None, :, -WINDOW:])
    outs.append(kv_p[5][:, :, SEQ - w_keep:])
    outs.append(vw_all[None, :, -WINDOW:])
    n_keep = CONV_W - 1
    outs.append(jnp.stack([u[b * SEQ + SEQ - n_keep:(b + 1) * SEQ] for b in range(BATCH)])[None])
    u_s = u[M_PROMPT:].reshape(DEC_BATCH, DEC_SEQ, D_MODEL)
    outs.append(jnp.concatenate([state_conv[0], u_s], axis=1)[None, :, -n_keep:])
    return tuple(outs)
```

```python
import functools
import math

import numpy as np
import jax
import jax.numpy as jnp
from jax import lax
from jax.experimental import pallas as pl
from jax.experimental.pallas import tpu as pltpu

F32 = jnp.float32
BF16 = jnp.bfloat16

D_MODEL = 2048
BATCH = 2
SEQ = 4096
DEPTH = 2
DEC_BATCH = 8
DEC_SEQ = 8
PAST_LEN = 16384
PAGE_SIZE = 128
N_HEADS = 16
HEAD_DIM = 128
N_KV = 4
GQA_R = 4
Q_DIM = 2048
KV_DIM = 512
CMP_BLOCK = 32
CMP_STRIDE = 16
SEL_BLOCK = 64
TOPK = 16
WINDOW = 512
ROPE_THETA = 10000.0
CONV_W = 31
D_FF = 5632
PLE_DIM = 256
RMS_EPS = 1e-6
LN_EPS = 1e-5
NEG = -1e30

M_PROMPT = BATCH * SEQ
M_SAMPLE = DEC_BATCH * DEC_SEQ
M_ALL = M_PROMPT + M_SAMPLE
N_PAGES = PAST_LEN // PAGE_SIZE
PAGE_ROWS = PAGE_SIZE * N_KV
CHUNKS_PER_PAGE = PAGE_SIZE // CMP_STRIDE
PAGES_PER_STEP = 8

TM = 688
TN = 512
TK_ATT = 512
TQ_PROMPT = 128
ATT_HEADS_PER_STEP = 2
WBAND_PROMPT = WINDOW + TQ_PROMPT
WBAND_SAMPLE = WINDOW + SEL_BLOCK
VMEM_LIMIT = 56 * 1024 * 1024

Q_PRESCALE = HEAD_DIM ** -0.5 * math.log2(math.e)


def _cparams(sem):
    return pltpu.CompilerParams(dimension_semantics=sem, vmem_limit_bytes=VMEM_LIMIT)


def _rms_scale(x, g):
    ms = jnp.mean(x * x, axis=-1, keepdims=True)
    return x * lax.rsqrt(ms + RMS_EPS) * g


def _dot(a, b):
    return jnp.dot(a, b, preferred_element_type=F32)


def _dot_nt(a, b):
    return lax.dot_general(a, b, (((1,), (1,)), ((), ())), preferred_element_type=F32)


N_QKV_TILES = (Q_DIM + 6 * KV_DIM) // TN
N_Q_TILES = Q_DIM // TN


def _in_proj_kernel(x_ref, g_ref, w_ref, wg_ref, cos_ref, sin_ref, qkv_ref, kvp_ref, kvs_ref, gate_ref, xn_ref):
    j = pl.program_id(1)

    @pl.when(j == 0)
    def _():
        xn = _rms_scale(x_ref[...], g_ref[...]).astype(BF16)
        xn_ref[...] = xn
        gate_ref[...] = jax.nn.sigmoid(_dot(xn, wg_ref[...]))

    acc = _dot(xn_ref[...], w_ref[...])
    is_rope = (j < N_Q_TILES + 1) | (j == N_Q_TILES + 2) | (j == N_Q_TILES + 4)
    post = jnp.where(j < N_Q_TILES, Q_PRESCALE, 1.0)
    cos = jnp.where(is_rope, cos_ref[...], 1.0) * post
    sin = jnp.where(is_rope, sin_ref[...], 0.0) * post
    parts = []
    for h in range(TN // HEAD_DIM):
        a = acc[:, h * HEAD_DIM:(h + 1) * HEAD_DIM]
        parts.append(a * cos + pltpu.roll(a, HEAD_DIM // 2, axis=1) * sin)
    qkv_ref[...] = jnp.concatenate(parts, axis=1).astype(BF16)

    @pl.when(j >= N_Q_TILES)
    def _():
        for g in range(N_KV):
            kvp_ref[pl.ds(g, TM, stride=N_KV), :] = parts[g]
            kvs_ref[pl.ds(g, M_SAMPLE, stride=N_KV), :] = parts[g][TM - M_SAMPLE:]


def _in_proj(x, g, w, wg, cos, sin):
    m = x.shape[0]
    assert m == M_ALL and (m // TM - 1) * TM + (TM - M_SAMPLE) == M_PROMPT
    return pl.pallas_call(
        _in_proj_kernel,
        grid=(m // TM, N_QKV_TILES),
        in_specs=[
            pl.BlockSpec((TM, D_MODEL), lambda i, j: (i, 0)),
            pl.BlockSpec((1, D_MODEL), lambda i, j: (0, 0)),
            pl.BlockSpec((None, D_MODEL, TN), lambda i, j: (0, 0, j)),
            pl.BlockSpec((D_MODEL, HEAD_DIM), lambda i, j: (0, 0)),
            pl.BlockSpec((TM, HEAD_DIM), lambda i, j: (i, 0)),
            pl.BlockSpec((TM, HEAD_DIM), lambda i, j: (i, 0)),
        ],
        out_specs=[
            pl.BlockSpec((TM, TN), lambda i, j: (i, j)),
            pl.BlockSpec((None, TM * N_KV, HEAD_DIM), lambda i, j: (jnp.maximum(j - N_Q_TILES, 0), i, 0)),
            pl.BlockSpec((None, M_SAMPLE * N_KV, HEAD_DIM), lambda i, j: (jnp.maximum(j - N_Q_TILES, 0), i, 0)),
            pl.BlockSpec((TM, HEAD_DIM), lambda i, j: (i, 0)),
        ],
        out_shape=[
            jax.ShapeDtypeStruct((m, Q_DIM + 6 * KV_DIM), BF16),
            jax.ShapeDtypeStruct((6, M_PROMPT * N_KV, HEAD_DIM), F32),
            jax.ShapeDtypeStruct((6, (m // TM) * M_SAMPLE * N_KV, HEAD_DIM), F32),
            jax.ShapeDtypeStruct((m, HEAD_DIM), F32),
        ],
        scratch_shapes=[pltpu.VMEM((TM, D_MODEL), BF16)],
        compiler_params=_cparams(("parallel", "arbitrary")),
        name="nsa_in_proj",
    )(x, g, w, wg, cos, sin)


def _dual_kernel(*refs, mode, has_bias):
    if has_bias:
        x_ref, g_ref, wa_ref, wb_ref, ba_ref, bb_ref, o_ref, xn_ref = refs
    else:
        x_ref, g_ref, wa_ref, wb_ref, o_ref, xn_ref = refs

    @pl.when(pl.program_id(1) == 0)
    def _():
        xn_ref[...] = _rms_scale(x_ref[...], g_ref[...]).astype(BF16)

    xn = xn_ref[...]
    a = _dot(xn, wa_ref[...])
    b = _dot(xn, wb_ref[...])
    if has_bias:
        a = a + ba_ref[...]
        b = b + bb_ref[...]
    if mode == "swiglu":
        o = jax.nn.silu(a) * b
    else:
        o = a * jax.nn.sigmoid(b)
    o_ref[...] = o.astype(o_ref.dtype)


def _dual_proj(x, g, wa, wb, layer, n_out, b_col0, bias, mode, out_dtype, name):
    m = x.shape[0]
    nb = b_col0 // TN
    in_specs = [
        pl.BlockSpec((TM, D_MODEL), lambda i, j: (i, 0)),
        pl.BlockSpec((1, D_MODEL), lambda i, j: (0, 0)),
        pl.BlockSpec((None, D_MODEL, TN), lambda i, j: (layer, 0, j)),
        pl.BlockSpec((None, D_MODEL, TN), lambda i, j: (layer, 0, j + nb)),
    ]
    args = [x, g, wa, wb]
    if bias is not None:
        in_specs += [pl.BlockSpec((1, TN), lambda i, j: (0, j)), pl.BlockSpec((1, TN), lambda i, j: (0, j + nb))]
        args += [bias, bias]
    return pl.pallas_call(
        functools.partial(_dual_kernel, mode=mode, has_bias=bias is not None),
        grid=(m // TM, n_out // TN),
        in_specs=in_specs,
        out_specs=pl.BlockSpec((TM, TN), lambda i, j: (i, j)),
        out_shape=jax.ShapeDtypeStruct((m, n_out), out_dtype),
        scratch_shapes=[pltpu.VMEM((TM, D_MODEL), BF16)],
        compiler_params=_cparams(("parallel", "arbitrary")),
        name=name,
    )(*args)


def _proj_norm_res_kernel(a_ref, w_ref, g_ref, h_ref, o_ref, acc_ref, *, nk):
    k = pl.program_id(1)
    part = _dot(a_ref[...], w_ref[...])

    @pl.when(k == 0)
    def _():
        acc_ref[...] = part

    @pl.when(k > 0)
    def _():
        acc_ref[...] += part

    @pl.when(k == nk - 1)
    def _():
        o_ref[...] = h_ref[...] + _rms_scale(acc_ref[...], g_ref[...])


def _proj_norm_res(a, w, layer, g, h, nk, name):
    m, kdim = a.shape
    tk = kdim // nk
    return pl.pallas_call(
        functools.partial(_proj_norm_res_kernel, nk=nk),
        grid=(m // TM, nk),
        in_specs=[
            pl.BlockSpec((TM, tk), lambda i, k: (i, k)),
            pl.BlockSpec((None, tk, D_MODEL), lambda i, k: (layer, k, 0)),
            pl.BlockSpec((1, D_MODEL), lambda i, k: (0, 0)),
            pl.BlockSpec((TM, D_MODEL), lambda i, k: (i, 0)),
        ],
        out_specs=pl.BlockSpec((TM, D_MODEL), lambda i, k: (i, 0)),
        out_shape=jax.ShapeDtypeStruct((m, D_MODEL), F32),
        scratch_shapes=[pltpu.VMEM((TM, D_MODEL), F32)],
        compiler_params=_cparams(("parallel", "arbitrary")),
        name=name,
    )(a, w, g, h)


def _ple_kernel(h_ref, hc_ref, p_ref, wg_ref, wp_ref, o_ref, *rest, split):
    hb_ref = rest[-1]

    @pl.when(pl.program_id(1) == 0)
    def _():
        hb_ref[...] = h_ref[...].astype(BF16)

    gate = jax.nn.sigmoid(_dot(hb_ref[...], wg_ref[...]))
    out = hc_ref[...] + gate * _dot(p_ref[...], wp_ref[...])
    o_ref[...] = out
    if split:
        rest[0][...] = out[TM - M_SAMPLE:]


def _ple(h, p, wg, wp, layer, name, split=False):
    m = h.shape[0]
    out_specs = pl.BlockSpec((TM, TN), lambda i, j: (i, j))
    out_shape = jax.ShapeDtypeStruct((m, D_MODEL), F32)
    if split:
        assert (m // TM - 1) * TM + (TM - M_SAMPLE) == M_PROMPT
        out_specs = [out_specs, pl.BlockSpec((M_SAMPLE, TN), lambda i, j: (i, j))]
        out_shape = [jax.ShapeDtypeStruct((M_PROMPT, D_MODEL), F32),
                     jax.ShapeDtypeStruct(((m // TM) * M_SAMPLE, D_MODEL), F32)]
    return pl.pallas_call(
        functools.partial(_ple_kernel, split=split),
        grid=(m // TM, D_MODEL // TN),
        in_specs=[
            pl.BlockSpec((TM, D_MODEL), lambda i, j: (i, 0)),
            pl.BlockSpec((TM, TN), lambda i, j: (i, j)),
            pl.BlockSpec((None, TM, PLE_DIM), lambda i, j: (layer, i, 0)),
            pl.BlockSpec((None, D_MODEL, TN), lambda i, j: (layer, 0, j)),
            pl.BlockSpec((None, PLE_DIM, TN), lambda i, j: (layer, 0, j)),
        ],
        out_specs=out_specs,
        out_shape=out_shape,
        scratch_shapes=[pltpu.VMEM((TM, D_MODEL), BF16)],
        compiler_params=_cparams(("parallel", "arbitrary")),
        name=name,
    )(h, h, p, wg, wp)


TILES_PER_CHUNK = CMP_STRIDE * N_KV // 8


def _compress_kernel(pt_ref, *refs):
    n = PAGES_PER_STEP
    k_refs, v_refs = refs[0:n], refs[n:2 * n]
    wk_ref, wv_ref, pk0_ref, pk1_ref, pv0_ref, pv1_ref = refs[2 * n:2 * n + 6]
    for srcs, w_ref, o0_ref, o1_ref in ((k_refs, wk_ref, pk0_ref, pk1_ref), (v_refs, wv_ref, pv0_ref, pv1_ref)):
        acc = None
        for jj in range(TILES_PER_CHUNK // 2):
            halves = []
            for j in (2 * jj, 2 * jj + 1):
                rows = [srcs[p][(c * TILES_PER_CHUNK + j) * 8:(c * TILES_PER_CHUNK + j + 1) * 8, :]
                        for p in range(n) for c in range(CHUNKS_PER_PAGE)]
                halves.append(jnp.concatenate(rows, axis=0))
            term = _dot(jnp.concatenate(halves, axis=1).astype(BF16), w_ref[jj])
            acc = term if acc is None else acc + term
        width = 2 * HEAD_DIM
        folded = acc[:, :width] + pltpu.roll(acc[:, width:], acc.shape[0] - N_KV, axis=0)
        o0_ref[0] = folded[:, :HEAD_DIM]
        o1_ref[0] = folded[:, HEAD_DIM:]


def _compress_stage1(page_table, k_src, k_type, v_src, v_type, wk, wv, n_seq, pages_per_seq, name):
    n = PAGES_PER_STEP
    groups = pages_per_seq // n
    n_chunks = pages_per_seq * CHUNKS_PER_PAGE

    def src_spec(slot, typ):
        return pl.BlockSpec(
            (None, PAGE_ROWS, HEAD_DIM), lambda b, t, pt: (typ, pt[b * pages_per_seq + t * n + slot], 0))

    w_spec = pl.BlockSpec(wk.shape, lambda b, t, pt: (0, 0, 0))
    out_spec = pl.BlockSpec((1, n * CHUNKS_PER_PAGE * 8, HEAD_DIM), lambda b, t, pt: (b, t, 0))
    out_sds = jax.ShapeDtypeStruct((n_seq, n_chunks * 8, HEAD_DIM), F32)
    return pl.pallas_call(
        _compress_kernel,
        grid_spec=pltpu.PrefetchScalarGridSpec(
            num_scalar_prefetch=1,
            grid=(n_seq, groups),
            in_specs=[src_spec(s, k_type) for s in range(n)] + [src_spec(s, v_type) for s in range(n)]
            + [w_spec, w_spec],
            out_specs=[out_spec] * 4,
        ),
        out_shape=[out_sds] * 4,
        compiler_params=_cparams(("parallel", "arbitrary")),
        name=name,
    )(page_table, *([k_src] * n), *([v_src] * n), wk, wv)


def _compress_finish_kernel(p0_ref, p1_ref, pn_ref, pe_ref, w1_ref, w2_ref, o_ref):
    c = o_ref.shape[2]
    bias = _dot(pe_ref[...].astype(BF16), w1_ref[...])[0:1]
    row = lax.broadcasted_iota(jnp.int32, (c, 1), 0)
    for g in range(N_KV):
        p0 = p0_ref[0, pl.ds(g, c, stride=8), :]
        p1 = p1_ref[0, pl.ds(g, c, stride=8), :]
        nxt = pltpu.roll(p1, c - 1, axis=0)
        nxt = jnp.where(row == c - 1, pn_ref[g:g + 1, :], nxt)
        hid = p0 + nxt + bias
        o_ref[0, g] = _dot(jax.nn.silu(hid).astype(BF16), w2_ref[...]).astype(o_ref.dtype)


def _compress_finish(p0, p1, p1_next, pe8, w1, w2, name):
    n_seq, rows, _ = p0.shape
    c = rows // 8
    seq_spec = pl.BlockSpec((1, rows, HEAD_DIM), lambda b: (b, 0, 0))
    return pl.pallas_call(
        _compress_finish_kernel,
        grid=(n_seq,),
        in_specs=[
            seq_spec, seq_spec,
            pl.BlockSpec((8, HEAD_DIM), lambda b: (b, 0)),
            pl.BlockSpec((8, CMP_BLOCK * HEAD_DIM), lambda b: (0, 0)),
            pl.BlockSpec((CMP_BLOCK * HEAD_DIM, HEAD_DIM), lambda b: (0, 0)),
            pl.BlockSpec((HEAD_DIM, HEAD_DIM), lambda b: (0, 0)),
        ],
        out_specs=pl.BlockSpec((1, N_KV, c, HEAD_DIM), lambda b: (b, 0, 0, 0)),
        out_shape=jax.ShapeDtypeStruct((n_seq, N_KV, c, HEAD_DIM), BF16),
        compiler_params=_cparams(("parallel",)),
        name=name,
    )(p0, p1, p1_next, pe8, w1, w2)


def _masked_softmax2(s, mask):
    s = jnp.where(mask, s, NEG)
    m = jnp.max(s, axis=-1, keepdims=True)
    e = jnp.where(mask, jnp.exp2(s - m), 0.0)
    return e / jnp.maximum(jnp.sum(e, axis=-1, keepdims=True), 1e-30)


def _split3(x):
    parts = []
    rem = x
    for _ in range(3):
        part = rem.astype(BF16)
        parts.append(part)
        rem = rem - part.astype(F32)
    return parts


def _sel_tile(kt, carry, qa, t_row, ks, vs, e_ref, e_row0, tk, masked, lanes=slice(0, HEAD_DIM)):
    m, l, acc = carry
    r0 = pl.multiple_of(kt * tk, tk)
    e0 = pl.multiple_of(kt * tk - e_row0, tk)
    ka = jnp.concatenate([ks[pl.ds(r0, tk), lanes], e_ref[pl.ds(e0, tk), :]], axis=1)
    sc = _dot_nt(qa, ka)
    if masked:
        kpos = r0 + lax.broadcasted_iota(jnp.int32, (1, tk), 1)
        sc = jnp.where(kpos <= t_row, sc, NEG)
    m_new = jnp.maximum(m, jnp.max(sc, axis=-1, keepdims=True))
    alpha = jnp.exp2(m - m_new)
    p = jnp.exp2(sc - m_new)
    l = alpha * l + jnp.sum(p, axis=-1, keepdims=True)
    acc = alpha * acc + _dot(p.astype(BF16), vs[pl.ds(r0, tk), lanes])
    return m_new, l, acc


def _sel_init(rows):
    return (jnp.full((rows, 1), -jnp.inf, F32), jnp.zeros((rows, 1), F32), jnp.zeros((rows, HEAD_DIM), F32))


def _gated_merge(gt, o_c, o_s, o_w, tq):
    outs = []
    for r in range(GQA_R):
        sl = slice(r * tq, (r + 1) * tq)
        outs.append(gt[:, r:r + 1] * o_c[sl] + gt[:, GQA_R + r:GQA_R + r + 1] * o_s[sl]
                    + gt[:, 2 * GQA_R + r:2 * GQA_R + r + 1] * o_w[sl])
    return jnp.concatenate(outs, axis=1)


def _cmp_to_sel_matrix(nb, nsp):
    n = np.arange(nb)[:, None]
    j = np.arange(nsp)[None, :]
    return ((n >= 4 * j) & (n <= 4 * j + 3)).astype(np.float32) + ((n >= 4 * j - 1) & (n <= 4 * j + 2)).astype(np.float32)


def _block_onehot(n_keys):
    k = np.arange(n_keys)[:, None] // SEL_BLOCK
    return jnp.asarray((k % 128 == np.arange(128)[None, :]).astype(np.float32), BF16)


def _window_bias_prompt():
    q = np.arange(TQ_PROMPT)[:, None]
    kb = np.arange(WBAND_PROMPT)[None, :]
    pats = [kb <= TQ_PROMPT * i + q for i in range(WINDOW // TQ_PROMPT)]
    pats.append((kb > q) & (kb <= q + WINDOW))
    return jnp.asarray(np.where(np.stack(pats), 0.0, NEG), F32)


def _attn_prompt_kernel(q_ref, gate_ref, kc_ref, vc_ref, ks_ref, vs_ref, kw_ref, vw_ref, mmt_ref, e_ref, wb_ref,
                        o_init_ref, o_ref, *, nb, ns):
    del o_init_ref
    tq = TQ_PROMPT
    rows = GQA_R * tq
    qi = pl.program_id(2)
    t0 = qi * tq
    t_row = t0 + (lax.broadcasted_iota(jnp.int32, (rows, 1), 0) & (tq - 1))
    cmp_end = lax.broadcasted_iota(jnp.int32, (1, nb), 1) * CMP_STRIDE + (CMP_BLOCK - 1)
    jb = lax.broadcasted_iota(jnp.int32, (ns, tq), 0)
    cur = lax.shift_right_logical(t0 + lax.broadcasted_iota(jnp.int32, (1, tq), 1), 6)
    forced = (jb == 0) | (jb == cur) | (jb == cur - 1)
    eligible = jb <= cur
    groups = ns // 8
    jgrp = [jb[8 * v:8 * v + 8] for v in range(groups)]
    eye = jnp.where(lax.broadcasted_iota(jnp.int32, (tq, tq), 0) == lax.broadcasted_iota(jnp.int32, (tq, tq), 1),
                    1.0, 0.0).astype(BF16)
    mmt = mmt_ref[...]

    heads = range(ATT_HEADS_PER_STEP)
    qs, o_cs, qas = [], [], []
    for h in heads:
        qq = q_ref[:, h * GQA_R * HEAD_DIM:(h + 1) * GQA_R * HEAD_DIM]
        q = jnp.concatenate([qq[:, r * HEAD_DIM:(r + 1) * HEAD_DIM] for r in range(GQA_R)], axis=0)

        s = _dot_nt(q, kc_ref[0, h])
        p_c = _masked_softmax2(s, cmp_end <= t_row)
        o_cs.append(_dot(p_c.astype(BF16), vc_ref[0, h]))
        imp = p_c[0:tq]
        for r in range(1, GQA_R):
            imp = imp + p_c[r * tq:(r + 1) * tq]

        blk_imp = None
        for part in _split3(imp):
            term = _dot_nt(mmt, part)
            blk_imp = term if blk_imp is None else blk_imp + term
        score = jnp.where(forced, jnp.inf, jnp.where(eligible, blk_imp, -jnp.inf))
        sgrp = [score[8 * v:8 * v + 8] for v in range(groups)]
        rank = [jnp.zeros((8, tq), F32) for _ in range(groups)]
        for i in range(ns):
            ci = score[i:i + 1]
            for v in range(groups):
                if 8 * v > i:
                    beats = ci >= sgrp[v]
                elif 8 * v + 7 <= i:
                    beats = ci > sgrp[v]
                else:
                    beats = (ci > sgrp[v]) | ((ci == sgrp[v]) & (jgrp[v] > i))
                rank[v] = rank[v] + jnp.where(beats, 1.0, 0.0)
        rank = jnp.concatenate(rank, axis=0)
        sel_t = jnp.where((rank < float(TOPK)) & eligible, 1.0, 0.0)
        sel_t = jnp.concatenate([sel_t, jnp.zeros((128 - ns, tq), F32)], axis=0).astype(BF16)
        sel = _dot_nt(eye, sel_t)
        negsel = jnp.where(sel > 0.5, 0.0, NEG).astype(BF16)
        qs.append(q)
        qas.append(jnp.concatenate([q, jnp.concatenate([negsel] * GQA_R, axis=0)], axis=1))

    n_tiles = lax.shift_right_logical(t0 + tq - 1, TK_ATT.bit_length() - 1) + 1

    def tiles(kt, carries, masked):
        return tuple(
            _sel_tile(kt, carries[h], qa=qas[h], t_row=t_row, ks=ks_ref, vs=vs_ref, e_ref=e_ref, e_row0=0,
                      tk=TK_ATT, masked=masked, lanes=slice(h * HEAD_DIM, (h + 1) * HEAD_DIM))
            for h in heads)

    carries = lax.fori_loop(0, n_tiles - 1, functools.partial(tiles, masked=False),
                            tuple(_sel_init(rows) for _ in heads))
    carries = tiles(n_tiles - 1, carries, masked=True)

    wstart = pl.multiple_of(jnp.maximum(t0 - WINDOW, 0), tq)
    bias = wb_ref[jnp.minimum(qi, WINDOW // tq)]
    bias = jnp.concatenate([bias] * GQA_R, axis=0)
    for h in heads:
        lanes = slice(h * HEAD_DIM, (h + 1) * HEAD_DIM)
        _, l, acc = carries[h]
        o_s = acc / jnp.maximum(l, 1e-30)
        s = _dot_nt(qs[h], kw_ref[pl.ds(wstart, WBAND_PROMPT), lanes]) + bias
        e = jnp.exp2(s - jnp.max(s, axis=-1, keepdims=True))
        p_w = e / jnp.sum(e, axis=-1, keepdims=True)
        o_w = _dot(p_w.astype(BF16), vw_ref[pl.ds(wstart, WBAND_PROMPT), lanes])
        o_ref[:, h * GQA_R * HEAD_DIM:(h + 1) * GQA_R * HEAD_DIM] = _gated_merge(
            gate_ref[h], o_cs[h], o_s, o_w, tq).astype(o_ref.dtype)


def _attn_prompt(qkv, gates_t, kc, vc, o_init):
    tq = TQ_PROMPT
    hp = ATT_HEADS_PER_STEP
    nq = SEQ // tq
    nb = SEQ // CMP_STRIDE
    ns = SEQ // SEL_BLOCK
    k_sel0, v_sel0 = Q_DIM + 2 * KV_DIM, Q_DIM + 3 * KV_DIM
    k_win0, v_win0 = Q_DIM + 4 * KV_DIM, Q_DIM + 5 * KV_DIM
    seq_spec = lambda base: pl.BlockSpec((SEQ, hp * HEAD_DIM), lambda b, g, i: (b, base // (hp * HEAD_DIM) + g))
    blk_spec = pl.BlockSpec((1, hp, nb, HEAD_DIM), lambda b, g, i: (b, g, 0, 0))
    mmt = jnp.asarray(_cmp_to_sel_matrix(nb, ns).T, BF16)
    wbias = _window_bias_prompt()
    return pl.pallas_call(
        functools.partial(_attn_prompt_kernel, nb=nb, ns=ns),
        grid=(BATCH, N_KV // hp, nq),
        in_specs=[
            pl.BlockSpec((tq, hp * GQA_R * HEAD_DIM), lambda b, g, i: (b * nq + i, g)),
            pl.BlockSpec((hp, tq, HEAD_DIM), lambda b, g, i: (g, b * nq + i, 0)),
            blk_spec, blk_spec,
            seq_spec(k_sel0), seq_spec(v_sel0), seq_spec(k_win0), seq_spec(v_win0),
            pl.BlockSpec((ns, nb), lambda b, g, i: (0, 0)),
            pl.BlockSpec((SEQ, 128), lambda b, g, i: (0, 0)),
            pl.BlockSpec(wbias.shape, lambda b, g, i: (0, 0, 0)),
            pl.BlockSpec(memory_space=pl.ANY),
        ],
        out_specs=pl.BlockSpec((tq, hp * GQA_R * HEAD_DIM), lambda b, g, i: (b * nq + i, g)),
        out_shape=jax.ShapeDtypeStruct((M_ALL, Q_DIM), BF16),
        input_output_aliases={11: 0},
        compiler_params=_cparams(("parallel", "parallel", "arbitrary")),
        name="nsa_attn_prompt",
    )(qkv, gates_t, kc, vc, qkv, qkv, qkv, qkv, mmt, _block_onehot(SEQ), wbias, o_init)


SAMPLE_NB = PAST_LEN // CMP_STRIDE
SAMPLE_NS = -(-(PAST_LEN + DEC_SEQ) // SEL_BLOCK)
SAMPLE_NSP = 384
SAMPLE_PAGES_PER_STEP = 16
KEYS_PER_STEP = SAMPLE_PAGES_PER_STEP * PAGE_SIZE
BLOCKS_PER_STEP = KEYS_PER_STEP // SEL_BLOCK
SAMPLE_STEPS = N_PAGES // SAMPLE_PAGES_PER_STEP


def _attn_sample_kernel(pt_ref, q_ref, gate_ref, kc_ref, vc_ref, kw_ref, vw_ref, mmat_ref, e_ref, *refs):
    n = SAMPLE_PAGES_PER_STEP
    k_refs, v_refs = refs[0:n], refs[n:2 * n]
    kn_ref, vn_ref, o_ref, qa_ref, m_ref, l_ref, acc_ref, oc_ref, ow_ref = refs[2 * n:]
    tq, nb, ns, nsp = DEC_SEQ, SAMPLE_NB, SAMPLE_NS, SAMPLE_NSP
    rows = GQA_R * tq
    t0 = PAST_LEN
    t = pl.program_id(1)
    t_row = t0 + (lax.broadcasted_iota(jnp.int32, (rows, 1), 0) & (tq - 1))

    @pl.when(t == 0)
    def _():
        cmp_end = lax.broadcasted_iota(jnp.int32, (1, nb), 1) * CMP_STRIDE + (CMP_BLOCK - 1)
        jb = lax.broadcasted_iota(jnp.int32, (tq, nsp), 1)
        cur = lax.shift_right_logical(t0 + lax.broadcasted_iota(jnp.int32, (tq, 1), 0), 6)
        forced = (jb == 0) | (jb == cur) | (jb == cur - 1)
        eligible = jb <= cur
        kpos = (PAST_LEN - WINDOW) + lax.broadcasted_iota(jnp.int32, (1, WBAND_SAMPLE), 1)
        win_mask = (kpos <= t_row) & (kpos > t_row - WINDOW)
        mmat = mmat_ref[...]
        for g in range(N_KV):
            q = q_ref[0, g]

            p_c = _masked_softmax2(_dot_nt(q, kc_ref[0, g]), cmp_end <= t_row)
            oc_ref[g] = _dot(p_c.astype(BF16), vc_ref[0, g])
            imp = p_c[0:tq]
            for r in range(1, GQA_R):
                imp = imp + p_c[r * tq:(r + 1) * tq]

            blk_imp = None
            for part in _split3(imp):
                term = _dot(part, mmat)
                blk_imp = term if blk_imp is None else blk_imp + term
            score = jnp.where(forced, jnp.inf, jnp.where(eligible, blk_imp, -jnp.inf))
            rank = jnp.zeros((tq, nsp), F32)
            for i in range(ns):
                ci = score[:, i:i + 1]
                beats = (ci > score) | ((ci == score) & (jb > i))
                rank = rank + jnp.where(beats, 1.0, 0.0)
            negsel = jnp.where((rank < float(TOPK)) & eligible, 0.0, NEG)
            negsel = jnp.concatenate([negsel] * GQA_R, axis=0).astype(BF16)
            pad = jnp.zeros((rows, HEAD_DIM - BLOCKS_PER_STEP), BF16)
            for st in range(SAMPLE_STEPS + 1):
                piece = negsel[:, st * BLOCKS_PER_STEP:(st + 1) * BLOCKS_PER_STEP]
                qa_ref[g, st] = jnp.concatenate([q, piece, pad], axis=1)

            p_w = _masked_softmax2(_dot_nt(q, kw_ref[0, g]), win_mask)
            ow_ref[g] = _dot(p_w.astype(BF16), vw_ref[0, g])

            m_ref[g] = jnp.full((rows, 1), -jnp.inf, F32)
            l_ref[g] = jnp.zeros((rows, 1), F32)
            acc_ref[g] = jnp.zeros((rows, HEAD_DIM), F32)

    def update(g, qa, ka, v, mask):
        sc = _dot_nt(qa, ka)
        if mask is not None:
            sc = jnp.where(mask, sc, NEG)
        m = m_ref[g]
        m_new = jnp.maximum(m, jnp.max(sc, axis=-1, keepdims=True))
        alpha = jnp.exp2(m - m_new)
        p = jnp.exp2(sc - m_new)
        l_ref[g] = alpha * l_ref[g] + jnp.sum(p, axis=-1, keepdims=True)
        acc_ref[g] = alpha * acc_ref[g] + _dot(p.astype(BF16), v)
        m_ref[g] = m_new

    def head_rows(page_refs, g):
        return jnp.concatenate([r[pl.ds(g, PAGE_SIZE, stride=N_KV), :] for r in page_refs], axis=0).astype(BF16)

    e_blk = e_ref[...]
    for g in range(N_KV):
        ka = jnp.concatenate([head_rows(k_refs, g), e_blk], axis=1)
        update(g, qa_ref[g, t], ka, head_rows(v_refs, g), None)

    @pl.when(t == SAMPLE_STEPS - 1)
    def _():
        kpos = t0 + lax.broadcasted_iota(jnp.int32, (1, PAGE_SIZE), 1)
        gt = gate_ref[...]
        for g in range(N_KV):
            ka = jnp.concatenate([head_rows([kn_ref], g), e_blk[0:PAGE_SIZE]], axis=1)
            update(g, qa_ref[g, SAMPLE_STEPS], ka, head_rows([vn_ref], g), kpos <= t_row)
            o_s = acc_ref[g] / jnp.maximum(l_ref[g], 1e-30)
            o_ref[:, g * GQA_R * HEAD_DIM:(g + 1) * GQA_R * HEAD_DIM] = _gated_merge(
                gt[g], oc_ref[g], o_s, ow_ref[g], tq)


def _attn_sample(page_table, q_s, gates_s, kc, vc, kw, vw, k_pool, v_pool, k_new, v_new):
    n = SAMPLE_PAGES_PER_STEP
    rows = GQA_R * DEC_SEQ
    seq_spec = lambda r: pl.BlockSpec((1, N_KV, r, HEAD_DIM), lambda b, t, pt: (b, 0, 0, 0))
    page = (PAGE_ROWS, HEAD_DIM)
    pool_spec = lambda slot: pl.BlockSpec(page, lambda b, t, pt: (pt[b * N_PAGES + t * n + slot], 0))
    new_spec = pl.BlockSpec(page, lambda b, t, pt: (b, 0))
    mmat = jnp.asarray(_cmp_to_sel_matrix(SAMPLE_NB, SAMPLE_NSP), BF16)
    return pl.pallas_call(
        _attn_sample_kernel,
        grid_spec=pltpu.PrefetchScalarGridSpec(
            num_scalar_prefetch=1,
            grid=(DEC_BATCH, SAMPLE_STEPS),
            in_specs=[
                seq_spec(rows),
                pl.BlockSpec((N_KV, DEC_SEQ, HEAD_DIM), lambda b, t, pt: (0, b, 0)),
                seq_spec(SAMPLE_NB), seq_spec(SAMPLE_NB),
                seq_spec(WBAND_SAMPLE), seq_spec(WBAND_SAMPLE),
                pl.BlockSpec((SAMPLE_NB, SAMPLE_NSP), lambda b, t, pt: (0, 0)),
                pl.BlockSpec((KEYS_PER_STEP, HEAD_DIM), lambda b, t, pt: (0, 0)),
            ] + [pool_spec(s) for s in range(n)] * 2 + [new_spec, new_spec],
            out_specs=pl.BlockSpec((DEC_SEQ, Q_DIM), lambda b, t, pt: (b, 0)),
            scratch_shapes=[
                pltpu.VMEM((N_KV, SAMPLE_STEPS + 1, rows, 2 * HEAD_DIM), BF16),
                pltpu.VMEM((N_KV, rows, 1), F32),
                pltpu.VMEM((N_KV, rows, 1), F32),
                pltpu.VMEM((N_KV, rows, HEAD_DIM), F32),
                pltpu.VMEM((N_KV, rows, HEAD_DIM), F32),
                pltpu.VMEM((N_KV, rows, HEAD_DIM), F32),
            ],
        ),
        out_shape=jax.ShapeDtypeStruct((M_SAMPLE, Q_DIM), F32),
        compiler_params=_cparams(("parallel", "arbitrary")),
        name="nsa_attn_sample",
    )(page_table, q_s, gates_s, kc, vc, kw, vw, mmat, _block_onehot(KEYS_PER_STEP),
      *([k_pool] * n), *([v_pool] * n), k_new, v_new)


HALO = 32


def _conv_kernel(u_ref, halo_ref, wdw_ref, bdw_ref, lng_ref, lnb_ref, w2_ref, b2_ref, g_ref, h_ref, *rest,
                 t, tiles_per_seq, halo_is_state):
    o_ref, buf_ref, y_ref = rest[-3:]
    i = pl.program_id(0)
    if halo_is_state:
        buf_ref[0:HALO, :] = halo_ref[...]
    else:
        first = (i % tiles_per_seq) == 0
        buf_ref[0:HALO, :] = jnp.where(first, 0.0, halo_ref[...])
    buf_ref[HALO:HALO + t, :] = u_ref[...]
    off = HALO - (CONV_W - 1)
    for c in range(D_MODEL // HEAD_DIM):
        cs = slice(c * HEAD_DIM, (c + 1) * HEAD_DIM)
        base = buf_ref[:, cs]
        w = wdw_ref[:, cs]
        y = bdw_ref[:, cs]
        n_rows = HALO + t
        for s in range(8):
            taps = [k for k in range(CONV_W) if (off + k) % 8 == s]
            shifted = base if s == 0 else pltpu.roll(base, n_rows - s, axis=0)
            for k in taps:
                a = off + k - s
                y = y + shifted[a:a + t] * w[k:k + 1]
        y_ref[:, cs] = y
    y = y_ref[...]
    mu = jnp.mean(y, axis=-1, keepdims=True)
    yc = y - mu
    var = jnp.mean(yc * yc, axis=-1, keepdims=True)
    z = yc * lax.rsqrt(var + LN_EPS) * lng_ref[...] + lnb_ref[...]
    mval = _dot(jax.nn.silu(z).astype(BF16), w2_ref[...]) + b2_ref[...]
    o_ref[...] = h_ref[...] + _rms_scale(mval, g_ref[...])


def _conv_tail(u, halo_src, wdw, bdw, lng, lnb, w2, b2, g, h, *, t, n_tiles, row0, tiles_per_seq, halo_is_state,
               out_init=None):
    blk0 = row0 // t
    extra_specs, extra_args, aliases = [], [], {}
    out_rows = n_tiles * t
    if out_init is not None:
        extra_specs, extra_args, aliases = [pl.BlockSpec(memory_space=pl.ANY)], [out_init], {10: 0}
        out_rows = out_init.shape[0]
    if halo_is_state:
        halo_spec = pl.BlockSpec((HALO, D_MODEL), lambda i: (i, 0))
    else:
        per = t // HALO
        halo_spec = pl.BlockSpec((HALO, D_MODEL), lambda i: (jnp.maximum((blk0 + i) * per - 1, 0), 0))
    vec = pl.BlockSpec((1, D_MODEL), lambda i: (0, 0))
    kern = functools.partial(_conv_kernel, t=t, tiles_per_seq=tiles_per_seq, halo_is_state=halo_is_state)
    return pl.pallas_call(
        kern,
        grid=(n_tiles,),
        in_specs=[
            pl.BlockSpec((t, D_MODEL), lambda i: (blk0 + i, 0)),
            halo_spec,
            pl.BlockSpec((HALO, D_MODEL), lambda i: (0, 0)),
            vec, vec, vec,
            pl.BlockSpec((None, D_MODEL, D_MODEL), lambda i: (0, 0, 0)),
            vec, vec,
            pl.BlockSpec((t, D_MODEL), lambda i: (blk0 + i, 0)),
        ] + extra_specs,
        out_specs=pl.BlockSpec((t, D_MODEL), lambda i: (i, 0)),
        out_shape=jax.ShapeDtypeStruct((out_rows, D_MODEL), F32),
        input_output_aliases=aliases,
        scratch_shapes=[pltpu.VMEM((HALO + t, D_MODEL), F32), pltpu.VMEM((t, D_MODEL), F32)],
        compiler_params=_cparams(("parallel",)),
        name="conv_tail_state" if halo_is_state else "conv_tail",
    )(u, halo_src, wdw, bdw, lng, lnb, w2, b2, g, h, *extra_args)


def _row(v):
    return v.reshape(1, -1).astype(F32)


def _rope_tables():
    half = HEAD_DIM // 2
    pos = jnp.concatenate([jnp.tile(jnp.arange(SEQ), BATCH), jnp.tile(PAST_LEN + jnp.arange(DEC_SEQ), DEC_BATCH)])
    inv = ROPE_THETA ** (-jnp.arange(half, dtype=F32) / half)
    ang = pos.astype(F32)[:, None] * inv[None, :]
    cos, sin = jnp.cos(ang), jnp.sin(ang)
    return jnp.concatenate([cos, cos], axis=1), jnp.concatenate([-sin, sin], axis=1)


def _cmp_weights(w1):
    w1r = w1.reshape(CMP_BLOCK // CMP_STRIDE, CMP_STRIDE * HEAD_DIM, HEAD_DIM)
    per_pos = jnp.concatenate([w1r[0], w1r[1]], axis=1).reshape(CMP_STRIDE, HEAD_DIM, 2 * HEAD_DIM)
    per_tile = jnp.concatenate([per_pos[0::2], per_pos[1::2]], axis=2)
    return per_tile.reshape(TILES_PER_CHUNK // 2, 2 * HEAD_DIM, 4 * HEAD_DIM).astype(BF16)


def kernel(x_prompt, x_sample, p_prompt, p_sample, page_table, cache_k_cmp, cache_v_cmp, cache_k_sel, cache_v_sel, cache_k_win, cache_v_win, state_conv, norm_mix_pre, norm_mix_post, norm_ffn_pre, norm_ffn_post, ffn_w_gate, ffn_w_up, ffn_w_down, ple_w_proj, ple_w_gate, nsa_w_in, nsa_w_out, nsa_cmp_pe_k, nsa_cmp_w1_k, nsa_cmp_w2_k, nsa_cmp_pe_v, nsa_cmp_w1_v, nsa_cmp_w2_v, conv_w_pw1, conv_b_pw1, conv_w_dw, conv_b_dw, conv_ln_g, conv_ln_b, conv_w_pw2, conv_b_pw2):
    h = jnp.concatenate([x_prompt.reshape(M_PROMPT, D_MODEL), x_sample.reshape(M_SAMPLE, D_MODEL)], axis=0)
    p_all = jnp.concatenate([p_prompt.reshape(DEPTH, M_PROMPT, PLE_DIM),
                             p_sample.reshape(DEPTH, M_SAMPLE, PLE_DIM)], axis=1).astype(BF16)
    pt_flat = page_table.reshape(-1)
    w_gate, w_up, w_down = ffn_w_gate.astype(BF16), ffn_w_up.astype(BF16), ffn_w_down.astype(BF16)
    w_ple_gate, w_ple = ple_w_gate.astype(BF16), ple_w_proj.astype(BF16)

    def residual_tail(h, layer, split=False):
        act = _dual_proj(h, _row(norm_ffn_pre[layer]), w_gate, w_up, layer, D_FF, 0, None, "swiglu", BF16,
                         f"ffn_up_{layer}")
        h = _proj_norm_res(act, w_down, layer, _row(norm_ffn_post[layer]), h, 4, f"ffn_down_{layer}")
        return _ple(h, p_all, w_ple_gate, w_ple, layer, f"ple_{layer}", split=split)

    n_main = Q_DIM + 6 * KV_DIM
    w_gates = jnp.pad(nsa_w_in[0][:, n_main:], ((0, 0), (0, HEAD_DIM - 3 * N_HEADS))).astype(BF16)
    cos, sin = _rope_tables()
    qkv, kvp, kvs, gates = _in_proj(h, _row(norm_mix_pre[0]), nsa_w_in.astype(BF16), w_gates, cos, sin)

    gates_t = gates[:, :3 * N_HEADS].reshape(M_ALL, 3, N_KV, GQA_R).transpose(2, 0, 1, 3).reshape(N_KV, M_ALL, 3 * GQA_R)
    gates_t = jnp.pad(gates_t, ((0, 0), (0, 0), (0, HEAD_DIM - 3 * GQA_R)))

    kv_s = kvs[:, -M_SAMPLE * N_KV:].reshape(6, DEC_BATCH, DEC_SEQ, N_KV, HEAD_DIM)
    new_page = lambda c: jnp.pad(kv_s[c], ((0, 0), (0, PAGE_SIZE - DEC_SEQ), (0, 0), (0, 0))).reshape(-1, HEAD_DIM)

    wk, wv = _cmp_weights(nsa_cmp_w1_k[0]), _cmp_weights(nsa_cmp_w1_v[0])
    pe_k = jnp.broadcast_to(nsa_cmp_pe_k[0].reshape(1, -1), (8, CMP_BLOCK * HEAD_DIM))
    pe_v = jnp.broadcast_to(nsa_cmp_pe_v[0].reshape(1, -1), (8, CMP_BLOCK * HEAD_DIM))
    w1k, w1v = nsa_cmp_w1_k[0].astype(BF16), nsa_cmp_w1_v[0].astype(BF16)
    w2k, w2v = nsa_cmp_w2_k[0].astype(BF16), nsa_cmp_w2_v[0].astype(BF16)

    pages_prompt = SEQ // PAGE_SIZE
    ident = jnp.arange(BATCH * pages_prompt, dtype=jnp.int32)
    pk0, pk1, pv0, pv1 = _compress_stage1(ident, kvp, 0, kvp, 1, wk, wv, BATCH, pages_prompt, "nsa_cmp_prompt")
    zero_next = jnp.zeros((BATCH * 8, HEAD_DIM), F32)
    kc_p = _compress_finish(pk0, pk1, zero_next, pe_k, w1k, w2k, "nsa_cmp_fin_k_prompt")
    vc_p = _compress_finish(pv0, pv1, zero_next, pe_v, w1v, w2v, "nsa_cmp_fin_v_prompt")

    pool3 = lambda c: c.reshape(1, -1, HEAD_DIM)
    pk0, pk1, pv0, pv1 = _compress_stage1(pt_flat, pool3(cache_k_cmp), 0, pool3(cache_v_cmp), 0, wk, wv, DEC_BATCH,
                                          N_PAGES, "nsa_cmp_sample")
    _, pk1_n, _, pv1_n = _compress_stage1(jnp.arange(DEC_BATCH, dtype=jnp.int32), new_page(0)[None], 0,
                                          new_page(1)[None], 0, wk, wv, 1, DEC_BATCH, "nsa_cmp_sample_new")
    first_chunk = lambda pn: pn[0].reshape(DEC_BATCH, CHUNKS_PER_PAGE * 8, HEAD_DIM)[:, 0:8].reshape(-1, HEAD_DIM)
    kc_s = _compress_finish(pk0, pk1, first_chunk(pk1_n), pe_k, w1k, w2k, "nsa_cmp_fin_k_sample")
    vc_s = _compress_finish(pv0, pv1, first_chunk(pv1_n), pe_v, w1v, w2v, "nsa_cmp_fin_v_sample")

    pool2 = lambda c: c.reshape(-1, HEAD_DIM)
    kw_all = jnp.concatenate([cache_k_win[0], kv_s[4]], axis=1)
    vw_all = jnp.concatenate([cache_v_win[0], kv_s[5]], axis=1)
    band = lambda w: jnp.pad(w.transpose(0, 2, 1, 3),
                             ((0, 0), (0, 0), (0, WBAND_SAMPLE - w.shape[1]), (0, 0))).astype(BF16)
    q_s = qkv[M_PROMPT:, :Q_DIM].reshape(DEC_BATCH, DEC_SEQ, N_KV, GQA_R, HEAD_DIM).transpose(0, 2, 3, 1, 4)
    q_s = q_s.reshape(DEC_BATCH, N_KV, GQA_R * DEC_SEQ, HEAD_DIM)
    o_s = _attn_sample(pt_flat, q_s, gates_t[:, M_PROMPT:], kc_s, vc_s, band(kw_all), band(vw_all),
                       pool2(cache_k_sel), pool2(cache_v_sel), new_page(2), new_page(3))
    o_all = _attn_prompt(qkv, gates_t, kc_p, vc_p, jnp.pad(o_s.astype(BF16), ((M_PROMPT, 0), (0, 0))))

    h = _proj_norm_res(o_all, nsa_w_out.astype(BF16), 0, _row(norm_mix_post[0]), h, 2, "nsa_out_proj")
    h = residual_tail(h, 0)

    w_pw1 = conv_w_pw1.astype(BF16)
    u = _dual_proj(h, _row(norm_mix_pre[1]), w_pw1, w_pw1, 0, D_MODEL, D_MODEL, _row(conv_b_pw1[0]), "glu", F32,
                   "conv_pw1_glu")
    wdw = jnp.pad(conv_w_dw[0], ((0, HALO - CONV_W), (0, 0)))
    conv_args = (wdw, _row(conv_b_dw[0]), _row(conv_ln_g[0]), _row(conv_ln_b[0]), conv_w_pw2.astype(BF16),
                 _row(conv_b_pw2[0]), _row(norm_mix_post[1]))
    t_p = 256
    state = jnp.pad(state_conv[0], ((0, 0), (HALO - (CONV_W - 1), 0), (0, 0))).reshape(DEC_BATCH * HALO, D_MODEL)
    h_s = _conv_tail(u, state, *conv_args, h, t=DEC_SEQ, n_tiles=DEC_BATCH, row0=M_PROMPT, tiles_per_seq=1,
                     halo_is_state=True)
    h = _conv_tail(u, u, *conv_args, h, t=t_p, n_tiles=M_PROMPT // t_p, row0=0, tiles_per_seq=SEQ // t_p,
                   halo_is_state=False, out_init=jnp.pad(h_s, ((M_PROMPT, 0), (0, 0))))
    y_p, y_s = residual_tail(h, 1, split=True)

    y_prompt = y_p.reshape(BATCH, SEQ, D_MODEL)
    y_sample = y_s[-M_SAMPLE:].reshape(DEC_BATCH, DEC_SEQ, D_MODEL)
    kv_p = kvp.reshape(6, 1, BATCH, SEQ, N_KV, HEAD_DIM)
    outs = [y_prompt, y_sample]
    for c in range(4):
        outs.append(kv_p[c])
        outs.append(kv_s[c][None])
    w_keep = min(WINDOW, SEQ)
    outs.append(kv_p[4][:, :, SEQ - w_keep:])
    outs.append(kw_all[None, :, -WINDOW:])
    outs.append(kv_p[5][:, :, SEQ - w_keep:])
    outs.append(vw_all[None, :, -WINDOW:])
    n_keep = CONV_W - 1
    outs.append(jnp.stack([u[b * SEQ + SEQ - n_keep:(b + 1) * SEQ] for b in range(BATCH)])[None])
    u_s = u[M_PROMPT:].reshape(DEC_BATCH, DEC_SEQ, D_MODEL)
    outs.append(jnp.concatenate([state_conv[0], u_s], axis=1)[None, :, -n_keep:])
    return tuple(outs)
```

```python
import functools
import math

import numpy as np
import jax
import jax.numpy as jnp
from jax import lax
from jax.experimental import pallas as pl
from jax.experimental.pallas import tpu as pltpu

F32 = jnp.float32
BF16 = jnp.bfloat16

D_MODEL = 2048
BATCH = 2
SEQ = 4096
DEPTH = 2
DEC_BATCH = 8
DEC_SEQ = 8
PAST_LEN = 16384
PAGE_SIZE = 128
N_HEADS = 16
HEAD_DIM = 128
N_KV = 4
GQA_R = 4
Q_DIM = 2048
KV_DIM = 512
CMP_BLOCK = 32
CMP_STRIDE = 16
SEL_BLOCK = 64
TOPK = 16
WINDOW = 512
ROPE_THETA = 10000.0
CONV_W = 31
D_FF = 5632
PLE_DIM = 256
RMS_EPS = 1e-6
LN_EPS = 1e-5
NEG = -1e30

M_PROMPT = BATCH * SEQ
M_SAMPLE = DEC_BATCH * DEC_SEQ
M_ALL = M_PROMPT + M_SAMPLE
N_PAGES = PAST_LEN // PAGE_SIZE
PAGE_ROWS = PAGE_SIZE * N_KV
CHUNKS_PER_PAGE = PAGE_SIZE // CMP_STRIDE
PAGES_PER_STEP = 16

TM = 688
TM_ROWS = 688
TN = 512
TK_ATT = 512
TQ_PROMPT = 128
ATT_HEADS_PER_STEP = 2
WBAND_PROMPT = WINDOW + TQ_PROMPT
WBAND_SAMPLE = WINDOW + SEL_BLOCK
VMEM_LIMIT = 56 * 1024 * 1024

Q_PRESCALE = HEAD_DIM ** -0.5 * math.log2(math.e)


def _cparams(sem):
    return pltpu.CompilerParams(dimension_semantics=sem, vmem_limit_bytes=VMEM_LIMIT)


def _rms_scale(x, g):
    ms = jnp.mean(x * x, axis=-1, keepdims=True)
    return x * lax.rsqrt(ms + RMS_EPS) * g


def _dot(a, b):
    return jnp.dot(a, b, preferred_element_type=F32)


def _dot_nt(a, b):
    return lax.dot_general(a, b, (((1,), (1,)), ((), ())), preferred_element_type=F32)


N_QKV_TILES = (Q_DIM + 6 * KV_DIM) // TN
N_Q_TILES = Q_DIM // TN


def _in_proj_kernel(x_ref, g_ref, w_ref, wg_ref, cos_ref, sin_ref, qkv_ref, *rest):
    kvp_refs, (kvs_ref, gate_ref, xn_ref) = rest[:6], rest[6:]
    j = pl.program_id(1)

    @pl.when(j == 0)
    def _():
        xn = _rms_scale(x_ref[...], g_ref[...]).astype(BF16)
        xn_ref[...] = xn
        gate_ref[...] = jax.nn.sigmoid(_dot(xn, wg_ref[...]))

    acc = _dot(xn_ref[...], w_ref[...])
    is_rope = (j < N_Q_TILES + 1) | (j == N_Q_TILES + 2) | (j == N_Q_TILES + 4)
    post = jnp.where(j < N_Q_TILES, Q_PRESCALE, 1.0)
    cos = jnp.where(is_rope, cos_ref[...], 1.0) * post
    sin = jnp.where(is_rope, sin_ref[...], 0.0) * post
    parts = []
    for h in range(TN // HEAD_DIM):
        a = acc[:, h * HEAD_DIM:(h + 1) * HEAD_DIM]
        parts.append(a * cos + pltpu.roll(a, HEAD_DIM // 2, axis=1) * sin)
    qkv_ref[...] = jnp.concatenate(parts, axis=1).astype(BF16)

    @pl.when(j >= N_Q_TILES)
    def _():
        for g in range(N_KV):
            kvs_ref[pl.ds(g, M_SAMPLE, stride=N_KV), :] = parts[g][TM - M_SAMPLE:]

    for c in range(6):
        @pl.when(j == N_Q_TILES + c)
        def _(c=c):
            for g in range(N_KV):
                kvp_refs[c][pl.ds(g, TM, stride=N_KV), :] = parts[g]


def _in_proj(x, g, w, wg, cos, sin):
    m = x.shape[0]
    assert m == M_ALL and (m // TM - 1) * TM + (TM - M_SAMPLE) == M_PROMPT
    return pl.pallas_call(
        _in_proj_kernel,
        grid=(m // TM, N_QKV_TILES),
        in_specs=[
            pl.BlockSpec((TM, D_MODEL), lambda i, j: (i, 0)),
            pl.BlockSpec((1, D_MODEL), lambda i, j: (0, 0)),
            pl.BlockSpec((None, D_MODEL, TN), lambda i, j: (0, 0, j)),
            pl.BlockSpec((D_MODEL, HEAD_DIM), lambda i, j: (0, 0)),
            pl.BlockSpec((TM, HEAD_DIM), lambda i, j: (i, 0)),
            pl.BlockSpec((TM, HEAD_DIM), lambda i, j: (i, 0)),
        ],
        out_specs=[
            pl.BlockSpec((TM, TN), lambda i, j: (i, j)),
        ] + [pl.BlockSpec((TM * N_KV, HEAD_DIM), lambda i, j: (i, 0))] * 6 + [
            pl.BlockSpec((None, M_SAMPLE * N_KV, HEAD_DIM), lambda i, j: (jnp.maximum(j - N_Q_TILES, 0), i, 0)),
            pl.BlockSpec((TM, HEAD_DIM), lambda i, j: (i, 0)),
        ],
        out_shape=[
            jax.ShapeDtypeStruct((m, Q_DIM + 6 * KV_DIM), BF16),
        ] + [jax.ShapeDtypeStruct((M_PROMPT * N_KV, HEAD_DIM), F32)] * 6 + [
            jax.ShapeDtypeStruct((6, (m // TM) * M_SAMPLE * N_KV, HEAD_DIM), F32),
            jax.ShapeDtypeStruct((m, HEAD_DIM), F32),
        ],
        scratch_shapes=[pltpu.VMEM((TM, D_MODEL), BF16)],
        compiler_params=_cparams(("parallel", "arbitrary")),
        name="nsa_in_proj",
    )(x, g, w, wg, cos, sin)


def _dual_kernel(*refs, mode, has_bias):
    if has_bias:
        x_ref, g_ref, wa_ref, wb_ref, ba_ref, bb_ref, o_ref, xn_ref = refs
    else:
        x_ref, g_ref, wa_ref, wb_ref, o_ref, xn_ref = refs

    @pl.when(pl.program_id(1) == 0)
    def _():
        xn_ref[...] = _rms_scale(x_ref[...], g_ref[...]).astype(BF16)

    xn = xn_ref[...]
    a = _dot(xn, wa_ref[...])
    b = _dot(xn, wb_ref[...])
    if has_bias:
        a = a + ba_ref[...]
        b = b + bb_ref[...]
    if mode == "swiglu":
        o = jax.nn.silu(a) * b
    else:
        o = a * jax.nn.sigmoid(b)
    o_ref[...] = o.astype(o_ref.dtype)


def _dual_proj(x, g, wa, wb, layer, n_out, b_col0, bias, mode, out_dtype, name):
    m = x.shape[0]
    nb = b_col0 // TN
    in_specs = [
        pl.BlockSpec((TM, D_MODEL), lambda i, j: (i, 0)),
        pl.BlockSpec((1, D_MODEL), lambda i, j: (0, 0)),
        pl.BlockSpec((None, D_MODEL, TN), lambda i, j: (layer, 0, j)),
        pl.BlockSpec((None, D_MODEL, TN), lambda i, j: (layer, 0, j + nb)),
    ]
    args = [x, g, wa, wb]
    if bias is not None:
        in_specs += [pl.BlockSpec((1, TN), lambda i, j: (0, j)), pl.BlockSpec((1, TN), lambda i, j: (0, j + nb))]
        args += [bias, bias]
    return pl.pallas_call(
        functools.partial(_dual_kernel, mode=mode, has_bias=bias is not None),
        grid=(m // TM, n_out // TN),
        in_specs=in_specs,
        out_specs=pl.BlockSpec((TM, TN), lambda i, j: (i, j)),
        out_shape=jax.ShapeDtypeStruct((m, n_out), out_dtype),
        scratch_shapes=[pltpu.VMEM((TM, D_MODEL), BF16)],
        compiler_params=_cparams(("parallel", "arbitrary")),
        name=name,
    )(*args)


def _proj_norm_res_kernel(a_ref, w_ref, g_ref, h_ref, o_ref, acc_ref, *, nk):
    k = pl.program_id(1)
    part = _dot(a_ref[...], w_ref[...])

    @pl.when(k == 0)
    def _():
        acc_ref[...] = part

    @pl.when(k > 0)
    def _():
        acc_ref[...] += part

    @pl.when(k == nk - 1)
    def _():
        o_ref[...] = h_ref[...] + _rms_scale(acc_ref[...], g_ref[...])


def _proj_norm_res(a, w, layer, g, h, nk, name):
    m, kdim = a.shape
    tk = kdim // nk
    return pl.pallas_call(
        functools.partial(_proj_norm_res_kernel, nk=nk),
        grid=(m // TM_ROWS, nk),
        in_specs=[
            pl.BlockSpec((TM_ROWS, tk), lambda i, k: (i, k)),
            pl.BlockSpec((None, tk, D_MODEL), lambda i, k: (layer, k, 0)),
            pl.BlockSpec((1, D_MODEL), lambda i, k: (0, 0)),
            pl.BlockSpec((TM_ROWS, D_MODEL), lambda i, k: (i, 0)),
        ],
        out_specs=pl.BlockSpec((TM_ROWS, D_MODEL), lambda i, k: (i, 0)),
        out_shape=jax.ShapeDtypeStruct((m, D_MODEL), F32),
        scratch_shapes=[pltpu.VMEM((TM_ROWS, D_MODEL), F32)],
        compiler_params=_cparams(("parallel", "arbitrary")),
        name=name,
    )(a, w, g, h)


def _ple_kernel(h_ref, hc_ref, p_ref, wg_ref, wp_ref, o_ref, *rest, split):
    hb_ref = rest[-1]

    @pl.when(pl.program_id(1) == 0)
    def _():
        hb_ref[...] = h_ref[...].astype(BF16)

    gate = jax.nn.sigmoid(_dot(hb_ref[...], wg_ref[...]))
    out = hc_ref[...] + gate * _dot(p_ref[...], wp_ref[...])
    o_ref[...] = out
    if split:
        rest[0][...] = out[TM - M_SAMPLE:]


def _ple(h, p, wg, wp, layer, name, split=False):
    m = h.shape[0]
    out_specs = pl.BlockSpec((TM, TN), lambda i, j: (i, j))
    out_shape = jax.ShapeDtypeStruct((m, D_MODEL), F32)
    if split:
        assert (m // TM - 1) * TM + (TM - M_SAMPLE) == M_PROMPT
        out_specs = [out_specs, pl.BlockSpec((M_SAMPLE, TN), lambda i, j: (i, j))]
        out_shape = [jax.ShapeDtypeStruct((M_PROMPT, D_MODEL), F32),
                     jax.ShapeDtypeStruct(((m // TM) * M_SAMPLE, D_MODEL), F32)]
    return pl.pallas_call(
        functools.partial(_ple_kernel, split=split),
        grid=(m // TM, D_MODEL // TN),
        in_specs=[
            pl.BlockSpec((TM, D_MODEL), lambda i, j: (i, 0)),
            pl.BlockSpec((TM, TN), lambda i, j: (i, j)),
            pl.BlockSpec((None, TM, PLE_DIM), lambda i, j: (layer, i, 0)),
            pl.BlockSpec((None, D_MODEL, TN), lambda i, j: (layer, 0, j)),
            pl.BlockSpec((None, PLE_DIM, TN), lambda i, j: (layer, 0, j)),
        ],
        out_specs=out_specs,
        out_shape=out_shape,
        scratch_shapes=[pltpu.VMEM((TM, D_MODEL), BF16)],
        compiler_params=_cparams(("parallel", "arbitrary")),
        name=name,
    )(h, h, p, wg, wp)


TILES_PER_CHUNK = CMP_STRIDE * N_KV // 8


def _compress_kernel(pt_ref, *refs, n):
    k_refs, v_refs = refs[0:n], refs[n:2 * n]
    wk_ref, wv_ref, pk0_ref, pk1_ref, pv0_ref, pv1_ref = refs[2 * n:2 * n + 6]
    for srcs, w_ref, o0_ref, o1_ref in ((k_refs, wk_ref, pk0_ref, pk1_ref), (v_refs, wv_ref, pv0_ref, pv1_ref)):
        acc = None
        for jj in range(TILES_PER_CHUNK // 2):
            halves = []
            for j in (2 * jj, 2 * jj + 1):
                rows = [srcs[p][(c * TILES_PER_CHUNK + j) * 8:(c * TILES_PER_CHUNK + j + 1) * 8, :]
                        for p in range(n) for c in range(CHUNKS_PER_PAGE)]
                halves.append(jnp.concatenate(rows, axis=0))
            term = _dot(jnp.concatenate(halves, axis=1).astype(BF16), w_ref[jj])
            acc = term if acc is None else acc + term
        width = 2 * HEAD_DIM
        folded = acc[:, :width] + pltpu.roll(acc[:, width:], acc.shape[0] - N_KV, axis=0)
        o0_ref[0] = folded[:, :HEAD_DIM]
        o1_ref[0] = folded[:, HEAD_DIM:]


def _compress_stage1(page_table, k_src, k_type, v_src, v_type, wk, wv, n_seq, pages_per_seq, name):
    n = min(PAGES_PER_STEP, pages_per_seq)
    groups = pages_per_seq // n
    n_chunks = pages_per_seq * CHUNKS_PER_PAGE

    def src_spec(slot, typ):
        return pl.BlockSpec(
            (None, PAGE_ROWS, HEAD_DIM), lambda b, t, pt: (typ, pt[b * pages_per_seq + t * n + slot], 0))

    w_spec = pl.BlockSpec(wk.shape, lambda b, t, pt: (0, 0, 0))
    out_spec = pl.BlockSpec((1, n * CHUNKS_PER_PAGE * 8, HEAD_DIM), lambda b, t, pt: (b, t, 0))
    out_sds = jax.ShapeDtypeStruct((n_seq, n_chunks * 8, HEAD_DIM), F32)
    return pl.pallas_call(
        functools.partial(_compress_kernel, n=n),
        grid_spec=pltpu.PrefetchScalarGridSpec(
            num_scalar_prefetch=1,
            grid=(n_seq, groups),
            in_specs=[src_spec(s, k_type) for s in range(n)] + [src_spec(s, v_type) for s in range(n)]
            + [w_spec, w_spec],
            out_specs=[out_spec] * 4,
        ),
        out_shape=[out_sds] * 4,
        compiler_params=_cparams(("parallel", "arbitrary")),
        name=name,
    )(page_table, *([k_src] * n), *([v_src] * n), wk, wv)


def _compress_finish_kernel(p0_ref, p1_ref, pn_ref, pe_ref, w1_ref, w2_ref, o_ref):
    c = o_ref.shape[2]
    bias = _dot(pe_ref[...].astype(BF16), w1_ref[...])[0:1]
    row = lax.broadcasted_iota(jnp.int32, (c, 1), 0)
    for g in range(N_KV):
        p0 = p0_ref[0, pl.ds(g, c, stride=8), :]
        p1 = p1_ref[0, pl.ds(g, c, stride=8), :]
        nxt = pltpu.roll(p1, c - 1, axis=0)
        nxt = jnp.where(row == c - 1, pn_ref[g:g + 1, :], nxt)
        hid = p0 + nxt + bias
        o_ref[0, g] = _dot(jax.nn.silu(hid).astype(BF16), w2_ref[...]).astype(o_ref.dtype)


def _compress_finish(p0, p1, p1_next, pe8, w1, w2, name):
    n_seq, rows, _ = p0.shape
    c = rows // 8
    seq_spec = pl.BlockSpec((1, rows, HEAD_DIM), lambda b: (b, 0, 0))
    return pl.pallas_call(
        _compress_finish_kernel,
        grid=(n_seq,),
        in_specs=[
            seq_spec, seq_spec,
            pl.BlockSpec((8, HEAD_DIM), lambda b: (b, 0)),
            pl.BlockSpec((8, CMP_BLOCK * HEAD_DIM), lambda b: (0, 0)),
            pl.BlockSpec((CMP_BLOCK * HEAD_DIM, HEAD_DIM), lambda b: (0, 0)),
            pl.BlockSpec((HEAD_DIM, HEAD_DIM), lambda b: (0, 0)),
        ],
        out_specs=pl.BlockSpec((1, N_KV, c, HEAD_DIM), lambda b: (b, 0, 0, 0)),
        out_shape=jax.ShapeDtypeStruct((n_seq, N_KV, c, HEAD_DIM), BF16),
        compiler_params=_cparams(("parallel",)),
        name=name,
    )(p0, p1, p1_next, pe8, w1, w2)


def _masked_softmax2(s, mask):
    s = jnp.where(mask, s, NEG)
    m = jnp.max(s, axis=-1, keepdims=True)
    e = jnp.where(mask, jnp.exp2(s - m), 0.0)
    return e / jnp.maximum(jnp.sum(e, axis=-1, keepdims=True), 1e-30)


def _split3(x):
    parts = []
    rem = x
    for _ in range(3):
        part = rem.astype(BF16)
        parts.append(part)
        rem = rem - part.astype(F32)
    return parts


def _sel_tile(kt, carry, qa, t_row, ks, vs, e_ref, e_row0, tk, masked, lanes=slice(0, HEAD_DIM)):
    m, l, acc = carry
    r0 = pl.multiple_of(kt * tk, tk)
    e0 = pl.multiple_of(kt * tk - e_row0, tk)
    ka = jnp.concatenate([ks[pl.ds(r0, tk), lanes], e_ref[pl.ds(e0, tk), :]], axis=1)
    sc = _dot_nt(qa, ka)
    if masked:
        kpos = r0 + lax.broadcasted_iota(jnp.int32, (1, tk), 1)
        sc = jnp.where(kpos <= t_row, sc, NEG)
    m_new = jnp.maximum(m, jnp.max(sc, axis=-1, keepdims=True))
    alpha = jnp.exp2(m - m_new)
    p = jnp.exp2(sc - m_new)
    l = alpha * l + jnp.sum(p, axis=-1, keepdims=True)
    acc = alpha * acc + _dot(p.astype(BF16), vs[pl.ds(r0, tk), lanes])
    return m_new, l, acc


def _sel_init(rows):
    return (jnp.full((rows, 1), -jnp.inf, F32), jnp.zeros((rows, 1), F32), jnp.zeros((rows, HEAD_DIM), F32))


def _gated_merge(gt, o_c, o_s, o_w, tq):
    outs = []
    for r in range(GQA_R):
        sl = slice(r * tq, (r + 1) * tq)
        outs.append(gt[:, r:r + 1] * o_c[sl] + gt[:, GQA_R + r:GQA_R + r + 1] * o_s[sl]
                    + gt[:, 2 * GQA_R + r:2 * GQA_R + r + 1] * o_w[sl])
    return jnp.concatenate(outs, axis=1)


def _cmp_to_sel_matrix(nb, nsp):
    n = np.arange(nb)[:, None]
    j = np.arange(nsp)[None, :]
    return ((n >= 4 * j) & (n <= 4 * j + 3)).astype(np.float32) + ((n >= 4 * j - 1) & (n <= 4 * j + 2)).astype(np.float32)


def _block_onehot(n_keys):
    k = np.arange(n_keys)[:, None] // SEL_BLOCK
    return jnp.asarray((k % 128 == np.arange(128)[None, :]).astype(np.float32), BF16)


def _window_bias_prompt():
    q = np.arange(TQ_PROMPT)[:, None]
    kb = np.arange(WBAND_PROMPT)[None, :]
    pats = [kb <= TQ_PROMPT * i + q for i in range(WINDOW // TQ_PROMPT)]
    pats.append((kb > q) & (kb <= q + WINDOW))
    return jnp.asarray(np.where(np.stack(pats), 0.0, NEG), F32)


def _attn_prompt_kernel(q_ref, gate_ref, kc_ref, vc_ref, ks_ref, vs_ref, kw_ref, vw_ref, mmt_ref, e_ref, wb_ref,
                        o_init_ref, o_ref, *, nb, ns):
    del o_init_ref
    tq = TQ_PROMPT
    rows = GQA_R * tq
    qi = pl.program_id(2)
    t0 = qi * tq
    t_row = t0 + (lax.broadcasted_iota(jnp.int32, (rows, 1), 0) & (tq - 1))
    cmp_end = lax.broadcasted_iota(jnp.int32, (1, nb), 1) * CMP_STRIDE + (CMP_BLOCK - 1)
    jb = lax.broadcasted_iota(jnp.int32, (ns, tq), 0)
    cur = lax.shift_right_logical(t0 + lax.broadcasted_iota(jnp.int32, (1, tq), 1), 6)
    forced = (jb == 0) | (jb == cur) | (jb == cur - 1)
    eligible = jb <= cur
    groups = ns // 8
    jgrp = [jb[8 * v:8 * v + 8] for v in range(groups)]
    eye = jnp.where(lax.broadcasted_iota(jnp.int32, (tq, tq), 0) == lax.broadcasted_iota(jnp.int32, (tq, tq), 1),
                    1.0, 0.0).astype(BF16)
    mmt = mmt_ref[...]

    heads = range(ATT_HEADS_PER_STEP)
    qs, o_cs, qas = [], [], []
    for h in heads:
        qq = q_ref[:, h * GQA_R * HEAD_DIM:(h + 1) * GQA_R * HEAD_DIM]
        q = jnp.concatenate([qq[:, r * HEAD_DIM:(r + 1) * HEAD_DIM] for r in range(GQA_R)], axis=0)

        s = _dot_nt(q, kc_ref[0, h])
        p_c = _masked_softmax2(s, cmp_end <= t_row)
        o_cs.append(_dot(p_c.astype(BF16), vc_ref[0, h]))
        imp = p_c[0:tq]
        for r in range(1, GQA_R):
            imp = imp + p_c[r * tq:(r + 1) * tq]

        blk_imp = None
        for part in _split3(imp):
            term = _dot_nt(mmt, part)
            blk_imp = term if blk_imp is None else blk_imp + term
        score = jnp.where(forced, jnp.inf, jnp.where(eligible, blk_imp, -jnp.inf))
        sgrp = [score[8 * v:8 * v + 8] for v in range(groups)]
        rank = [jnp.zeros((8, tq), F32) for _ in range(groups)]
        for i in range(ns):
            ci = score[i:i + 1]
            for v in range(groups):
                if 8 * v > i:
                    beats = ci >= sgrp[v]
                elif 8 * v + 7 <= i:
                    beats = ci > sgrp[v]
                else:
                    beats = (ci > sgrp[v]) | ((ci == sgrp[v]) & (jgrp[v] > i))
                rank[v] = rank[v] + jnp.where(beats, 1.0, 0.0)
        rank = jnp.concatenate(rank, axis=0)
        sel_t = jnp.where((rank < float(TOPK)) & eligible, 1.0, 0.0)
        sel_t = jnp.concatenate([sel_t, jnp.zeros((128 - ns, tq), F32)], axis=0).astype(BF16)
        sel = _dot_nt(eye, sel_t)
        negsel = jnp.where(sel > 0.5, 0.0, NEG).astype(BF16)
        qs.append(q)
        qas.append(jnp.concatenate([q, jnp.concatenate([negsel] * GQA_R, axis=0)], axis=1))

    n_tiles = lax.shift_right_logical(t0 + tq - 1, TK_ATT.bit_length() - 1) + 1

    def tiles(kt, carries, masked):
        return tuple(
            _sel_tile(kt, carries[h], qa=qas[h], t_row=t_row, ks=ks_ref, vs=vs_ref, e_ref=e_ref, e_row0=0,
                      tk=TK_ATT, masked=masked, lanes=slice(h * HEAD_DIM, (h + 1) * HEAD_DIM))
            for h in heads)

    carries = lax.fori_loop(0, n_tiles - 1, functools.partial(tiles, masked=False),
                            tuple(_sel_init(rows) for _ in heads))
    carries = tiles(n_tiles - 1, carries, masked=True)

    wstart = pl.multiple_of(jnp.maximum(t0 - WINDOW, 0), tq)
    bias = wb_ref[jnp.minimum(qi, WINDOW // tq)]
    bias = jnp.concatenate([bias] * GQA_R, axis=0)
    for h in heads:
        lanes = slice(h * HEAD_DIM, (h + 1) * HEAD_DIM)
        _, l, acc = carries[h]
        o_s = acc / jnp.maximum(l, 1e-30)
        s = _dot_nt(qs[h], kw_ref[pl.ds(wstart, WBAND_PROMPT), lanes]) + bias
        e = jnp.exp2(s - jnp.max(s, axis=-1, keepdims=True))
        p_w = e / jnp.sum(e, axis=-1, keepdims=True)
        o_w = _dot(p_w.astype(BF16), vw_ref[pl.ds(wstart, WBAND_PROMPT), lanes])
        o_ref[:, h * GQA_R * HEAD_DIM:(h + 1) * GQA_R * HEAD_DIM] = _gated_merge(
            gate_ref[h], o_cs[h], o_s, o_w, tq).astype(o_ref.dtype)


def _attn_prompt(qkv, gates_t, kc, vc, o_init):
    tq = TQ_PROMPT
    hp = ATT_HEADS_PER_STEP
    nq = SEQ // tq
    nb = SEQ // CMP_STRIDE
    ns = SEQ // SEL_BLOCK
    k_sel0, v_sel0 = Q_DIM + 2 * KV_DIM, Q_DIM + 3 * KV_DIM
    k_win0, v_win0 = Q_DIM + 4 * KV_DIM, Q_DIM + 5 * KV_DIM
    seq_spec = lambda base: pl.BlockSpec((SEQ, hp * HEAD_DIM), lambda b, g, i: (b, base // (hp * HEAD_DIM) + g))
    blk_spec = pl.BlockSpec((1, hp, nb, HEAD_DIM), lambda b, g, i: (b, g, 0, 0))
    mmt = jnp.asarray(_cmp_to_sel_matrix(nb, ns).T, BF16)
    wbias = _window_bias_prompt()
    return pl.pallas_call(
        functools.partial(_attn_prompt_kernel, nb=nb, ns=ns),
        grid=(BATCH, N_KV // hp, nq),
        in_specs=[
            pl.BlockSpec((tq, hp * GQA_R * HEAD_DIM), lambda b, g, i: (b * nq + i, g)),
            pl.BlockSpec((hp, tq, HEAD_DIM), lambda b, g, i: (g, b * nq + i, 0)),
            blk_spec, blk_spec,
            seq_spec(k_sel0), seq_spec(v_sel0), seq_spec(k_win0), seq_spec(v_win0),
            pl.BlockSpec((ns, nb), lambda b, g, i: (0, 0)),
            pl.BlockSpec((SEQ, 128), lambda b, g, i: (0, 0)),
            pl.BlockSpec(wbias.shape, lambda b, g, i: (0, 0, 0)),
            pl.BlockSpec(memory_space=pl.ANY),
        ],
        out_specs=pl.BlockSpec((tq, hp * GQA_R * HEAD_DIM), lambda b, g, i: (b * nq + i, g)),
        out_shape=jax.ShapeDtypeStruct((M_ALL, Q_DIM), BF16),
        input_output_aliases={11: 0},
        compiler_params=_cparams(("parallel", "parallel", "arbitrary")),
        name="nsa_attn_prompt",
    )(qkv, gates_t, kc, vc, qkv, qkv, qkv, qkv, mmt, _block_onehot(SEQ), wbias, o_init)


SAMPLE_NB = PAST_LEN // CMP_STRIDE
SAMPLE_NS = -(-(PAST_LEN + DEC_SEQ) // SEL_BLOCK)
SAMPLE_NSP = 384
SAMPLE_PAGES_PER_STEP = 16
KEYS_PER_STEP = SAMPLE_PAGES_PER_STEP * PAGE_SIZE
BLOCKS_PER_STEP = KEYS_PER_STEP // SEL_BLOCK
SAMPLE_STEPS = N_PAGES // SAMPLE_PAGES_PER_STEP


def _attn_sample_kernel(pt_ref, q_ref, gate_ref, kc_ref, vc_ref, kw_ref, vw_ref, mmat_ref, e_ref, *refs):
    n = SAMPLE_PAGES_PER_STEP
    k_refs, v_refs = refs[0:n], refs[n:2 * n]
    kn_ref, vn_ref, o_ref, qa_ref, m_ref, l_ref, acc_ref, oc_ref, ow_ref = refs[2 * n:]
    tq, nb, ns, nsp = DEC_SEQ, SAMPLE_NB, SAMPLE_NS, SAMPLE_NSP
    rows = GQA_R * tq
    t0 = PAST_LEN
    t = pl.program_id(1)
    t_row = t0 + (lax.broadcasted_iota(jnp.int32, (rows, 1), 0) & (tq - 1))

    @pl.when(t == 0)
    def _():
        cmp_end = lax.broadcasted_iota(jnp.int32, (1, nb), 1) * CMP_STRIDE + (CMP_BLOCK - 1)
        jb = lax.broadcasted_iota(jnp.int32, (tq, nsp), 1)
        cur = lax.shift_right_logical(t0 + lax.broadcasted_iota(jnp.int32, (tq, 1), 0), 6)
        forced = (jb == 0) | (jb == cur) | (jb == cur - 1)
        eligible = jb <= cur
        kpos = (PAST_LEN - WINDOW) + lax.broadcasted_iota(jnp.int32, (1, WBAND_SAMPLE), 1)
        win_mask = (kpos <= t_row) & (kpos > t_row - WINDOW)
        mmat = mmat_ref[...]
        for g in range(N_KV):
            q = q_ref[0, g]

            p_c = _masked_softmax2(_dot_nt(q, kc_ref[0, g]), cmp_end <= t_row)
            oc_ref[g] = _dot(p_c.astype(BF16), vc_ref[0, g])
            imp = p_c[0:tq]
            for r in range(1, GQA_R):
                imp = imp + p_c[r * tq:(r + 1) * tq]

            blk_imp = None
            for part in _split3(imp):
                term = _dot(part, mmat)
                blk_imp = term if blk_imp is None else blk_imp + term
            score = jnp.where(forced, jnp.inf, jnp.where(eligible, blk_imp, -jnp.inf))
            rank = jnp.zeros((tq, nsp), F32)
            for i in range(ns):
                ci = score[:, i:i + 1]
                beats = (ci > score) | ((ci == score) & (jb > i))
                rank = rank + jnp.where(beats, 1.0, 0.0)
            negsel = jnp.where((rank < float(TOPK)) & eligible, 0.0, NEG)
            negsel = jnp.concatenate([negsel] * GQA_R, axis=0).astype(BF16)
            pad = jnp.zeros((rows, HEAD_DIM - BLOCKS_PER_STEP), BF16)
            for st in range(SAMPLE_STEPS + 1):
                piece = negsel[:, st * BLOCKS_PER_STEP:(st + 1) * BLOCKS_PER_STEP]
                qa_ref[g, st] = jnp.concatenate([q, piece, pad], axis=1)

            p_w = _masked_softmax2(_dot_nt(q, kw_ref[0, g]), win_mask)
            ow_ref[g] = _dot(p_w.astype(BF16), vw_ref[0, g])

            m_ref[g] = jnp.full((rows, 1), -jnp.inf, F32)
            l_ref[g] = jnp.zeros((rows, 1), F32)
            acc_ref[g] = jnp.zeros((rows, HEAD_DIM), F32)

    def update(g, qa, ka, v, mask):
        sc = _dot_nt(qa, ka)
        if mask is not None:
            sc = jnp.where(mask, sc, NEG)
        m = m_ref[g]
        m_new = jnp.maximum(m, jnp.max(sc, axis=-1, keepdims=True))
        alpha = jnp.exp2(m - m_new)
        p = jnp.exp2(sc - m_new)
        l_ref[g] = alpha * l_ref[g] + jnp.sum(p, axis=-1, keepdims=True)
        acc_ref[g] = alpha * acc_ref[g] + _dot(p.astype(BF16), v)
        m_ref[g] = m_new

    def head_rows(page_refs, g):
        return jnp.concatenate([r[pl.ds(g, PAGE_SIZE, stride=N_KV), :] for r in page_refs], axis=0).astype(BF16)

    e_blk = e_ref[...]
    for g in range(N_KV):
        ka = jnp.concatenate([head_rows(k_refs, g), e_blk], axis=1)
        update(g, qa_ref[g, t], ka, head_rows(v_refs, g), None)

    @pl.when(t == SAMPLE_STEPS - 1)
    def _():
        kpos = t0 + lax.broadcasted_iota(jnp.int32, (1, PAGE_SIZE), 1)
        gt = gate_ref[...]
        for g in range(N_KV):
            ka = jnp.concatenate([head_rows([kn_ref], g), e_blk[0:PAGE_SIZE]], axis=1)
            update(g, qa_ref[g, SAMPLE_STEPS], ka, head_rows([vn_ref], g), kpos <= t_row)
            o_s = acc_ref[g] / jnp.maximum(l_ref[g], 1e-30)
            o_ref[:, g * GQA_R * HEAD_DIM:(g + 1) * GQA_R * HEAD_DIM] = _gated_merge(
                gt[g], oc_ref[g], o_s, ow_ref[g], tq)


def _attn_sample(page_table, q_s, gates_s, kc, vc, kw, vw, k_pool, v_pool, k_new, v_new):
    n = SAMPLE_PAGES_PER_STEP
    rows = GQA_R * DEC_SEQ
    seq_spec = lambda r: pl.BlockSpec((1, N_KV, r, HEAD_DIM), lambda b, t, pt: (b, 0, 0, 0))
    page = (PAGE_ROWS, HEAD_DIM)
    pool_spec = lambda slot: pl.BlockSpec(page, lambda b, t, pt: (pt[b * N_PAGES + t * n + slot], 0))
    new_spec = pl.BlockSpec(page, lambda b, t, pt: (b, 0))
    mmat = jnp.asarray(_cmp_to_sel_matrix(SAMPLE_NB, SAMPLE_NSP), BF16)
    return pl.pallas_call(
        _attn_sample_kernel,
        grid_spec=pltpu.PrefetchScalarGridSpec(
            num_scalar_prefetch=1,
            grid=(DEC_BATCH, SAMPLE_STEPS),
            in_specs=[
                seq_spec(rows),
                pl.BlockSpec((N_KV, DEC_SEQ, HEAD_DIM), lambda b, t, pt: (0, b, 0)),
                seq_spec(SAMPLE_NB), seq_spec(SAMPLE_NB),
                seq_spec(WBAND_SAMPLE), seq_spec(WBAND_SAMPLE),
                pl.BlockSpec((SAMPLE_NB, SAMPLE_NSP), lambda b, t, pt: (0, 0)),
                pl.BlockSpec((KEYS_PER_STEP, HEAD_DIM), lambda b, t, pt: (0, 0)),
            ] + [pool_spec(s) for s in range(n)] * 2 + [new_spec, new_spec],
            out_specs=pl.BlockSpec((DEC_SEQ, Q_DIM), lambda b, t, pt: (b, 0)),
            scratch_shapes=[
                pltpu.VMEM((N_KV, SAMPLE_STEPS + 1, rows, 2 * HEAD_DIM), BF16),
                pltpu.VMEM((N_KV, rows, 1), F32),
                pltpu.VMEM((N_KV, rows, 1), F32),
                pltpu.VMEM((N_KV, rows, HEAD_DIM), F32),
                pltpu.VMEM((N_KV, rows, HEAD_DIM), F32),
                pltpu.VMEM((N_KV, rows, HEAD_DIM), F32),
            ],
        ),
        out_shape=jax.ShapeDtypeStruct((M_SAMPLE, Q_DIM), F32),
        compiler_params=_cparams(("parallel", "arbitrary")),
        name="nsa_attn_sample",
    )(page_table, q_s, gates_s, kc, vc, kw, vw, mmat, _block_onehot(KEYS_PER_STEP),
      *([k_pool] * n), *([v_pool] * n), k_new, v_new)


HALO = 32


def _conv_kernel(u_ref, halo_ref, wdw_ref, bdw_ref, lng_ref, lnb_ref, w2_ref, b2_ref, g_ref, h_ref, *rest,
                 t, tiles_per_seq, halo_is_state):
    o_ref, buf_ref, y_ref = rest[-3:]
    i = pl.program_id(0)
    if halo_is_state:
        buf_ref[0:HALO, :] = halo_ref[...]
    else:
        first = (i % tiles_per_seq) == 0
        buf_ref[0:HALO, :] = jnp.where(first, 0.0, halo_ref[...])
    buf_ref[HALO:HALO + t, :] = u_ref[...]
    off = HALO - (CONV_W - 1)
    for c in range(D_MODEL // HEAD_DIM):
        cs = slice(c * HEAD_DIM, (c + 1) * HEAD_DIM)
        base = buf_ref[:, cs]
        w = wdw_ref[:, cs]
        y = bdw_ref[:, cs]
        n_rows = HALO + t
        for s in range(8):
            taps = [k for k in range(CONV_W) if (off + k) % 8 == s]
            shifted = base if s == 0 else pltpu.roll(base, n_rows - s, axis=0)
            for k in taps:
                a = off + k - s
                y = y + shifted[a:a + t] * w[k:k + 1]
        y_ref[:, cs] = y
    y = y_ref[...]
    mu = jnp.mean(y, axis=-1, keepdims=True)
    yc = y - mu
    var = jnp.mean(yc * yc, axis=-1, keepdims=True)
    z = yc * lax.rsqrt(var + LN_EPS) * lng_ref[...] + lnb_ref[...]
    mval = _dot(jax.nn.silu(z).astype(BF16), w2_ref[...]) + b2_ref[...]
    o_ref[...] = h_ref[...] + _rms_scale(mval, g_ref[...])


def _conv_tail(u, halo_src, wdw, bdw, lng, lnb, w2, b2, g, h, *, t, n_tiles, row0, tiles_per_seq, halo_is_state,
               out_init=None):
    blk0 = row0 // t
    extra_specs, extra_args, aliases = [], [], {}
    out_rows = n_tiles * t
    if out_init is not None:
        extra_specs, extra_args, aliases = [pl.BlockSpec(memory_space=pl.ANY)], [out_init], {10: 0}
        out_rows = out_init.shape[0]
    if halo_is_state:
        halo_spec = pl.BlockSpec((HALO, D_MODEL), lambda i: (i, 0))
    else:
        per = t // HALO
        halo_spec = pl.BlockSpec((HALO, D_MODEL), lambda i: (jnp.maximum((blk0 + i) * per - 1, 0), 0))
    vec = pl.BlockSpec((1, D_MODEL), lambda i: (0, 0))
    kern = functools.partial(_conv_kernel, t=t, tiles_per_seq=tiles_per_seq, halo_is_state=halo_is_state)
    return pl.pallas_call(
        kern,
        grid=(n_tiles,),
        in_specs=[
            pl.BlockSpec((t, D_MODEL), lambda i: (blk0 + i, 0)),
            halo_spec,
            pl.BlockSpec((HALO, D_MODEL), lambda i: (0, 0)),
            vec, vec, vec,
            pl.BlockSpec((None, D_MODEL, D_MODEL), lambda i: (0, 0, 0)),
            vec, vec,
            pl.BlockSpec((t, D_MODEL), lambda i: (blk0 + i, 0)),
        ] + extra_specs,
        out_specs=pl.BlockSpec((t, D_MODEL), lambda i: (i, 0)),
        out_shape=jax.ShapeDtypeStruct((out_rows, D_MODEL), F32),
        input_output_aliases=aliases,
        scratch_shapes=[pltpu.VMEM((HALO + t, D_MODEL), F32), pltpu.VMEM((t, D_MODEL), F32)],
        compiler_params=_cparams(("parallel",)),
        name="conv_tail_state" if halo_is_state else "conv_tail",
    )(u, halo_src, wdw, bdw, lng, lnb, w2, b2, g, h, *extra_args)


def _row(v):
    return v.reshape(1, -1).astype(F32)


def _rope_tables():
    half = HEAD_DIM // 2
    pos = jnp.concatenate([jnp.tile(jnp.arange(SEQ), BATCH), jnp.tile(PAST_LEN + jnp.arange(DEC_SEQ), DEC_BATCH)])
    inv = ROPE_THETA ** (-jnp.arange(half, dtype=F32) / half)
    ang = pos.astype(F32)[:, None] * inv[None, :]
    cos, sin = jnp.cos(ang), jnp.sin(ang)
    return jnp.concatenate([cos, cos], axis=1), jnp.concatenate([-sin, sin], axis=1)


def _cmp_weights(w1):
    w1r = w1.reshape(CMP_BLOCK // CMP_STRIDE, CMP_STRIDE * HEAD_DIM, HEAD_DIM)
    per_pos = jnp.concatenate([w1r[0], w1r[1]], axis=1).reshape(CMP_STRIDE, HEAD_DIM, 2 * HEAD_DIM)
    per_tile = jnp.concatenate([per_pos[0::2], per_pos[1::2]], axis=2)
    return per_tile.reshape(TILES_PER_CHUNK // 2, 2 * HEAD_DIM, 4 * HEAD_DIM).astype(BF16)


def kernel(x_prompt, x_sample, p_prompt, p_sample, page_table, cache_k_cmp, cache_v_cmp, cache_k_sel, cache_v_sel, cache_k_win, cache_v_win, state_conv, norm_mix_pre, norm_mix_post, norm_ffn_pre, norm_ffn_post, ffn_w_gate, ffn_w_up, ffn_w_down, ple_w_proj, ple_w_gate, nsa_w_in, nsa_w_out, nsa_cmp_pe_k, nsa_cmp_w1_k, nsa_cmp_w2_k, nsa_cmp_pe_v, nsa_cmp_w1_v, nsa_cmp_w2_v, conv_w_pw1, conv_b_pw1, conv_w_dw, conv_b_dw, conv_ln_g, conv_ln_b, conv_w_pw2, conv_b_pw2):
    h = jnp.concatenate([x_prompt.reshape(M_PROMPT, D_MODEL), x_sample.reshape(M_SAMPLE, D_MODEL)], axis=0)
    p_all = jnp.concatenate([p_prompt.reshape(DEPTH, M_PROMPT, PLE_DIM),
                             p_sample.reshape(DEPTH, M_SAMPLE, PLE_DIM)], axis=1).astype(BF16)
    pt_flat = page_table.reshape(-1)
    w_gate, w_up, w_down = ffn_w_gate.astype(BF16), ffn_w_up.astype(BF16), ffn_w_down.astype(BF16)
    w_ple_gate, w_ple = ple_w_gate.astype(BF16), ple_w_proj.astype(BF16)

    def residual_tail(h, layer, split=False):
        act = _dual_proj(h, _row(norm_ffn_pre[layer]), w_gate, w_up, layer, D_FF, 0, None, "swiglu", BF16,
                         f"ffn_up_{layer}")
        h = _proj_norm_res(act, w_down, layer, _row(norm_ffn_post[layer]), h, 4, f"ffn_down_{layer}")
        return _ple(h, p_all, w_ple_gate, w_ple, layer, f"ple_{layer}", split=split)

    n_main = Q_DIM + 6 * KV_DIM
    w_gates = jnp.pad(nsa_w_in[0][:, n_main:], ((0, 0), (0, HEAD_DIM - 3 * N_HEADS))).astype(BF16)
    cos, sin = _rope_tables()
    qkv, *kvp, kvs, gates = _in_proj(h, _row(norm_mix_pre[0]), nsa_w_in.astype(BF16), w_gates, cos, sin)

    gates_t = gates[:, :3 * N_HEADS].reshape(M_ALL, 3, N_KV, GQA_R).transpose(2, 0, 1, 3).reshape(N_KV, M_ALL, 3 * GQA_R)
    gates_t = jnp.pad(gates_t, ((0, 0), (0, 0), (0, HEAD_DIM - 3 * GQA_R)))

    kv_s = kvs[:, -M_SAMPLE * N_KV:].reshape(6, DEC_BATCH, DEC_SEQ, N_KV, HEAD_DIM)
    new_page = lambda c: jnp.pad(kv_s[c], ((0, 0), (0, PAGE_SIZE - DEC_SEQ), (0, 0), (0, 0))).reshape(-1, HEAD_DIM)

    wk, wv = _cmp_weights(nsa_cmp_w1_k[0]), _cmp_weights(nsa_cmp_w1_v[0])
    pe_k = jnp.broadcast_to(nsa_cmp_pe_k[0].reshape(1, -1), (8, CMP_BLOCK * HEAD_DIM))
    pe_v = jnp.broadcast_to(nsa_cmp_pe_v[0].reshape(1, -1), (8, CMP_BLOCK * HEAD_DIM))
    w1k, w1v = nsa_cmp_w1_k[0].astype(BF16), nsa_cmp_w1_v[0].astype(BF16)
    w2k, w2v = nsa_cmp_w2_k[0].astype(BF16), nsa_cmp_w2_v[0].astype(BF16)

    pages_prompt = SEQ // PAGE_SIZE
    ident = jnp.arange(BATCH * pages_prompt, dtype=jnp.int32)
    pk0, pk1, pv0, pv1 = _compress_stage1(ident, kvp[0][None], 0, kvp[1][None], 0, wk, wv, BATCH, pages_prompt,
                                          "nsa_cmp_prompt")
    zero_next = jnp.zeros((BATCH * 8, HEAD_DIM), F32)
    kc_p = _compress_finish(pk0, pk1, zero_next, pe_k, w1k, w2k, "nsa_cmp_fin_k_prompt")
    vc_p = _compress_finish(pv0, pv1, zero_next, pe_v, w1v, w2v, "nsa_cmp_fin_v_prompt")

    pool3 = lambda c: c.reshape(1, -1, HEAD_DIM)
    pk0, pk1, pv0, pv1 = _compress_stage1(pt_flat, pool3(cache_k_cmp), 0, pool3(cache_v_cmp), 0, wk, wv, DEC_BATCH,
                                          N_PAGES, "nsa_cmp_sample")
    _, pk1_n, _, pv1_n = _compress_stage1(jnp.arange(DEC_BATCH, dtype=jnp.int32), new_page(0)[None], 0,
                                          new_page(1)[None], 0, wk, wv, 1, DEC_BATCH, "nsa_cmp_sample_new")
    first_chunk = lambda pn: pn[0].reshape(DEC_BATCH, CHUNKS_PER_PAGE * 8, HEAD_DIM)[:, 0:8].reshape(-1, HEAD_DIM)
    kc_s = _compress_finish(pk0, pk1, first_chunk(pk1_n), pe_k, w1k, w2k, "nsa_cmp_fin_k_sample")
    vc_s = _compress_finish(pv0, pv1, first_chunk(pv1_n), pe_v, w1v, w2v, "nsa_cmp_fin_v_sample")

    pool2 = lambda c: c.reshape(-1, HEAD_DIM)
    kw_all = jnp.concatenate([cache_k_win[0], kv_s[4]], axis=1)
    vw_all = jnp.concatenate([cache_v_win[0], kv_s[5]], axis=1)
    band = lambda w: jnp.pad(w.transpose(0, 2, 1, 3),
                             ((0, 0), (0, 0), (0, WBAND_SAMPLE - w.shape[1]), (0, 0))).astype(BF16)
    q_s = qkv[M_PROMPT:, :Q_DIM].reshape(DEC_BATCH, DEC_SEQ, N_KV, GQA_R, HEAD_DIM).transpose(0, 2, 3, 1, 4)
    q_s = q_s.reshape(DEC_BATCH, N_KV, GQA_R * DEC_SEQ, HEAD_DIM)
    o_s = _attn_sample(pt_flat, q_s, gates_t[:, M_PROMPT:], kc_s, vc_s, band(kw_all), band(vw_all),
                       pool2(cache_k_sel), pool2(cache_v_sel), new_page(2), new_page(3))
    o_all = _attn_prompt(qkv, gates_t, kc_p, vc_p, jnp.pad(o_s.astype(BF16), ((M_PROMPT, 0), (0, 0))))

    h = _proj_norm_res(o_all, nsa_w_out.astype(BF16), 0, _row(norm_mix_post[0]), h, 2, "nsa_out_proj")
    h = residual_tail(h, 0)

    w_pw1 = conv_w_pw1.astype(BF16)
    u = _dual_proj(h, _row(norm_mix_pre[1]), w_pw1, w_pw1, 0, D_MODEL, D_MODEL, _row(conv_b_pw1[0]), "glu", F32,
                   "conv_pw1_glu")
    wdw = jnp.pad(conv_w_dw[0], ((0, HALO - CONV_W), (0, 0)))
    conv_args = (wdw, _row(conv_b_dw[0]), _row(conv_ln_g[0]), _row(conv_ln_b[0]), conv_w_pw2.astype(BF16),
                 _row(conv_b_pw2[0]), _row(norm_mix_post[1]))
    t_p = 256
    state = jnp.pad(state_conv[0], ((0, 0), (HALO - (CONV_W - 1), 0), (0, 0))).reshape(DEC_BATCH * HALO, D_MODEL)
    h_s = _conv_tail(u, state, *conv_args, h, t=DEC_SEQ, n_tiles=DEC_BATCH, row0=M_PROMPT, tiles_per_seq=1,
                     halo_is_state=True)
    h = _conv_tail(u, u, *conv_args, h, t=t_p, n_tiles=M_PROMPT // t_p, row0=0, tiles_per_seq=SEQ // t_p,
                   halo_is_state=False, out_init=jnp.pad(h_s, ((M_PROMPT, 0), (0, 0))))
    y_p, y_s = residual_tail(h, 1, split=True)

    y_prompt = y_p.reshape(BATCH, SEQ, D_MODEL)
    y_sample = y_s[-M_SAMPLE:].reshape(DEC_BATCH, DEC_SEQ, D_MODEL)
    kv_p = [a.reshape(1, BATCH, SEQ, N_KV, HEAD_DIM) for a in kvp]
    outs = [y_prompt, y_sample]
    for c in range(4):
        outs.append(kv_p[c])
        outs.append(kv_s[c][None])
    w_keep = min(WINDOW, SEQ)
    outs.append(kv_p[4][:, :, SEQ - w_keep:])
    outs.append(kw_all[None, :, -WINDOW:])
    outs.append(kv_p[5][:, :, SEQ - w_keep:])
    outs.append(vw_all[None, :, -WINDOW:])
    n_keep = CONV_W - 1
    outs.append(jnp.stack([u[b * SEQ + SEQ - n_keep:(b + 1) * SEQ] for b in range(BATCH)])[None])
    u_s = u[M_PROMPT:].reshape(DEC_BATCH, DEC_SEQ, D_MODEL)
    outs.append(jnp.concatenate([state_conv[0], u_s], axis=1)[None, :, -n_keep:])
    return tuple(outs)
```

```python
import functools
import math

import numpy as np
import jax
import jax.numpy as jnp
from jax import lax
from jax.experimental import pallas as pl
from jax.experimental.pallas import tpu as pltpu

F32 = jnp.float32
BF16 = jnp.bfloat16

D_MODEL = 2048
BATCH = 2
SEQ = 4096
DEPTH = 2
DEC_BATCH = 8
DEC_SEQ = 8
PAST_LEN = 16384
PAGE_SIZE = 128
N_HEADS = 16
HEAD_DIM = 128
N_KV = 4
GQA_R = 4
Q_DIM = 2048
KV_DIM = 512
CMP_BLOCK = 32
CMP_STRIDE = 16
SEL_BLOCK = 64
TOPK = 16
WINDOW = 512
ROPE_THETA = 10000.0
CONV_W = 31
D_FF = 5632
PLE_DIM = 256
RMS_EPS = 1e-6
LN_EPS = 1e-5
NEG = -1e30

M_PROMPT = BATCH * SEQ
M_SAMPLE = DEC_BATCH * DEC_SEQ
M_ALL = M_PROMPT + M_SAMPLE
N_PAGES = PAST_LEN // PAGE_SIZE
PAGE_ROWS = PAGE_SIZE * N_KV
CHUNKS_PER_PAGE = PAGE_SIZE // CMP_STRIDE
PAGES_PER_STEP = 16

TM = 688
TM_ROWS = 688
TN = 512
TK_ATT = 512
TQ_PROMPT = 128
ATT_HEADS_PER_STEP = 4
WBAND_PROMPT = WINDOW + TQ_PROMPT
WBAND_SAMPLE = WINDOW + SEL_BLOCK
VMEM_LIMIT = 56 * 1024 * 1024

Q_PRESCALE = HEAD_DIM ** -0.5 * math.log2(math.e)


def _cparams(sem):
    return pltpu.CompilerParams(dimension_semantics=sem, vmem_limit_bytes=VMEM_LIMIT)


def _rms_scale(x, g):
    ms = jnp.mean(x * x, axis=-1, keepdims=True)
    return x * lax.rsqrt(ms + RMS_EPS) * g


def _dot(a, b):
    return jnp.dot(a, b, preferred_element_type=F32)


def _dot_nt(a, b):
    return lax.dot_general(a, b, (((1,), (1,)), ((), ())), preferred_element_type=F32)


N_QKV_TILES = (Q_DIM + 6 * KV_DIM) // TN
N_Q_TILES = Q_DIM // TN


def _in_proj_kernel(x_ref, g_ref, w_ref, wg_ref, cos_ref, sin_ref, qkv_ref, *rest):
    kvp_refs, (kvs_ref, gate_ref, xn_ref) = rest[:6], rest[6:]
    j = pl.program_id(1)

    @pl.when(j == 0)
    def _():
        xn = _rms_scale(x_ref[...], g_ref[...]).astype(BF16)
        xn_ref[...] = xn
        gate_ref[...] = jax.nn.sigmoid(_dot(xn, wg_ref[...]))

    acc = _dot(xn_ref[...], w_ref[...])
    is_rope = (j < N_Q_TILES + 1) | (j == N_Q_TILES + 2) | (j == N_Q_TILES + 4)
    post = jnp.where(j < N_Q_TILES, Q_PRESCALE, 1.0)
    cos = jnp.where(is_rope, cos_ref[...], 1.0) * post
    sin = jnp.where(is_rope, sin_ref[...], 0.0) * post
    parts = []
    for h in range(TN // HEAD_DIM):
        a = acc[:, h * HEAD_DIM:(h + 1) * HEAD_DIM]
        parts.append(a * cos + pltpu.roll(a, HEAD_DIM // 2, axis=1) * sin)
    qkv_ref[...] = jnp.concatenate(parts, axis=1).astype(BF16)

    @pl.when(j >= N_Q_TILES)
    def _():
        for g in range(N_KV):
            kvs_ref[pl.ds(g, M_SAMPLE, stride=N_KV), :] = parts[g][TM - M_SAMPLE:]

    for c in range(6):
        @pl.when(j == N_Q_TILES + c)
        def _(c=c):
            for g in range(N_KV):
                kvp_refs[c][pl.ds(g, TM, stride=N_KV), :] = parts[g]


def _in_proj(x, g, w, wg, cos, sin):
    m = x.shape[0]
    assert m == M_ALL and (m // TM - 1) * TM + (TM - M_SAMPLE) == M_PROMPT
    return pl.pallas_call(
        _in_proj_kernel,
        grid=(m // TM, N_QKV_TILES),
        in_specs=[
            pl.BlockSpec((TM, D_MODEL), lambda i, j: (i, 0)),
            pl.BlockSpec((1, D_MODEL), lambda i, j: (0, 0)),
            pl.BlockSpec((None, D_MODEL, TN), lambda i, j: (0, 0, j)),
            pl.BlockSpec((D_MODEL, HEAD_DIM), lambda i, j: (0, 0)),
            pl.BlockSpec((TM, HEAD_DIM), lambda i, j: (i, 0)),
            pl.BlockSpec((TM, HEAD_DIM), lambda i, j: (i, 0)),
        ],
        out_specs=[
            pl.BlockSpec((TM, TN), lambda i, j: (i, j)),
        ] + [pl.BlockSpec((TM * N_KV, HEAD_DIM), lambda i, j: (i, 0))] * 6 + [
            pl.BlockSpec((None, M_SAMPLE * N_KV, HEAD_DIM), lambda i, j: (jnp.maximum(j - N_Q_TILES, 0), i, 0)),
            pl.BlockSpec((TM, HEAD_DIM), lambda i, j: (i, 0)),
        ],
        out_shape=[
            jax.ShapeDtypeStruct((m, Q_DIM + 6 * KV_DIM), BF16),
        ] + [jax.ShapeDtypeStruct((M_PROMPT * N_KV, HEAD_DIM), F32)] * 6 + [
            jax.ShapeDtypeStruct((6, (m // TM) * M_SAMPLE * N_KV, HEAD_DIM), F32),
            jax.ShapeDtypeStruct((m, HEAD_DIM), F32),
        ],
        scratch_shapes=[pltpu.VMEM((TM, D_MODEL), BF16)],
        compiler_params=_cparams(("parallel", "arbitrary")),
        name="nsa_in_proj",
    )(x, g, w, wg, cos, sin)


def _dual_kernel(*refs, mode, has_bias):
    if has_bias:
        x_ref, g_ref, wa_ref, wb_ref, ba_ref, bb_ref, o_ref, xn_ref = refs
    else:
        x_ref, g_ref, wa_ref, wb_ref, o_ref, xn_ref = refs

    @pl.when(pl.program_id(1) == 0)
    def _():
        xn_ref[...] = _rms_scale(x_ref[...], g_ref[...]).astype(BF16)

    xn = xn_ref[...]
    a = _dot(xn, wa_ref[...])
    b = _dot(xn, wb_ref[...])
    if has_bias:
        a = a + ba_ref[...]
        b = b + bb_ref[...]
    if mode == "swiglu":
        o = jax.nn.silu(a) * b
    else:
        o = a * jax.nn.sigmoid(b)
    o_ref[...] = o.astype(o_ref.dtype)


def _dual_proj(x, g, wa, wb, layer, n_out, b_col0, bias, mode, out_dtype, name):
    m = x.shape[0]
    nb = b_col0 // TN
    in_specs = [
        pl.BlockSpec((TM, D_MODEL), lambda i, j: (i, 0)),
        pl.BlockSpec((1, D_MODEL), lambda i, j: (0, 0)),
        pl.BlockSpec((None, D_MODEL, TN), lambda i, j: (layer, 0, j)),
        pl.BlockSpec((None, D_MODEL, TN), lambda i, j: (layer, 0, j + nb)),
    ]
    args = [x, g, wa, wb]
    if bias is not None:
        in_specs += [pl.BlockSpec((1, TN), lambda i, j: (0, j)), pl.BlockSpec((1, TN), lambda i, j: (0, j + nb))]
        args += [bias, bias]
    return pl.pallas_call(
        functools.partial(_dual_kernel, mode=mode, has_bias=bias is not None),
        grid=(m // TM, n_out // TN),
        in_specs=in_specs,
        out_specs=pl.BlockSpec((TM, TN), lambda i, j: (i, j)),
        out_shape=jax.ShapeDtypeStruct((m, n_out), out_dtype),
        scratch_shapes=[pltpu.VMEM((TM, D_MODEL), BF16)],
        compiler_params=_cparams(("parallel", "arbitrary")),
        name=name,
    )(*args)


def _proj_norm_res_kernel(a_ref, w_ref, g_ref, h_ref, o_ref, acc_ref, *, nk):
    k = pl.program_id(1)
    part = _dot(a_ref[...], w_ref[...])

    @pl.when(k == 0)
    def _():
        acc_ref[...] = part

    @pl.when(k > 0)
    def _():
        acc_ref[...] += part

    @pl.when(k == nk - 1)
    def _():
        o_ref[...] = h_ref[...] + _rms_scale(acc_ref[...], g_ref[...])


def _proj_norm_res(a, w, layer, g, h, nk, name):
    m, kdim = a.shape
    tk = kdim // nk
    return pl.pallas_call(
        functools.partial(_proj_norm_res_kernel, nk=nk),
        grid=(m // TM_ROWS, nk),
        in_specs=[
            pl.BlockSpec((TM_ROWS, tk), lambda i, k: (i, k)),
            pl.BlockSpec((None, tk, D_MODEL), lambda i, k: (layer, k, 0)),
            pl.BlockSpec((1, D_MODEL), lambda i, k: (0, 0)),
            pl.BlockSpec((TM_ROWS, D_MODEL), lambda i, k: (i, 0)),
        ],
        out_specs=pl.BlockSpec((TM_ROWS, D_MODEL), lambda i, k: (i, 0)),
        out_shape=jax.ShapeDtypeStruct((m, D_MODEL), F32),
        scratch_shapes=[pltpu.VMEM((TM_ROWS, D_MODEL), F32)],
        compiler_params=_cparams(("parallel", "arbitrary")),
        name=name,
    )(a, w, g, h)


def _ple_kernel(h_ref, hc_ref, p_ref, wg_ref, wp_ref, o_ref, *rest, split):
    hb_ref = rest[-1]

    @pl.when(pl.program_id(1) == 0)
    def _():
        hb_ref[...] = h_ref[...].astype(BF16)

    gate = jax.nn.sigmoid(_dot(hb_ref[...], wg_ref[...]))
    out = hc_ref[...] + gate * _dot(p_ref[...], wp_ref[...])
    o_ref[...] = out
    if split:
        rest[0][...] = out[TM - M_SAMPLE:]


def _ple(h, p, wg, wp, layer, name, split=False):
    m = h.shape[0]
    out_specs = pl.BlockSpec((TM, TN), lambda i, j: (i, j))
    out_shape = jax.ShapeDtypeStruct((m, D_MODEL), F32)
    if split:
        assert (m // TM - 1) * TM + (TM - M_SAMPLE) == M_PROMPT
        out_specs = [out_specs, pl.BlockSpec((M_SAMPLE, TN), lambda i, j: (i, j))]
        out_shape = [jax.ShapeDtypeStruct((M_PROMPT, D_MODEL), F32),
                     jax.ShapeDtypeStruct(((m // TM) * M_SAMPLE, D_MODEL), F32)]
    return pl.pallas_call(
        functools.partial(_ple_kernel, split=split),
        grid=(m // TM, D_MODEL // TN),
        in_specs=[
            pl.BlockSpec((TM, D_MODEL), lambda i, j: (i, 0)),
            pl.BlockSpec((TM, TN), lambda i, j: (i, j)),
            pl.BlockSpec((None, TM, PLE_DIM), lambda i, j: (layer, i, 0)),
            pl.BlockSpec((None, D_MODEL, TN), lambda i, j: (layer, 0, j)),
            pl.BlockSpec((None, PLE_DIM, TN), lambda i, j: (layer, 0, j)),
        ],
        out_specs=out_specs,
        out_shape=out_shape,
        scratch_shapes=[pltpu.VMEM((TM, D_MODEL), BF16)],
        compiler_params=_cparams(("parallel", "arbitrary")),
        name=name,
    )(h, h, p, wg, wp)


TILES_PER_CHUNK = CMP_STRIDE * N_KV // 8


def _compress_kernel(pt_ref, *refs, n):
    k_refs, v_refs = refs[0:n], refs[n:2 * n]
    wk_ref, wv_ref, pk0_ref, pk1_ref, pv0_ref, pv1_ref = refs[2 * n:2 * n + 6]
    for srcs, w_ref, o0_ref, o1_ref in ((k_refs, wk_ref, pk0_ref, pk1_ref), (v_refs, wv_ref, pv0_ref, pv1_ref)):
        acc = None
        for jj in range(TILES_PER_CHUNK // 2):
            halves = []
            for j in (2 * jj, 2 * jj + 1):
                rows = [srcs[p][(c * TILES_PER_CHUNK + j) * 8:(c * TILES_PER_CHUNK + j + 1) * 8, :]
                        for p in range(n) for c in range(CHUNKS_PER_PAGE)]
                halves.append(jnp.concatenate(rows, axis=0))
            term = _dot(jnp.concatenate(halves, axis=1).astype(BF16), w_ref[jj])
            acc = term if acc is None else acc + term
        width = 2 * HEAD_DIM
        folded = acc[:, :width] + pltpu.roll(acc[:, width:], acc.shape[0] - N_KV, axis=0)
        o0_ref[0] = folded[:, :HEAD_DIM]
        o1_ref[0] = folded[:, HEAD_DIM:]


def _compress_stage1(page_table, k_src, k_type, v_src, v_type, wk, wv, n_seq, pages_per_seq, name):
    n = min(PAGES_PER_STEP, pages_per_seq)
    groups = pages_per_seq // n
    n_chunks = pages_per_seq * CHUNKS_PER_PAGE

    def src_spec(slot, typ):
        return pl.BlockSpec(
            (None, PAGE_ROWS, HEAD_DIM), lambda b, t, pt: (typ, pt[b * pages_per_seq + t * n + slot], 0))

    w_spec = pl.BlockSpec(wk.shape, lambda b, t, pt: (0, 0, 0))
    out_spec = pl.BlockSpec((1, n * CHUNKS_PER_PAGE * 8, HEAD_DIM), lambda b, t, pt: (b, t, 0))
    out_sds = jax.ShapeDtypeStruct((n_seq, n_chunks * 8, HEAD_DIM), F32)
    return pl.pallas_call(
        functools.partial(_compress_kernel, n=n),
        grid_spec=pltpu.PrefetchScalarGridSpec(
            num_scalar_prefetch=1,
            grid=(n_seq, groups),
            in_specs=[src_spec(s, k_type) for s in range(n)] + [src_spec(s, v_type) for s in range(n)]
            + [w_spec, w_spec],
            out_specs=[out_spec] * 4,
        ),
        out_shape=[out_sds] * 4,
        compiler_params=_cparams(("parallel", "arbitrary")),
        name=name,
    )(page_table, *([k_src] * n), *([v_src] * n), wk, wv)


def _compress_finish_kernel(p0_ref, p1_ref, pn_ref, pe_ref, w1_ref, w2_ref, o_ref):
    c = o_ref.shape[2]
    bias = _dot(pe_ref[...].astype(BF16), w1_ref[...])[0:1]
    row = lax.broadcasted_iota(jnp.int32, (c, 1), 0)
    for g in range(N_KV):
        p0 = p0_ref[0, pl.ds(g, c, stride=8), :]
        p1 = p1_ref[0, pl.ds(g, c, stride=8), :]
        nxt = pltpu.roll(p1, c - 1, axis=0)
        nxt = jnp.where(row == c - 1, pn_ref[g:g + 1, :], nxt)
        hid = p0 + nxt + bias
        o_ref[0, g] = _dot(jax.nn.silu(hid).astype(BF16), w2_ref[...]).astype(o_ref.dtype)


def _compress_finish(p0, p1, p1_next, pe8, w1, w2, name):
    n_seq, rows, _ = p0.shape
    c = rows // 8
    seq_spec = pl.BlockSpec((1, rows, HEAD_DIM), lambda b: (b, 0, 0))
    return pl.pallas_call(
        _compress_finish_kernel,
        grid=(n_seq,),
        in_specs=[
            seq_spec, seq_spec,
            pl.BlockSpec((8, HEAD_DIM), lambda b: (b, 0)),
            pl.BlockSpec((8, CMP_BLOCK * HEAD_DIM), lambda b: (0, 0)),
            pl.BlockSpec((CMP_BLOCK * HEAD_DIM, HEAD_DIM), lambda b: (0, 0)),
            pl.BlockSpec((HEAD_DIM, HEAD_DIM), lambda b: (0, 0)),
        ],
        out_specs=pl.BlockSpec((1, N_KV, c, HEAD_DIM), lambda b: (b, 0, 0, 0)),
        out_shape=jax.ShapeDtypeStruct((n_seq, N_KV, c, HEAD_DIM), BF16),
        compiler_params=_cparams(("parallel",)),
        name=name,
    )(p0, p1, p1_next, pe8, w1, w2)


def _masked_softmax2(s, mask):
    s = jnp.where(mask, s, NEG)
    m = jnp.max(s, axis=-1, keepdims=True)
    e = jnp.where(mask, jnp.exp2(s - m), 0.0)
    return e / jnp.maximum(jnp.sum(e, axis=-1, keepdims=True), 1e-30)


def _split3(x):
    parts = []
    rem = x
    for _ in range(3):
        part = rem.astype(BF16)
        parts.append(part)
        rem = rem - part.astype(F32)
    return parts


def _sel_tile(kt, carry, qa, t_row, ks, vs, e_ref, e_row0, tk, masked, lanes=slice(0, HEAD_DIM)):
    m, l, acc = carry
    r0 = pl.multiple_of(kt * tk, tk)
    e0 = pl.multiple_of(kt * tk - e_row0, tk)
    ka = jnp.concatenate([ks[pl.ds(r0, tk), lanes], e_ref[pl.ds(e0, tk), :]], axis=1)
    sc = _dot_nt(qa, ka)
    if masked:
        kpos = r0 + lax.broadcasted_iota(jnp.int32, (1, tk), 1)
        sc = jnp.where(kpos <= t_row, sc, NEG)
    m_new = jnp.maximum(m, jnp.max(sc, axis=-1, keepdims=True))
    alpha = jnp.exp2(m - m_new)
    p = jnp.exp2(sc - m_new)
    l = alpha * l + jnp.sum(p, axis=-1, keepdims=True)
    acc = alpha * acc + _dot(p.astype(BF16), vs[pl.ds(r0, tk), lanes])
    return m_new, l, acc


def _sel_init(rows):
    return (jnp.full((rows, 1), -jnp.inf, F32), jnp.zeros((rows, 1), F32), jnp.zeros((rows, HEAD_DIM), F32))


def _gated_merge(gt, o_c, o_s, o_w, tq):
    outs = []
    for r in range(GQA_R):
        sl = slice(r * tq, (r + 1) * tq)
        outs.append(gt[:, r:r + 1] * o_c[sl] + gt[:, GQA_R + r:GQA_R + r + 1] * o_s[sl]
                    + gt[:, 2 * GQA_R + r:2 * GQA_R + r + 1] * o_w[sl])
    return jnp.concatenate(outs, axis=1)


def _cmp_to_sel_matrix(nb, nsp):
    n = np.arange(nb)[:, None]
    j = np.arange(nsp)[None, :]
    return ((n >= 4 * j) & (n <= 4 * j + 3)).astype(np.float32) + ((n >= 4 * j - 1) & (n <= 4 * j + 2)).astype(np.float32)


def _block_onehot(n_keys):
    k = np.arange(n_keys)[:, None] // SEL_BLOCK
    return jnp.asarray((k % 128 == np.arange(128)[None, :]).astype(np.float32), BF16)


def _window_bias_prompt():
    q = np.arange(TQ_PROMPT)[:, None]
    kb = np.arange(WBAND_PROMPT)[None, :]
    pats = [kb <= TQ_PROMPT * i + q for i in range(WINDOW // TQ_PROMPT)]
    pats.append((kb > q) & (kb <= q + WINDOW))
    return jnp.asarray(np.where(np.stack(pats), 0.0, NEG), F32)


def _attn_prompt_kernel(q_ref, gate_ref, kc_ref, vc_ref, ks_ref, vs_ref, kw_ref, vw_ref, mmt_ref, e_ref, wb_ref,
                        o_init_ref, o_ref, *, nb, ns):
    del o_init_ref
    tq = TQ_PROMPT
    rows = GQA_R * tq
    qi = pl.program_id(2)
    t0 = qi * tq
    t_row = t0 + (lax.broadcasted_iota(jnp.int32, (rows, 1), 0) & (tq - 1))
    cmp_end = lax.broadcasted_iota(jnp.int32, (1, nb), 1) * CMP_STRIDE + (CMP_BLOCK - 1)
    jb = lax.broadcasted_iota(jnp.int32, (ns, tq), 0)
    cur = lax.shift_right_logical(t0 + lax.broadcasted_iota(jnp.int32, (1, tq), 1), 6)
    forced = (jb == 0) | (jb == cur) | (jb == cur - 1)
    eligible = jb <= cur
    groups = ns // 8
    jgrp = [jb[8 * v:8 * v + 8] for v in range(groups)]
    eye = jnp.where(lax.broadcasted_iota(jnp.int32, (tq, tq), 0) == lax.broadcasted_iota(jnp.int32, (tq, tq), 1),
                    1.0, 0.0).astype(BF16)
    mmt = mmt_ref[...]

    heads = range(ATT_HEADS_PER_STEP)
    qs, o_cs, qas = [], [], []
    for h in heads:
        qq = q_ref[:, h * GQA_R * HEAD_DIM:(h + 1) * GQA_R * HEAD_DIM]
        q = jnp.concatenate([qq[:, r * HEAD_DIM:(r + 1) * HEAD_DIM] for r in range(GQA_R)], axis=0)

        s = _dot_nt(q, kc_ref[0, h])
        p_c = _masked_softmax2(s, cmp_end <= t_row)
        o_cs.append(_dot(p_c.astype(BF16), vc_ref[0, h]))
        imp = p_c[0:tq]
        for r in range(1, GQA_R):
            imp = imp + p_c[r * tq:(r + 1) * tq]

        blk_imp = None
        for part in _split3(imp):
            term = _dot_nt(mmt, part)
            blk_imp = term if blk_imp is None else blk_imp + term
        score = jnp.where(forced, jnp.inf, jnp.where(eligible, blk_imp, -jnp.inf))
        sgrp = [score[8 * v:8 * v + 8] for v in range(groups)]
        rank = [jnp.zeros((8, tq), F32) for _ in range(groups)]
        for i in range(ns):
            ci = score[i:i + 1]
            for v in range(groups):
                if 8 * v > i:
                    beats = ci >= sgrp[v]
                elif 8 * v + 7 <= i:
                    beats = ci > sgrp[v]
                else:
                    beats = (ci > sgrp[v]) | ((ci == sgrp[v]) & (jgrp[v] > i))
                rank[v] = rank[v] + jnp.where(beats, 1.0, 0.0)
        rank = jnp.concatenate(rank, axis=0)
        sel_t = jnp.where((rank < float(TOPK)) & eligible, 1.0, 0.0)
        sel_t = jnp.concatenate([sel_t, jnp.zeros((128 - ns, tq), F32)], axis=0).astype(BF16)
        sel = _dot_nt(eye, sel_t)
        negsel = jnp.where(sel > 0.5, 0.0, NEG).astype(BF16)
        qs.append(q)
        qas.append(jnp.concatenate([q, jnp.concatenate([negsel] * GQA_R, axis=0)], axis=1))

    n_tiles = lax.shift_right_logical(t0 + tq - 1, TK_ATT.bit_length() - 1) + 1

    def tiles(kt, carries, masked):
        return tuple(
            _sel_tile(kt, carries[h], qa=qas[h], t_row=t_row, ks=ks_ref, vs=vs_ref, e_ref=e_ref, e_row0=0,
                      tk=TK_ATT, masked=masked, lanes=slice(h * HEAD_DIM, (h + 1) * HEAD_DIM))
            for h in heads)

    carries = lax.fori_loop(0, n_tiles - 1, functools.partial(tiles, masked=False),
                            tuple(_sel_init(rows) for _ in heads))
    carries = tiles(n_tiles - 1, carries, masked=True)

    wstart = pl.multiple_of(jnp.maximum(t0 - WINDOW, 0), tq)
    bias = wb_ref[jnp.minimum(qi, WINDOW // tq)]
    bias = jnp.concatenate([bias] * GQA_R, axis=0)
    for h in heads:
        lanes = slice(h * HEAD_DIM, (h + 1) * HEAD_DIM)
        _, l, acc = carries[h]
        o_s = acc / jnp.maximum(l, 1e-30)
        s = _dot_nt(qs[h], kw_ref[pl.ds(wstart, WBAND_PROMPT), lanes]) + bias
        e = jnp.exp2(s - jnp.max(s, axis=-1, keepdims=True))
        p_w = e / jnp.sum(e, axis=-1, keepdims=True)
        o_w = _dot(p_w.astype(BF16), vw_ref[pl.ds(wstart, WBAND_PROMPT), lanes])
        o_ref[:, h * GQA_R * HEAD_DIM:(h + 1) * GQA_R * HEAD_DIM] = _gated_merge(
            gate_ref[h], o_cs[h], o_s, o_w, tq).astype(o_ref.dtype)


def _attn_prompt(qkv, gates_t, kc, vc, o_init):
    tq = TQ_PROMPT
    hp = ATT_HEADS_PER_STEP
    nq = SEQ // tq
    nb = SEQ // CMP_STRIDE
    ns = SEQ // SEL_BLOCK
    k_sel0, v_sel0 = Q_DIM + 2 * KV_DIM, Q_DIM + 3 * KV_DIM
    k_win0, v_win0 = Q_DIM + 4 * KV_DIM, Q_DIM + 5 * KV_DIM
    seq_spec = lambda base: pl.BlockSpec((SEQ, hp * HEAD_DIM), lambda b, g, i: (b, base // (hp * HEAD_DIM) + g))
    blk_spec = pl.BlockSpec((1, hp, nb, HEAD_DIM), lambda b, g, i: (b, g, 0, 0))
    mmt = jnp.asarray(_cmp_to_sel_matrix(nb, ns).T, BF16)
    wbias = _window_bias_prompt()
    return pl.pallas_call(
        functools.partial(_attn_prompt_kernel, nb=nb, ns=ns),
        grid=(BATCH, N_KV // hp, nq),
        in_specs=[
            pl.BlockSpec((tq, hp * GQA_R * HEAD_DIM), lambda b, g, i: (b * nq + i, g)),
            pl.BlockSpec((hp, tq, HEAD_DIM), lambda b, g, i: (g, b * nq + i, 0)),
            blk_spec, blk_spec,
            seq_spec(k_sel0), seq_spec(v_sel0), seq_spec(k_win0), seq_spec(v_win0),
            pl.BlockSpec((ns, nb), lambda b, g, i: (0, 0)),
            pl.BlockSpec((SEQ, 128), lambda b, g, i: (0, 0)),
            pl.BlockSpec(wbias.shape, lambda b, g, i: (0, 0, 0)),
            pl.BlockSpec(memory_space=pl.ANY),
        ],
        out_specs=pl.BlockSpec((tq, hp * GQA_R * HEAD_DIM), lambda b, g, i: (b * nq + i, g)),
        out_shape=jax.ShapeDtypeStruct((M_ALL, Q_DIM), BF16),
        input_output_aliases={11: 0},
        compiler_params=_cparams(("parallel", "parallel", "arbitrary")),
        name="nsa_attn_prompt",
    )(qkv, gates_t, kc, vc, qkv, qkv, qkv, qkv, mmt, _block_onehot(SEQ), wbias, o_init)


SAMPLE_NB = PAST_LEN // CMP_STRIDE
SAMPLE_NS = -(-(PAST_LEN + DEC_SEQ) // SEL_BLOCK)
SAMPLE_NSP = 384
SAMPLE_PAGES_PER_STEP = 16
KEYS_PER_STEP = SAMPLE_PAGES_PER_STEP * PAGE_SIZE
BLOCKS_PER_STEP = KEYS_PER_STEP // SEL_BLOCK
SAMPLE_STEPS = N_PAGES // SAMPLE_PAGES_PER_STEP


def _attn_sample_kernel(pt_ref, q_ref, gate_ref, kc_ref, vc_ref, kw_ref, vw_ref, mmat_ref, e_ref, *refs):
    n = SAMPLE_PAGES_PER_STEP
    k_refs, v_refs = refs[0:n], refs[n:2 * n]
    kn_ref, vn_ref, o_ref, qa_ref, m_ref, l_ref, acc_ref, oc_ref, ow_ref = refs[2 * n:]
    tq, nb, ns, nsp = DEC_SEQ, SAMPLE_NB, SAMPLE_NS, SAMPLE_NSP
    rows = GQA_R * tq
    t0 = PAST_LEN
    t = pl.program_id(1)
    t_row = t0 + (lax.broadcasted_iota(jnp.int32, (rows, 1), 0) & (tq - 1))

    @pl.when(t == 0)
    def _():
        cmp_end = lax.broadcasted_iota(jnp.int32, (1, nb), 1) * CMP_STRIDE + (CMP_BLOCK - 1)
        jb = lax.broadcasted_iota(jnp.int32, (tq, nsp), 1)
        cur = lax.shift_right_logical(t0 + lax.broadcasted_iota(jnp.int32, (tq, 1), 0), 6)
        forced = (jb == 0) | (jb == cur) | (jb == cur - 1)
        eligible = jb <= cur
        kpos = (PAST_LEN - WINDOW) + lax.broadcasted_iota(jnp.int32, (1, WBAND_SAMPLE), 1)
        win_mask = (kpos <= t_row) & (kpos > t_row - WINDOW)
        mmat = mmat_ref[...]
        for g in range(N_KV):
            q = q_ref[0, g]

            p_c = _masked_softmax2(_dot_nt(q, kc_ref[0, g]), cmp_end <= t_row)
            oc_ref[g] = _dot(p_c.astype(BF16), vc_ref[0, g])
            imp = p_c[0:tq]
            for r in range(1, GQA_R):
                imp = imp + p_c[r * tq:(r + 1) * tq]

            blk_imp = None
            for part in _split3(imp):
                term = _dot(part, mmat)
                blk_imp = term if blk_imp is None else blk_imp + term
            score = jnp.where(forced, jnp.inf, jnp.where(eligible, blk_imp, -jnp.inf))
            rank = jnp.zeros((tq, nsp), F32)
            for i in range(ns):
                ci = score[:, i:i + 1]
                beats = (ci > score) | ((ci == score) & (jb > i))
                rank = rank + jnp.where(beats, 1.0, 0.0)
            negsel = jnp.where((rank < float(TOPK)) & eligible, 0.0, NEG)
            negsel = jnp.concatenate([negsel] * GQA_R, axis=0).astype(BF16)
            pad = jnp.zeros((rows, HEAD_DIM - BLOCKS_PER_STEP), BF16)
            for st in range(SAMPLE_STEPS + 1):
                piece = negsel[:, st * BLOCKS_PER_STEP:(st + 1) * BLOCKS_PER_STEP]
                qa_ref[g, st] = jnp.concatenate([q, piece, pad], axis=1)

            p_w = _masked_softmax2(_dot_nt(q, kw_ref[0, g]), win_mask)
            ow_ref[g] = _dot(p_w.astype(BF16), vw_ref[0, g])

            m_ref[g] = jnp.full((rows, 1), -jnp.inf, F32)
            l_ref[g] = jnp.zeros((rows, 1), F32)
            acc_ref[g] = jnp.zeros((rows, HEAD_DIM), F32)

    def update(g, qa, ka, v, mask):
        sc = _dot_nt(qa, ka)
        if mask is not None:
            sc = jnp.where(mask, sc, NEG)
        m = m_ref[g]
        m_new = jnp.maximum(m, jnp.max(sc, axis=-1, keepdims=True))
        alpha = jnp.exp2(m - m_new)
        p = jnp.exp2(sc - m_new)
        l_ref[g] = alpha * l_ref[g] + jnp.sum(p, axis=-1, keepdims=True)
        acc_ref[g] = alpha * acc_ref[g] + _dot(p.astype(BF16), v)
        m_ref[g] = m_new

    def head_rows(page_refs, g):
        return jnp.concatenate([r[pl.ds(g, PAGE_SIZE, stride=N_KV), :] for r in page_refs], axis=0).astype(BF16)

    e_blk = e_ref[...]
    for g in range(N_KV):
        ka = jnp.concatenate([head_rows(k_refs, g), e_blk], axis=1)
        update(g, qa_ref[g, t], ka, head_rows(v_refs, g), None)

    @pl.when(t == SAMPLE_STEPS - 1)
    def _():
        kpos = t0 + lax.broadcasted_iota(jnp.int32, (1, PAGE_SIZE), 1)
        gt = gate_ref[...]
        for g in range(N_KV):
            ka = jnp.concatenate([head_rows([kn_ref], g), e_blk[0:PAGE_SIZE]], axis=1)
            update(g, qa_ref[g, SAMPLE_STEPS], ka, head_rows([vn_ref], g), kpos <= t_row)
            o_s = acc_ref[g] / jnp.maximum(l_ref[g], 1e-30)
            o_ref[:, g * GQA_R * HEAD_DIM:(g + 1) * GQA_R * HEAD_DIM] = _gated_merge(
                gt[g], oc_ref[g], o_s, ow_ref[g], tq)


def _attn_sample(page_table, q_s, gates_s, kc, vc, kw, vw, k_pool, v_pool, k_new, v_new):
    n = SAMPLE_PAGES_PER_STEP
    rows = GQA_R * DEC_SEQ
    seq_spec = lambda r: pl.BlockSpec((1, N_KV, r, HEAD_DIM), lambda b, t, pt: (b, 0, 0, 0))
    page = (PAGE_ROWS, HEAD_DIM)
    pool_spec = lambda slot: pl.BlockSpec(page, lambda b, t, pt: (pt[b * N_PAGES + t * n + slot], 0))
    new_spec = pl.BlockSpec(page, lambda b, t, pt: (b, 0))
    mmat = jnp.asarray(_cmp_to_sel_matrix(SAMPLE_NB, SAMPLE_NSP), BF16)
    return pl.pallas_call(
        _attn_sample_kernel,
        grid_spec=pltpu.PrefetchScalarGridSpec(
            num_scalar_prefetch=1,
            grid=(DEC_BATCH, SAMPLE_STEPS),
            in_specs=[
                seq_spec(rows),
                pl.BlockSpec((N_KV, DEC_SEQ, HEAD_DIM), lambda b, t, pt: (0, b, 0)),
                seq_spec(SAMPLE_NB), seq_spec(SAMPLE_NB),
                seq_spec(WBAND_SAMPLE), seq_spec(WBAND_SAMPLE),
                pl.BlockSpec((SAMPLE_NB, SAMPLE_NSP), lambda b, t, pt: (0, 0)),
                pl.BlockSpec((KEYS_PER_STEP, HEAD_DIM), lambda b, t, pt: (0, 0)),
            ] + [pool_spec(s) for s in range(n)] * 2 + [new_spec, new_spec],
            out_specs=pl.BlockSpec((DEC_SEQ, Q_DIM), lambda b, t, pt: (b, 0)),
            scratch_shapes=[
                pltpu.VMEM((N_KV, SAMPLE_STEPS + 1, rows, 2 * HEAD_DIM), BF16),
                pltpu.VMEM((N_KV, rows, 1), F32),
                pltpu.VMEM((N_KV, rows, 1), F32),
                pltpu.VMEM((N_KV, rows, HEAD_DIM), F32),
                pltpu.VMEM((N_KV, rows, HEAD_DIM), F32),
                pltpu.VMEM((N_KV, rows, HEAD_DIM), F32),
            ],
        ),
        out_shape=jax.ShapeDtypeStruct((M_SAMPLE, Q_DIM), F32),
        compiler_params=_cparams(("parallel", "arbitrary")),
        name="nsa_attn_sample",
    )(page_table, q_s, gates_s, kc, vc, kw, vw, mmat, _block_onehot(KEYS_PER_STEP),
      *([k_pool] * n), *([v_pool] * n), k_new, v_new)


HALO = 32


def _conv_kernel(u_ref, halo_ref, wdw_ref, bdw_ref, lng_ref, lnb_ref, w2_ref, b2_ref, g_ref, h_ref, *rest,
                 t, tiles_per_seq, halo_is_state):
    o_ref, buf_ref, y_ref = rest[-3:]
    i = pl.program_id(0)
    if halo_is_state:
        buf_ref[0:HALO, :] = halo_ref[...]
    else:
        first = (i % tiles_per_seq) == 0
        buf_ref[0:HALO, :] = jnp.where(first, 0.0, halo_ref[...])
    buf_ref[HALO:HALO + t, :] = u_ref[...]
    off = HALO - (CONV_W - 1)
    for c in range(D_MODEL // HEAD_DIM):
        cs = slice(c * HEAD_DIM, (c + 1) * HEAD_DIM)
        base = buf_ref[:, cs]
        w = wdw_ref[:, cs]
        y = bdw_ref[:, cs]
        n_rows = HALO + t
        for s in range(8):
            taps = [k for k in range(CONV_W) if (off + k) % 8 == s]
            shifted = base if s == 0 else pltpu.roll(base, n_rows - s, axis=0)
            for k in taps:
                a = off + k - s
                y = y + shifted[a:a + t] * w[k:k + 1]
        y_ref[:, cs] = y
    y = y_ref[...]
    mu = jnp.mean(y, axis=-1, keepdims=True)
    yc = y - mu
    var = jnp.mean(yc * yc, axis=-1, keepdims=True)
    z = yc * lax.rsqrt(var + LN_EPS) * lng_ref[...] + lnb_ref[...]
    mval = _dot(jax.nn.silu(z).astype(BF16), w2_ref[...]) + b2_ref[...]
    o_ref[...] = h_ref[...] + _rms_scale(mval, g_ref[...])


def _conv_tail(u, halo_src, wdw, bdw, lng, lnb, w2, b2, g, h, *, t, n_tiles, row0, tiles_per_seq, halo_is_state,
               out_init=None):
    blk0 = row0 // t
    extra_specs, extra_args, aliases = [], [], {}
    out_rows = n_tiles * t
    if out_init is not None:
        extra_specs, extra_args, aliases = [pl.BlockSpec(memory_space=pl.ANY)], [out_init], {10: 0}
        out_rows = out_init.shape[0]
    if halo_is_state:
        halo_spec = pl.BlockSpec((HALO, D_MODEL), lambda i: (i, 0))
    else:
        per = t // HALO
        halo_spec = pl.BlockSpec((HALO, D_MODEL), lambda i: (jnp.maximum((blk0 + i) * per - 1, 0), 0))
    vec = pl.BlockSpec((1, D_MODEL), lambda i: (0, 0))
    kern = functools.partial(_conv_kernel, t=t, tiles_per_seq=tiles_per_seq, halo_is_state=halo_is_state)
    return pl.pallas_call(
        kern,
        grid=(n_tiles,),
        in_specs=[
            pl.BlockSpec((t, D_MODEL), lambda i: (blk0 + i, 0)),
            halo_spec,
            pl.BlockSpec((HALO, D_MODEL), lambda i: (0, 0)),
            vec, vec, vec,
            pl.BlockSpec((None, D_MODEL, D_MODEL), lambda i: (0, 0, 0)),
            vec, vec,
            pl.BlockSpec((t, D_MODEL), lambda i: (blk0 + i, 0)),
        ] + extra_specs,
        out_specs=pl.BlockSpec((t, D_MODEL), lambda i: (i, 0)),
        out_shape=jax.ShapeDtypeStruct((out_rows, D_MODEL), F32),
        input_output_aliases=aliases,
        scratch_shapes=[pltpu.VMEM((HALO + t, D_MODEL), F32), pltpu.VMEM((t, D_MODEL), F32)],
        compiler_params=_cparams(("parallel",)),
        name="conv_tail_state" if halo_is_state else "conv_tail",
    )(u, halo_src, wdw, bdw, lng, lnb, w2, b2, g, h, *extra_args)


def _row(v):
    return v.reshape(1, -1).astype(F32)


def _rope_tables():
    half = HEAD_DIM // 2
    pos = np.concatenate([np.tile(np.arange(SEQ), BATCH), np.tile(PAST_LEN + np.arange(DEC_SEQ), DEC_BATCH)])
    inv = (ROPE_THETA ** (-np.arange(half, dtype=np.float64) / half)).astype(np.float32)
    ang = (pos.astype(np.float32)[:, None] * inv[None, :]).astype(np.float64)
    cos, sin = np.cos(ang).astype(np.float32), np.sin(ang).astype(np.float32)
    return jnp.asarray(np.concatenate([cos, cos], axis=1)), jnp.asarray(np.concatenate([-sin, sin], axis=1))


def _cmp_weights(w1):
    w1r = w1.reshape(CMP_BLOCK // CMP_STRIDE, CMP_STRIDE * HEAD_DIM, HEAD_DIM)
    per_pos = jnp.concatenate([w1r[0], w1r[1]], axis=1).reshape(CMP_STRIDE, HEAD_DIM, 2 * HEAD_DIM)
    per_tile = jnp.concatenate([per_pos[0::2], per_pos[1::2]], axis=2)
    return per_tile.reshape(TILES_PER_CHUNK // 2, 2 * HEAD_DIM, 4 * HEAD_DIM).astype(BF16)


def kernel(x_prompt, x_sample, p_prompt, p_sample, page_table, cache_k_cmp, cache_v_cmp, cache_k_sel, cache_v_sel, cache_k_win, cache_v_win, state_conv, norm_mix_pre, norm_mix_post, norm_ffn_pre, norm_ffn_post, ffn_w_gate, ffn_w_up, ffn_w_down, ple_w_proj, ple_w_gate, nsa_w_in, nsa_w_out, nsa_cmp_pe_k, nsa_cmp_w1_k, nsa_cmp_w2_k, nsa_cmp_pe_v, nsa_cmp_w1_v, nsa_cmp_w2_v, conv_w_pw1, conv_b_pw1, conv_w_dw, conv_b_dw, conv_ln_g, conv_ln_b, conv_w_pw2, conv_b_pw2):
    h = jnp.concatenate([x_prompt.reshape(M_PROMPT, D_MODEL), x_sample.reshape(M_SAMPLE, D_MODEL)], axis=0)
    p_all = jnp.concatenate([p_prompt.reshape(DEPTH, M_PROMPT, PLE_DIM),
                             p_sample.reshape(DEPTH, M_SAMPLE, PLE_DIM)], axis=1).astype(BF16)
    pt_flat = page_table.reshape(-1)
    w_gate, w_up, w_down = ffn_w_gate.astype(BF16), ffn_w_up.astype(BF16), ffn_w_down.astype(BF16)
    w_ple_gate, w_ple = ple_w_gate.astype(BF16), ple_w_proj.astype(BF16)

    def residual_tail(h, layer, split=False):
        act = _dual_proj(h, _row(norm_ffn_pre[layer]), w_gate, w_up, layer, D_FF, 0, None, "swiglu", BF16,
                         f"ffn_up_{layer}")
        h = _proj_norm_res(act, w_down, layer, _row(norm_ffn_post[layer]), h, 4, f"ffn_down_{layer}")
        return _ple(h, p_all, w_ple_gate, w_ple, layer, f"ple_{layer}", split=split)

    n_main = Q_DIM + 6 * KV_DIM
    w_gates = jnp.pad(nsa_w_in[0][:, n_main:], ((0, 0), (0, HEAD_DIM - 3 * N_HEADS))).astype(BF16)
    cos, sin = _rope_tables()
    qkv, *kvp, kvs, gates = _in_proj(h, _row(norm_mix_pre[0]), nsa_w_in.astype(BF16), w_gates, cos, sin)

    gates_t = gates[:, :3 * N_HEADS].reshape(M_ALL, 3, N_KV, GQA_R).transpose(2, 0, 1, 3).reshape(N_KV, M_ALL, 3 * GQA_R)
    gates_t = jnp.pad(gates_t, ((0, 0), (0, 0), (0, HEAD_DIM - 3 * GQA_R)))

    kv_s = kvs[:, -M_SAMPLE * N_KV:].reshape(6, DEC_BATCH, DEC_SEQ, N_KV, HEAD_DIM)
    new_page = lambda c: jnp.pad(kv_s[c], ((0, 0), (0, PAGE_SIZE - DEC_SEQ), (0, 0), (0, 0))).reshape(-1, HEAD_DIM)

    wk, wv = _cmp_weights(nsa_cmp_w1_k[0]), _cmp_weights(nsa_cmp_w1_v[0])
    pe_k = jnp.broadcast_to(nsa_cmp_pe_k[0].reshape(1, -1), (8, CMP_BLOCK * HEAD_DIM))
    pe_v = jnp.broadcast_to(nsa_cmp_pe_v[0].reshape(1, -1), (8, CMP_BLOCK * HEAD_DIM))
    w1k, w1v = nsa_cmp_w1_k[0].astype(BF16), nsa_cmp_w1_v[0].astype(BF16)
    w2k, w2v = nsa_cmp_w2_k[0].astype(BF16), nsa_cmp_w2_v[0].astype(BF16)

    pages_prompt = SEQ // PAGE_SIZE
    ident = jnp.arange(BATCH * pages_prompt, dtype=jnp.int32)
    pk0, pk1, pv0, pv1 = _compress_stage1(ident, kvp[0][None], 0, kvp[1][None], 0, wk, wv, BATCH, pages_prompt,
                                          "nsa_cmp_prompt")
    zero_next = jnp.zeros((BATCH * 8, HEAD_DIM), F32)
    kc_p = _compress_finish(pk0, pk1, zero_next, pe_k, w1k, w2k, "nsa_cmp_fin_k_prompt")
    vc_p = _compress_finish(pv0, pv1, zero_next, pe_v, w1v, w2v, "nsa_cmp_fin_v_prompt")

    pool3 = lambda c: c.reshape(1, -1, HEAD_DIM)
    pk0, pk1, pv0, pv1 = _compress_stage1(pt_flat, pool3(cache_k_cmp), 0, pool3(cache_v_cmp), 0, wk, wv, DEC_BATCH,
                                          N_PAGES, "nsa_cmp_sample")
    _, pk1_n, _, pv1_n = _compress_stage1(jnp.arange(DEC_BATCH, dtype=jnp.int32), new_page(0)[None], 0,
                                          new_page(1)[None], 0, wk, wv, 1, DEC_BATCH, "nsa_cmp_sample_new")
    first_chunk = lambda pn: pn[0].reshape(DEC_BATCH, CHUNKS_PER_PAGE * 8, HEAD_DIM)[:, 0:8].reshape(-1, HEAD_DIM)
    kc_s = _compress_finish(pk0, pk1, first_chunk(pk1_n), pe_k, w1k, w2k, "nsa_cmp_fin_k_sample")
    vc_s = _compress_finish(pv0, pv1, first_chunk(pv1_n), pe_v, w1v, w2v, "nsa_cmp_fin_v_sample")

    pool2 = lambda c: c.reshape(-1, HEAD_DIM)
    kw_all = jnp.concatenate([cache_k_win[0], kv_s[4]], axis=1)
    vw_all = jnp.concatenate([cache_v_win[0], kv_s[5]], axis=1)
    band = lambda w: jnp.pad(w.transpose(0, 2, 1, 3),
                             ((0, 0), (0, 0), (0, WBAND_SAMPLE - w.shape[1]), (0, 0))).astype(BF16)
    q_s = qkv[M_PROMPT:, :Q_DIM].reshape(DEC_BATCH, DEC_SEQ, N_KV, GQA_R, HEAD_DIM).transpose(0, 2, 3, 1, 4)
    q_s = q_s.reshape(DEC_BATCH, N_KV, GQA_R * DEC_SEQ, HEAD_DIM)
    o_s = _attn_sample(pt_flat, q_s, gates_t[:, M_PROMPT:], kc_s, vc_s, band(kw_all), band(vw_all),
                       pool2(cache_k_sel), pool2(cache_v_sel), new_page(2), new_page(3))
    o_all = _attn_prompt(qkv, gates_t, kc_p, vc_p, jnp.pad(o_s.astype(BF16), ((M_PROMPT, 0), (0, 0))))

    h = _proj_norm_res(o_all, nsa_w_out.astype(BF16), 0, _row(norm_mix_post[0]), h, 2, "nsa_out_proj")
    h = residual_tail(h, 0)

    w_pw1 = conv_w_pw1.astype(BF16)
    u = _dual_proj(h, _row(norm_mix_pre[1]), w_pw1, w_pw1, 0, D_MODEL, D_MODEL, _row(conv_b_pw1[0]), "glu", F32,
                   "conv_pw1_glu")
    wdw = jnp.pad(conv_w_dw[0], ((0, HALO - CONV_W), (0, 0)))
    conv_args = (wdw, _row(conv_b_dw[0]), _row(conv_ln_g[0]), _row(conv_ln_b[0]), conv_w_pw2.astype(BF16),
                 _row(conv_b_pw2[0]), _row(norm_mix_post[1]))
    t_p = 256
    state = jnp.pad(state_conv[0], ((0, 0), (HALO - (CONV_W - 1), 0), (0, 0))).reshape(DEC_BATCH * HALO, D_MODEL)
    h_s = _conv_tail(u, state, *conv_args, h, t=DEC_SEQ, n_tiles=DEC_BATCH, row0=M_PROMPT, tiles_per_seq=1,
                     halo_is_state=True)
    h = _conv_tail(u, u, *conv_args, h, t=t_p, n_tiles=M_PROMPT // t_p, row0=0, tiles_per_seq=SEQ // t_p,
                   halo_is_state=False, out_init=jnp.pad(h_s, ((M_PROMPT, 0), (0, 0))))
    y_p, y_s = residual_tail(h, 1, split=True)

    y_prompt = y_p.reshape(BATCH, SEQ, D_MODEL)
    y_sample = y_s[-M_SAMPLE:].reshape(DEC_BATCH, DEC_SEQ, D_MODEL)
    kv_p = [a.reshape(1, BATCH, SEQ, N_KV, HEAD_DIM) for a in kvp]
    outs = [y_prompt, y_sample]
    for c in range(4):
        outs.append(kv_p[c])
        outs.append(kv_s[c][None])
    w_keep = min(WINDOW, SEQ)
    outs.append(kv_p[4][:, :, SEQ - w_keep:])
    outs.append(kw_all[None, :, -WINDOW:])
    outs.append(kv_p[5][:, :, SEQ - w_keep:])
    outs.append(vw_all[None, :, -WINDOW:])
    n_keep = CONV_W - 1
    outs.append(jnp.stack([u[b * SEQ + SEQ - n_keep:(b + 1) * SEQ] for b in range(BATCH)])[None])
    u_s = u[M_PROMPT:].reshape(DEC_BATCH, DEC_SEQ, D_MODEL)
    outs.append(jnp.concatenate([state_conv[0], u_s], axis=1)[None, :, -n_keep:])
    return tuple(outs)
```

```python
import functools
import math

import numpy as np
import jax
import jax.numpy as jnp
from jax import lax
from jax.experimental import pallas as pl
from jax.experimental.pallas import tpu as pltpu

F32 = jnp.float32
BF16 = jnp.bfloat16

D_MODEL = 2048
BATCH = 2
SEQ = 4096
DEPTH = 2
DEC_BATCH = 8
DEC_SEQ = 8
PAST_LEN = 16384
PAGE_SIZE = 128
N_HEADS = 16
HEAD_DIM = 128
N_KV = 4
GQA_R = 4
Q_DIM = 2048
KV_DIM = 512
CMP_BLOCK = 32
CMP_STRIDE = 16
SEL_BLOCK = 64
TOPK = 16
WINDOW = 512
ROPE_THETA = 10000.0
CONV_W = 31
D_FF = 5632
PLE_DIM = 256
RMS_EPS = 1e-6
LN_EPS = 1e-5
NEG = -1e30

M_PROMPT = BATCH * SEQ
M_SAMPLE = DEC_BATCH * DEC_SEQ
M_ALL = M_PROMPT + M_SAMPLE
N_PAGES = PAST_LEN // PAGE_SIZE
PAGE_ROWS = PAGE_SIZE * N_KV
CHUNKS_PER_PAGE = PAGE_SIZE // CMP_STRIDE
PAGES_PER_STEP = 16

TM = 688
TM_ROWS = 688
TN = 512
TK_ATT = 1024
TQ_PROMPT = 128
ATT_HEADS_PER_STEP = 4
WBAND_PROMPT = WINDOW + TQ_PROMPT
WBAND_SAMPLE = WINDOW + SEL_BLOCK
VMEM_LIMIT = 56 * 1024 * 1024
VMEM_LIMIT_ATTN = 60 * 1024 * 1024

Q_PRESCALE = HEAD_DIM ** -0.5 * math.log2(math.e)


def _cparams(sem, vmem_limit=VMEM_LIMIT):
    return pltpu.CompilerParams(dimension_semantics=sem, vmem_limit_bytes=vmem_limit)


def _rms_scale(x, g):
    ms = jnp.mean(x * x, axis=-1, keepdims=True)
    return x * lax.rsqrt(ms + RMS_EPS) * g


def _dot(a, b):
    return jnp.dot(a, b, preferred_element_type=F32)


def _dot_nt(a, b):
    return lax.dot_general(a, b, (((1,), (1,)), ((), ())), preferred_element_type=F32)


N_QKV_TILES = (Q_DIM + 6 * KV_DIM) // TN
N_Q_TILES = Q_DIM // TN


def _in_proj_kernel(x_ref, g_ref, w_ref, wg_ref, cos_ref, sin_ref, qkv_ref, *rest):
    kvp_refs, (kvs_ref, gate_ref, xn_ref) = rest[:6], rest[6:]
    j = pl.program_id(1)

    @pl.when(j == 0)
    def _():
        xn = _rms_scale(x_ref[...], g_ref[...]).astype(BF16)
        xn_ref[...] = xn
        gate_ref[...] = jax.nn.sigmoid(_dot(xn, wg_ref[...]))

    acc = _dot(xn_ref[...], w_ref[...])
    is_rope = (j < N_Q_TILES + 1) | (j == N_Q_TILES + 2) | (j == N_Q_TILES + 4)
    post = jnp.where(j < N_Q_TILES, Q_PRESCALE, 1.0)
    cos = jnp.where(is_rope, cos_ref[...], 1.0) * post
    sin = jnp.where(is_rope, sin_ref[...], 0.0) * post
    parts = []
    for h in range(TN // HEAD_DIM):
        a = acc[:, h * HEAD_DIM:(h + 1) * HEAD_DIM]
        parts.append(a * cos + pltpu.roll(a, HEAD_DIM // 2, axis=1) * sin)
    qkv_ref[...] = jnp.concatenate(parts, axis=1).astype(BF16)

    @pl.when(j >= N_Q_TILES)
    def _():
        for g in range(N_KV):
            kvs_ref[pl.ds(g, M_SAMPLE, stride=N_KV), :] = parts[g][TM - M_SAMPLE:]

    for c in range(6):
        @pl.when(j == N_Q_TILES + c)
        def _(c=c):
            for g in range(N_KV):
                kvp_refs[c][pl.ds(g, TM, stride=N_KV), :] = parts[g]


def _in_proj(x, g, w, wg, cos, sin):
    m = x.shape[0]
    assert m == M_ALL and (m // TM - 1) * TM + (TM - M_SAMPLE) == M_PROMPT
    return pl.pallas_call(
        _in_proj_kernel,
        grid=(m // TM, N_QKV_TILES),
        in_specs=[
            pl.BlockSpec((TM, D_MODEL), lambda i, j: (i, 0)),
            pl.BlockSpec((1, D_MODEL), lambda i, j: (0, 0)),
            pl.BlockSpec((None, D_MODEL, TN), lambda i, j: (0, 0, j)),
            pl.BlockSpec((D_MODEL, HEAD_DIM), lambda i, j: (0, 0)),
            pl.BlockSpec((TM, HEAD_DIM), lambda i, j: (i, 0)),
            pl.BlockSpec((TM, HEAD_DIM), lambda i, j: (i, 0)),
        ],
        out_specs=[
            pl.BlockSpec((TM, TN), lambda i, j: (i, j)),
        ] + [pl.BlockSpec((TM * N_KV, HEAD_DIM), lambda i, j: (i, 0))] * 6 + [
            pl.BlockSpec((None, M_SAMPLE * N_KV, HEAD_DIM), lambda i, j: (jnp.maximum(j - N_Q_TILES, 0), i, 0)),
            pl.BlockSpec((TM, HEAD_DIM), lambda i, j: (i, 0)),
        ],
        out_shape=[
            jax.ShapeDtypeStruct((m, Q_DIM + 6 * KV_DIM), BF16),
        ] + [jax.ShapeDtypeStruct((M_PROMPT * N_KV, HEAD_DIM), F32)] * 6 + [
            jax.ShapeDtypeStruct((6, (m // TM) * M_SAMPLE * N_KV, HEAD_DIM), F32),
            jax.ShapeDtypeStruct((m, HEAD_DIM), F32),
        ],
        scratch_shapes=[pltpu.VMEM((TM, D_MODEL), BF16)],
        compiler_params=_cparams(("parallel", "arbitrary")),
        name="nsa_in_proj",
    )(x, g, w, wg, cos, sin)


def _dual_kernel(*refs, mode, has_bias):
    if has_bias:
        x_ref, g_ref, wa_ref, wb_ref, ba_ref, bb_ref, o_ref, xn_ref = refs
    else:
        x_ref, g_ref, wa_ref, wb_ref, o_ref, xn_ref = refs

    @pl.when(pl.program_id(1) == 0)
    def _():
        xn_ref[...] = _rms_scale(x_ref[...], g_ref[...]).astype(BF16)

    xn = xn_ref[...]
    a = _dot(xn, wa_ref[...])
    b = _dot(xn, wb_ref[...])
    if has_bias:
        a = a + ba_ref[...]
        b = b + bb_ref[...]
    if mode == "swiglu":
        o = jax.nn.silu(a) * b
    else:
        o = a * jax.nn.sigmoid(b)
    o_ref[...] = o.astype(o_ref.dtype)


def _dual_proj(x, g, wa, wb, layer, n_out, b_col0, bias, mode, out_dtype, name):
    m = x.shape[0]
    nb = b_col0 // TN
    in_specs = [
        pl.BlockSpec((TM, D_MODEL), lambda i, j: (i, 0)),
        pl.BlockSpec((1, D_MODEL), lambda i, j: (0, 0)),
        pl.BlockSpec((None, D_MODEL, TN), lambda i, j: (layer, 0, j)),
        pl.BlockSpec((None, D_MODEL, TN), lambda i, j: (layer, 0, j + nb)),
    ]
    args = [x, g, wa, wb]
    if bias is not None:
        in_specs += [pl.BlockSpec((1, TN), lambda i, j: (0, j)), pl.BlockSpec((1, TN), lambda i, j: (0, j + nb))]
        args += [bias, bias]
    return pl.pallas_call(
        functools.partial(_dual_kernel, mode=mode, has_bias=bias is not None),
        grid=(m // TM, n_out // TN),
        in_specs=in_specs,
        out_specs=pl.BlockSpec((TM, TN), lambda i, j: (i, j)),
        out_shape=jax.ShapeDtypeStruct((m, n_out), out_dtype),
        scratch_shapes=[pltpu.VMEM((TM, D_MODEL), BF16)],
        compiler_params=_cparams(("parallel", "arbitrary")),
        name=name,
    )(*args)


def _proj_norm_res_kernel(a_ref, w_ref, g_ref, h_ref, o_ref, acc_ref, *, nk):
    k = pl.program_id(1)
    part = _dot(a_ref[...], w_ref[...])

    @pl.when(k == 0)
    def _():
        acc_ref[...] = part

    @pl.when(k > 0)
    def _():
        acc_ref[...] += part

    @pl.when(k == nk - 1)
    def _():
        o_ref[...] = h_ref[...] + _rms_scale(acc_ref[...], g_ref[...])


def _proj_norm_res(a, w, layer, g, h, nk, name):
    m, kdim = a.shape
    tk = kdim // nk
    return pl.pallas_call(
        functools.partial(_proj_norm_res_kernel, nk=nk),
        grid=(m // TM_ROWS, nk),
        in_specs=[
            pl.BlockSpec((TM_ROWS, tk), lambda i, k: (i, k)),
            pl.BlockSpec((None, tk, D_MODEL), lambda i, k: (layer, k, 0)),
            pl.BlockSpec((1, D_MODEL), lambda i, k: (0, 0)),
            pl.BlockSpec((TM_ROWS, D_MODEL), lambda i, k: (i, 0)),
        ],
        out_specs=pl.BlockSpec((TM_ROWS, D_MODEL), lambda i, k: (i, 0)),
        out_shape=jax.ShapeDtypeStruct((m, D_MODEL), F32),
        scratch_shapes=[pltpu.VMEM((TM_ROWS, D_MODEL), F32)],
        compiler_params=_cparams(("parallel", "arbitrary")),
        name=name,
    )(a, w, g, h)


def _ple_kernel(h_ref, hc_ref, p_ref, wg_ref, wp_ref, o_ref, *rest, split):
    hb_ref = rest[-1]

    @pl.when(pl.program_id(1) == 0)
    def _():
        hb_ref[...] = h_ref[...].astype(BF16)

    gate = jax.nn.sigmoid(_dot(hb_ref[...], wg_ref[...]))
    out = hc_ref[...] + gate * _dot(p_ref[...], wp_ref[...])
    o_ref[...] = out
    if split:
        rest[0][...] = out[TM - M_SAMPLE:]


def _ple(h, p, wg, wp, layer, name, split=False):
    m = h.shape[0]
    out_specs = pl.BlockSpec((TM, TN), lambda i, j: (i, j))
    out_shape = jax.ShapeDtypeStruct((m, D_MODEL), F32)
    if split:
        assert (m // TM - 1) * TM + (TM - M_SAMPLE) == M_PROMPT
        out_specs = [out_specs, pl.BlockSpec((M_SAMPLE, TN), lambda i, j: (i, j))]
        out_shape = [jax.ShapeDtypeStruct((M_PROMPT, D_MODEL), F32),
                     jax.ShapeDtypeStruct(((m // TM) * M_SAMPLE, D_MODEL), F32)]
    return pl.pallas_call(
        functools.partial(_ple_kernel, split=split),
        grid=(m // TM, D_MODEL // TN),
        in_specs=[
            pl.BlockSpec((TM, D_MODEL), lambda i, j: (i, 0)),
            pl.BlockSpec((TM, TN), lambda i, j: (i, j)),
            pl.BlockSpec((None, TM, PLE_DIM), lambda i, j: (layer, i, 0)),
            pl.BlockSpec((None, D_MODEL, TN), lambda i, j: (layer, 0, j)),
            pl.BlockSpec((None, PLE_DIM, TN), lambda i, j: (layer, 0, j)),
        ],
        out_specs=out_specs,
        out_shape=out_shape,
        scratch_shapes=[pltpu.VMEM((TM, D_MODEL), BF16)],
        compiler_params=_cparams(("parallel", "arbitrary")),
        name=name,
    )(h, h, p, wg, wp)


TILES_PER_CHUNK = CMP_STRIDE * N_KV // 8


def _compress_kernel(pt_ref, *refs, n):
    k_refs, v_refs = refs[0:n], refs[n:2 * n]
    wk_ref, wv_ref, pk0_ref, pk1_ref, pv0_ref, pv1_ref = refs[2 * n:2 * n + 6]
    for srcs, w_ref, o0_ref, o1_ref in ((k_refs, wk_ref, pk0_ref, pk1_ref), (v_refs, wv_ref, pv0_ref, pv1_ref)):
        acc = None
        for jj in range(TILES_PER_CHUNK // 2):
            halves = []
            for j in (2 * jj, 2 * jj + 1):
                rows = [srcs[p][(c * TILES_PER_CHUNK + j) * 8:(c * TILES_PER_CHUNK + j + 1) * 8, :]
                        for p in range(n) for c in range(CHUNKS_PER_PAGE)]
                halves.append(jnp.concatenate(rows, axis=0))
            term = _dot(jnp.concatenate(halves, axis=1).astype(BF16), w_ref[jj])
            acc = term if acc is None else acc + term
        width = 2 * HEAD_DIM
        folded = acc[:, :width] + pltpu.roll(acc[:, width:], acc.shape[0] - N_KV, axis=0)
        o0_ref[0] = folded[:, :HEAD_DIM]
        o1_ref[0] = folded[:, HEAD_DIM:]


def _compress_stage1(page_table, k_src, k_type, v_src, v_type, wk, wv, n_seq, pages_per_seq, name):
    n = min(PAGES_PER_STEP, pages_per_seq)
    groups = pages_per_seq // n
    n_chunks = pages_per_seq * CHUNKS_PER_PAGE

    def src_spec(slot, typ):
        return pl.BlockSpec(
            (None, PAGE_ROWS, HEAD_DIM), lambda b, t, pt: (typ, pt[b * pages_per_seq + t * n + slot], 0))

    w_spec = pl.BlockSpec(wk.shape, lambda b, t, pt: (0, 0, 0))
    out_spec = pl.BlockSpec((1, n * CHUNKS_PER_PAGE * 8, HEAD_DIM), lambda b, t, pt: (b, t, 0))
    out_sds = jax.ShapeDtypeStruct((n_seq, n_chunks * 8, HEAD_DIM), F32)
    return pl.pallas_call(
        functools.partial(_compress_kernel, n=n),
        grid_spec=pltpu.PrefetchScalarGridSpec(
            num_scalar_prefetch=1,
            grid=(n_seq, groups),
            in_specs=[src_spec(s, k_type) for s in range(n)] + [src_spec(s, v_type) for s in range(n)]
            + [w_spec, w_spec],
            out_specs=[out_spec] * 4,
        ),
        out_shape=[out_sds] * 4,
        compiler_params=_cparams(("parallel", "arbitrary")),
        name=name,
    )(page_table, *([k_src] * n), *([v_src] * n), wk, wv)


def _compress_finish_kernel(p0_ref, p1_ref, pn_ref, pe_ref, w1_ref, w2_ref, o_ref):
    c = o_ref.shape[2]
    bias = _dot(pe_ref[...].astype(BF16), w1_ref[...])[0:1]
    row = lax.broadcasted_iota(jnp.int32, (c, 1), 0)
    for g in range(N_KV):
        p0 = p0_ref[0, pl.ds(g, c, stride=8), :]
        p1 = p1_ref[0, pl.ds(g, c, stride=8), :]
        nxt = pltpu.roll(p1, c - 1, axis=0)
        nxt = jnp.where(row == c - 1, pn_ref[g:g + 1, :], nxt)
        hid = p0 + nxt + bias
        o_ref[0, g] = _dot(jax.nn.silu(hid).astype(BF16), w2_ref[...]).astype(o_ref.dtype)


def _compress_finish(p0, p1, p1_next, pe8, w1, w2, name):
    n_seq, rows, _ = p0.shape
    c = rows // 8
    seq_spec = pl.BlockSpec((1, rows, HEAD_DIM), lambda b: (b, 0, 0))
    return pl.pallas_call(
        _compress_finish_kernel,
        grid=(n_seq,),
        in_specs=[
            seq_spec, seq_spec,
            pl.BlockSpec((8, HEAD_DIM), lambda b: (b, 0)),
            pl.BlockSpec((8, CMP_BLOCK * HEAD_DIM), lambda b: (0, 0)),
            pl.BlockSpec((CMP_BLOCK * HEAD_DIM, HEAD_DIM), lambda b: (0, 0)),
            pl.BlockSpec((HEAD_DIM, HEAD_DIM), lambda b: (0, 0)),
        ],
        out_specs=pl.BlockSpec((1, N_KV, c, HEAD_DIM), lambda b: (b, 0, 0, 0)),
        out_shape=jax.ShapeDtypeStruct((n_seq, N_KV, c, HEAD_DIM), BF16),
        compiler_params=_cparams(("parallel",)),
        name=name,
    )(p0, p1, p1_next, pe8, w1, w2)


def _masked_softmax2(s, mask):
    s = jnp.where(mask, s, NEG)
    m = jnp.max(s, axis=-1, keepdims=True)
    e = jnp.where(mask, jnp.exp2(s - m), 0.0)
    return e / jnp.maximum(jnp.sum(e, axis=-1, keepdims=True), 1e-30)


def _split3(x):
    parts = []
    rem = x
    for _ in range(3):
        part = rem.astype(BF16)
        parts.append(part)
        rem = rem - part.astype(F32)
    return parts


def _sel_tile(kt, carry, qa, t_row, ks, vs, e_ref, e_row0, tk, masked, lanes=slice(0, HEAD_DIM)):
    m, l, acc = carry
    r0 = pl.multiple_of(kt * tk, tk)
    e0 = pl.multiple_of(kt * tk - e_row0, tk)
    ka = jnp.concatenate([ks[pl.ds(r0, tk), lanes], e_ref[pl.ds(e0, tk), :]], axis=1)
    sc = _dot_nt(qa, ka)
    if masked:
        kpos = r0 + lax.broadcasted_iota(jnp.int32, (1, tk), 1)
        sc = jnp.where(kpos <= t_row, sc, NEG)
    m_new = jnp.maximum(m, jnp.max(sc, axis=-1, keepdims=True))
    alpha = jnp.exp2(m - m_new)
    p = jnp.exp2(sc - m_new)
    l = alpha * l + jnp.sum(p, axis=-1, keepdims=True)
    acc = alpha * acc + _dot(p.astype(BF16), vs[pl.ds(r0, tk), lanes])
    return m_new, l, acc


def _sel_init(rows):
    return (jnp.full((rows, 1), -jnp.inf, F32), jnp.zeros((rows, 1), F32), jnp.zeros((rows, HEAD_DIM), F32))


def _gated_merge(gt, o_c, o_s, o_w, tq):
    outs = []
    for r in range(GQA_R):
        sl = slice(r * tq, (r + 1) * tq)
        outs.append(gt[:, r:r + 1] * o_c[sl] + gt[:, GQA_R + r:GQA_R + r + 1] * o_s[sl]
                    + gt[:, 2 * GQA_R + r:2 * GQA_R + r + 1] * o_w[sl])
    return jnp.concatenate(outs, axis=1)


def _cmp_to_sel_matrix(nb, nsp):
    n = np.arange(nb)[:, None]
    j = np.arange(nsp)[None, :]
    return ((n >= 4 * j) & (n <= 4 * j + 3)).astype(np.float32) + ((n >= 4 * j - 1) & (n <= 4 * j + 2)).astype(np.float32)


def _block_onehot(n_keys):
    k = np.arange(n_keys)[:, None] // SEL_BLOCK
    return jnp.asarray((k % 128 == np.arange(128)[None, :]).astype(np.float32), BF16)


def _window_bias_prompt():
    q = np.arange(TQ_PROMPT)[:, None]
    kb = np.arange(WBAND_PROMPT)[None, :]
    pats = [kb <= TQ_PROMPT * i + q for i in range(WINDOW // TQ_PROMPT)]
    pats.append((kb > q) & (kb <= q + WINDOW))
    return jnp.asarray(np.where(np.stack(pats), 0.0, NEG), F32)


def _attn_prompt_kernel(q_ref, gate_ref, kc_ref, vc_ref, ks_ref, vs_ref, kw_ref, vw_ref, mmt_ref, e_ref, wb_ref,
                        o_init_ref, o_ref, *, nb, ns):
    del o_init_ref
    tq = TQ_PROMPT
    rows = GQA_R * tq
    qi = pl.program_id(2)
    t0 = qi * tq
    t_row = t0 + (lax.broadcasted_iota(jnp.int32, (rows, 1), 0) & (tq - 1))
    cmp_end = lax.broadcasted_iota(jnp.int32, (1, nb), 1) * CMP_STRIDE + (CMP_BLOCK - 1)
    jb = lax.broadcasted_iota(jnp.int32, (ns, tq), 0)
    cur = lax.shift_right_logical(t0 + lax.broadcasted_iota(jnp.int32, (1, tq), 1), 6)
    forced = (jb == 0) | (jb == cur) | (jb == cur - 1)
    eligible = jb <= cur
    groups = ns // 8
    jgrp = [jb[8 * v:8 * v + 8] for v in range(groups)]
    eye = jnp.where(lax.broadcasted_iota(jnp.int32, (tq, tq), 0) == lax.broadcasted_iota(jnp.int32, (tq, tq), 1),
                    1.0, 0.0).astype(BF16)
    mmt = mmt_ref[...]

    heads = range(ATT_HEADS_PER_STEP)
    qs, o_cs, qas = [], [], []
    for h in heads:
        qq = q_ref[:, h * GQA_R * HEAD_DIM:(h + 1) * GQA_R * HEAD_DIM]
        q = jnp.concatenate([qq[:, r * HEAD_DIM:(r + 1) * HEAD_DIM] for r in range(GQA_R)], axis=0)

        s = _dot_nt(q, kc_ref[0, h])
        p_c = _masked_softmax2(s, cmp_end <= t_row)
        o_cs.append(_dot(p_c.astype(BF16), vc_ref[0, h]))
        imp = p_c[0:tq]
        for r in range(1, GQA_R):
            imp = imp + p_c[r * tq:(r + 1) * tq]

        blk_imp = None
        for part in _split3(imp):
            term = _dot_nt(mmt, part)
            blk_imp = term if blk_imp is None else blk_imp + term
        score = jnp.where(forced, jnp.inf, jnp.where(eligible, blk_imp, -jnp.inf))
        sgrp = [score[8 * v:8 * v + 8] for v in range(groups)]
        rank = [jnp.zeros((8, tq), F32) for _ in range(groups)]
        for i in range(ns):
            ci = score[i:i + 1]
            for v in range(groups):
                if 8 * v > i:
                    beats = ci >= sgrp[v]
                elif 8 * v + 7 <= i:
                    beats = ci > sgrp[v]
                else:
                    beats = (ci > sgrp[v]) | ((ci == sgrp[v]) & (jgrp[v] > i))
                rank[v] = rank[v] + jnp.where(beats, 1.0, 0.0)
        rank = jnp.concatenate(rank, axis=0)
        sel_t = jnp.where((rank < float(TOPK)) & eligible, 1.0, 0.0)
        sel_t = jnp.concatenate([sel_t, jnp.zeros((128 - ns, tq), F32)], axis=0).astype(BF16)
        sel = _dot_nt(eye, sel_t)
        negsel = jnp.where(sel > 0.5, 0.0, NEG).astype(BF16)
        qs.append(q)
        qas.append(jnp.concatenate([q, jnp.concatenate([negsel] * GQA_R, axis=0)], axis=1))

    n_tiles = lax.shift_right_logical(t0 + tq - 1, TK_ATT.bit_length() - 1) + 1

    def tiles(kt, carries, masked):
        return tuple(
            _sel_tile(kt, carries[h], qa=qas[h], t_row=t_row, ks=ks_ref, vs=vs_ref, e_ref=e_ref, e_row0=0,
                      tk=TK_ATT, masked=masked, lanes=slice(h * HEAD_DIM, (h + 1) * HEAD_DIM))
            for h in heads)

    carries = lax.fori_loop(0, n_tiles - 1, functools.partial(tiles, masked=False),
                            tuple(_sel_init(rows) for _ in heads))
    carries = tiles(n_tiles - 1, carries, masked=True)

    wstart = pl.multiple_of(jnp.maximum(t0 - WINDOW, 0), tq)
    bias = wb_ref[jnp.minimum(qi, WINDOW // tq)]
    bias = jnp.concatenate([bias] * GQA_R, axis=0)
    for h in heads:
        lanes = slice(h * HEAD_DIM, (h + 1) * HEAD_DIM)
        _, l, acc = carries[h]
        o_s = acc / jnp.maximum(l, 1e-30)
        s = _dot_nt(qs[h], kw_ref[pl.ds(wstart, WBAND_PROMPT), lanes]) + bias
        e = jnp.exp2(s - jnp.max(s, axis=-1, keepdims=True))
        p_w = e / jnp.sum(e, axis=-1, keepdims=True)
        o_w = _dot(p_w.astype(BF16), vw_ref[pl.ds(wstart, WBAND_PROMPT), lanes])
        o_ref[:, h * GQA_R * HEAD_DIM:(h + 1) * GQA_R * HEAD_DIM] = _gated_merge(
            gate_ref[h], o_cs[h], o_s, o_w, tq).astype(o_ref.dtype)


def _attn_prompt(qkv, gates_t, kc, vc, o_init):
    tq = TQ_PROMPT
    hp = ATT_HEADS_PER_STEP
    nq = SEQ // tq
    nb = SEQ // CMP_STRIDE
    ns = SEQ // SEL_BLOCK
    k_sel0, v_sel0 = Q_DIM + 2 * KV_DIM, Q_DIM + 3 * KV_DIM
    k_win0, v_win0 = Q_DIM + 4 * KV_DIM, Q_DIM + 5 * KV_DIM
    seq_spec = lambda base: pl.BlockSpec((SEQ, hp * HEAD_DIM), lambda b, g, i: (b, base // (hp * HEAD_DIM) + g))
    blk_spec = pl.BlockSpec((1, hp, nb, HEAD_DIM), lambda b, g, i: (b, g, 0, 0))
    mmt = jnp.asarray(_cmp_to_sel_matrix(nb, ns).T, BF16)
    wbias = _window_bias_prompt()
    return pl.pallas_call(
        functools.partial(_attn_prompt_kernel, nb=nb, ns=ns),
        grid=(BATCH, N_KV // hp, nq),
        in_specs=[
            pl.BlockSpec((tq, hp * GQA_R * HEAD_DIM), lambda b, g, i: (b * nq + i, g)),
            pl.BlockSpec((hp, tq, HEAD_DIM), lambda b, g, i: (g, b * nq + i, 0)),
            blk_spec, blk_spec,
            seq_spec(k_sel0), seq_spec(v_sel0), seq_spec(k_win0), seq_spec(v_win0),
            pl.BlockSpec((ns, nb), lambda b, g, i: (0, 0)),
            pl.BlockSpec((SEQ, 128), lambda b, g, i: (0, 0)),
            pl.BlockSpec(wbias.shape, lambda b, g, i: (0, 0, 0)),
            pl.BlockSpec(memory_space=pl.ANY),
        ],
        out_specs=pl.BlockSpec((tq, hp * GQA_R * HEAD_DIM), lambda b, g, i: (b * nq + i, g)),
        out_shape=jax.ShapeDtypeStruct((M_ALL, Q_DIM), BF16),
        input_output_aliases={11: 0},
        compiler_params=_cparams(("parallel", "parallel", "arbitrary"), VMEM_LIMIT_ATTN),
        name="nsa_attn_prompt",
    )(qkv, gates_t, kc, vc, qkv, qkv, qkv, qkv, mmt, _block_onehot(SEQ), wbias, o_init)


SAMPLE_NB = PAST_LEN // CMP_STRIDE
SAMPLE_NS = -(-(PAST_LEN + DEC_SEQ) // SEL_BLOCK)
SAMPLE_NSP = 384
SAMPLE_PAGES_PER_STEP = 16
KEYS_PER_STEP = SAMPLE_PAGES_PER_STEP * PAGE_SIZE
BLOCKS_PER_STEP = KEYS_PER_STEP // SEL_BLOCK
SAMPLE_STEPS = N_PAGES // SAMPLE_PAGES_PER_STEP


def _attn_sample_kernel(pt_ref, q_ref, gate_ref, kc_ref, vc_ref, kw_ref, vw_ref, mmat_ref, e_ref, *refs):
    n = SAMPLE_PAGES_PER_STEP
    k_refs, v_refs = refs[0:n], refs[n:2 * n]
    kn_ref, vn_ref, o_ref, qa_ref, m_ref, l_ref, acc_ref, oc_ref, ow_ref = refs[2 * n:]
    tq, nb, ns, nsp = DEC_SEQ, SAMPLE_NB, SAMPLE_NS, SAMPLE_NSP
    rows = GQA_R * tq
    t0 = PAST_LEN
    t = pl.program_id(1)
    t_row = t0 + (lax.broadcasted_iota(jnp.int32, (rows, 1), 0) & (tq - 1))

    @pl.when(t == 0)
    def _():
        cmp_end = lax.broadcasted_iota(jnp.int32, (1, nb), 1) * CMP_STRIDE + (CMP_BLOCK - 1)
        jb = lax.broadcasted_iota(jnp.int32, (tq, nsp), 1)
        cur = lax.shift_right_logical(t0 + lax.broadcasted_iota(jnp.int32, (tq, 1), 0), 6)
        forced = (jb == 0) | (jb == cur) | (jb == cur - 1)
        eligible = jb <= cur
        kpos = (PAST_LEN - WINDOW) + lax.broadcasted_iota(jnp.int32, (1, WBAND_SAMPLE), 1)
        win_mask = (kpos <= t_row) & (kpos > t_row - WINDOW)
        mmat = mmat_ref[...]
        for g in range(N_KV):
            q = q_ref[0, g]

            p_c = _masked_softmax2(_dot_nt(q, kc_ref[0, g]), cmp_end <= t_row)
            oc_ref[g] = _dot(p_c.astype(BF16), vc_ref[0, g])
            imp = p_c[0:tq]
            for r in range(1, GQA_R):
                imp = imp + p_c[r * tq:(r + 1) * tq]

            blk_imp = None
            for part in _split3(imp):
                term = _dot(part, mmat)
                blk_imp = term if blk_imp is None else blk_imp + term
            score = jnp.where(forced, jnp.inf, jnp.where(eligible, blk_imp, -jnp.inf))
            rank = jnp.zeros((tq, nsp), F32)
            for i in range(ns):
                ci = score[:, i:i + 1]
                beats = (ci > score) | ((ci == score) & (jb > i))
                rank = rank + jnp.where(beats, 1.0, 0.0)
            negsel = jnp.where((rank < float(TOPK)) & eligible, 0.0, NEG)
            negsel = jnp.concatenate([negsel] * GQA_R, axis=0).astype(BF16)
            pad = jnp.zeros((rows, HEAD_DIM - BLOCKS_PER_STEP), BF16)
            for st in range(SAMPLE_STEPS + 1):
                piece = negsel[:, st * BLOCKS_PER_STEP:(st + 1) * BLOCKS_PER_STEP]
                qa_ref[g, st] = jnp.concatenate([q, piece, pad], axis=1)

            p_w = _masked_softmax2(_dot_nt(q, kw_ref[0, g]), win_mask)
            ow_ref[g] = _dot(p_w.astype(BF16), vw_ref[0, g])

            m_ref[g] = jnp.full((rows, 1), -jnp.inf, F32)
            l_ref[g] = jnp.zeros((rows, 1), F32)
            acc_ref[g] = jnp.zeros((rows, HEAD_DIM), F32)

    def update(g, qa, ka, v, mask):
        sc = _dot_nt(qa, ka)
        if mask is not None:
            sc = jnp.where(mask, sc, NEG)
        m = m_ref[g]
        m_new = jnp.maximum(m, jnp.max(sc, axis=-1, keepdims=True))
        alpha = jnp.exp2(m - m_new)
        p = jnp.exp2(sc - m_new)
        l_ref[g] = alpha * l_ref[g] + jnp.sum(p, axis=-1, keepdims=True)
        acc_ref[g] = alpha * acc_ref[g] + _dot(p.astype(BF16), v)
        m_ref[g] = m_new

    def head_rows(page_refs, g):
        return jnp.concatenate([r[pl.ds(g, PAGE_SIZE, stride=N_KV), :] for r in page_refs], axis=0).astype(BF16)

    e_blk = e_ref[...]
    for g in range(N_KV):
        ka = jnp.concatenate([head_rows(k_refs, g), e_blk], axis=1)
        update(g, qa_ref[g, t], ka, head_rows(v_refs, g), None)

    @pl.when(t == SAMPLE_STEPS - 1)
    def _():
        kpos = t0 + lax.broadcasted_iota(jnp.int32, (1, PAGE_SIZE), 1)
        gt = gate_ref[...]
        for g in range(N_KV):
            ka = jnp.concatenate([head_rows([kn_ref], g), e_blk[0:PAGE_SIZE]], axis=1)
            update(g, qa_ref[g, SAMPLE_STEPS], ka, head_rows([vn_ref], g), kpos <= t_row)
            o_s = acc_ref[g] / jnp.maximum(l_ref[g], 1e-30)
            o_ref[:, g * GQA_R * HEAD_DIM:(g + 1) * GQA_R * HEAD_DIM] = _gated_merge(
                gt[g], oc_ref[g], o_s, ow_ref[g], tq)


def _attn_sample(page_table, q_s, gates_s, kc, vc, kw, vw, k_pool, v_pool, k_new, v_new):
    n = SAMPLE_PAGES_PER_STEP
    rows = GQA_R * DEC_SEQ
    seq_spec = lambda r: pl.BlockSpec((1, N_KV, r, HEAD_DIM), lambda b, t, pt: (b, 0, 0, 0))
    page = (PAGE_ROWS, HEAD_DIM)
    pool_spec = lambda slot: pl.BlockSpec(page, lambda b, t, pt: (pt[b * N_PAGES + t * n + slot], 0))
    new_spec = pl.BlockSpec(page, lambda b, t, pt: (b, 0))
    mmat = jnp.asarray(_cmp_to_sel_matrix(SAMPLE_NB, SAMPLE_NSP), BF16)
    return pl.pallas_call(
        _attn_sample_kernel,
        grid_spec=pltpu.PrefetchScalarGridSpec(
            num_scalar_prefetch=1,
            grid=(DEC_BATCH, SAMPLE_STEPS),
            in_specs=[
                seq_spec(rows),
                pl.BlockSpec((N_KV, DEC_SEQ, HEAD_DIM), lambda b, t, pt: (0, b, 0)),
                seq_spec(SAMPLE_NB), seq_spec(SAMPLE_NB),
                seq_spec(WBAND_SAMPLE), seq_spec(WBAND_SAMPLE),
                pl.BlockSpec((SAMPLE_NB, SAMPLE_NSP), lambda b, t, pt: (0, 0)),
                pl.BlockSpec((KEYS_PER_STEP, HEAD_DIM), lambda b, t, pt: (0, 0)),
            ] + [pool_spec(s) for s in range(n)] * 2 + [new_spec, new_spec],
            out_specs=pl.BlockSpec((DEC_SEQ, Q_DIM), lambda b, t, pt: (b, 0)),
            scratch_shapes=[
                pltpu.VMEM((N_KV, SAMPLE_STEPS + 1, rows, 2 * HEAD_DIM), BF16),
                pltpu.VMEM((N_KV, rows, 1), F32),
                pltpu.VMEM((N_KV, rows, 1), F32),
                pltpu.VMEM((N_KV, rows, HEAD_DIM), F32),
                pltpu.VMEM((N_KV, rows, HEAD_DIM), F32),
                pltpu.VMEM((N_KV, rows, HEAD_DIM), F32),
            ],
        ),
        out_shape=jax.ShapeDtypeStruct((M_SAMPLE, Q_DIM), F32),
        compiler_params=_cparams(("parallel", "arbitrary")),
        name="nsa_attn_sample",
    )(page_table, q_s, gates_s, kc, vc, kw, vw, mmat, _block_onehot(KEYS_PER_STEP),
      *([k_pool] * n), *([v_pool] * n), k_new, v_new)


HALO = 32


def _conv_kernel(u_ref, halo_ref, wdw_ref, bdw_ref, lng_ref, lnb_ref, w2_ref, b2_ref, g_ref, h_ref, *rest,
                 t, tiles_per_seq, halo_is_state):
    o_ref, buf_ref, y_ref = rest[-3:]
    i = pl.program_id(0)
    if halo_is_state:
        buf_ref[0:HALO, :] = halo_ref[...]
    else:
        first = (i % tiles_per_seq) == 0
        buf_ref[0:HALO, :] = jnp.where(first, 0.0, halo_ref[...])
    buf_ref[HALO:HALO + t, :] = u_ref[...]
    off = HALO - (CONV_W - 1)
    for c in range(D_MODEL // HEAD_DIM):
        cs = slice(c * HEAD_DIM, (c + 1) * HEAD_DIM)
        base = buf_ref[:, cs]
        w = wdw_ref[:, cs]
        y = bdw_ref[:, cs]
        n_rows = HALO + t
        for s in range(8):
            taps = [k for k in range(CONV_W) if (off + k) % 8 == s]
            shifted = base if s == 0 else pltpu.roll(base, n_rows - s, axis=0)
            for k in taps:
                a = off + k - s
                y = y + shifted[a:a + t] * w[k:k + 1]
        y_ref[:, cs] = y
    y = y_ref[...]
    mu = jnp.mean(y, axis=-1, keepdims=True)
    yc = y - mu
    var = jnp.mean(yc * yc, axis=-1, keepdims=True)
    z = yc * lax.rsqrt(var + LN_EPS) * lng_ref[...] + lnb_ref[...]
    mval = _dot(jax.nn.silu(z).astype(BF16), w2_ref[...]) + b2_ref[...]
    o_ref[...] = h_ref[...] + _rms_scale(mval, g_ref[...])


def _conv_tail(u, halo_src, wdw, bdw, lng, lnb, w2, b2, g, h, *, t, n_tiles, row0, tiles_per_seq, halo_is_state,
               out_init=None):
    blk0 = row0 // t
    extra_specs, extra_args, aliases = [], [], {}
    out_rows = n_tiles * t
    if out_init is not None:
        extra_specs, extra_args, aliases = [pl.BlockSpec(memory_space=pl.ANY)], [out_init], {10: 0}
        out_rows = out_init.shape[0]
    if halo_is_state:
        halo_spec = pl.BlockSpec((HALO, D_MODEL), lambda i: (i, 0))
    else:
        per = t // HALO
        halo_spec = pl.BlockSpec((HALO, D_MODEL), lambda i: (jnp.maximum((blk0 + i) * per - 1, 0), 0))
    vec = pl.BlockSpec((1, D_MODEL), lambda i: (0, 0))
    kern = functools.partial(_conv_kernel, t=t, tiles_per_seq=tiles_per_seq, halo_is_state=halo_is_state)
    return pl.pallas_call(
        kern,
        grid=(n_tiles,),
        in_specs=[
            pl.BlockSpec((t, D_MODEL), lambda i: (blk0 + i, 0)),
            halo_spec,
            pl.BlockSpec((HALO, D_MODEL), lambda i: (0, 0)),
            vec, vec, vec,
            pl.BlockSpec((None, D_MODEL, D_MODEL), lambda i: (0, 0, 0)),
            vec, vec,
            pl.BlockSpec((t, D_MODEL), lambda i: (blk0 + i, 0)),
        ] + extra_specs,
        out_specs=pl.BlockSpec((t, D_MODEL), lambda i: (i, 0)),
        out_shape=jax.ShapeDtypeStruct((out_rows, D_MODEL), F32),
        input_output_aliases=aliases,
        scratch_shapes=[pltpu.VMEM((HALO + t, D_MODEL), F32), pltpu.VMEM((t, D_MODEL), F32)],
        compiler_params=_cparams(("parallel",)),
        name="conv_tail_state" if halo_is_state else "conv_tail",
    )(u, halo_src, wdw, bdw, lng, lnb, w2, b2, g, h, *extra_args)


def _row(v):
    return v.reshape(1, -1).astype(F32)


def _rope_tables():
    half = HEAD_DIM // 2
    pos = np.concatenate([np.tile(np.arange(SEQ), BATCH), np.tile(PAST_LEN + np.arange(DEC_SEQ), DEC_BATCH)])
    inv = (ROPE_THETA ** (-np.arange(half, dtype=np.float64) / half)).astype(np.float32)
    ang = (pos.astype(np.float32)[:, None] * inv[None, :]).astype(np.float64)
    cos, sin = np.cos(ang).astype(np.float32), np.sin(ang).astype(np.float32)
    return jnp.asarray(np.concatenate([cos, cos], axis=1)), jnp.asarray(np.concatenate([-sin, sin], axis=1))


def _cmp_weights(w1):
    w1r = w1.reshape(CMP_BLOCK // CMP_STRIDE, CMP_STRIDE * HEAD_DIM, HEAD_DIM)
    per_pos = jnp.concatenate([w1r[0], w1r[1]], axis=1).reshape(CMP_STRIDE, HEAD_DIM, 2 * HEAD_DIM)
    per_tile = jnp.concatenate([per_pos[0::2], per_pos[1::2]], axis=2)
    return per_tile.reshape(TILES_PER_CHUNK // 2, 2 * HEAD_DIM, 4 * HEAD_DIM).astype(BF16)


def kernel(x_prompt, x_sample, p_prompt, p_sample, page_table, cache_k_cmp, cache_v_cmp, cache_k_sel, cache_v_sel, cache_k_win, cache_v_win, state_conv, norm_mix_pre, norm_mix_post, norm_ffn_pre, norm_ffn_post, ffn_w_gate, ffn_w_up, ffn_w_down, ple_w_proj, ple_w_gate, nsa_w_in, nsa_w_out, nsa_cmp_pe_k, nsa_cmp_w1_k, nsa_cmp_w2_k, nsa_cmp_pe_v, nsa_cmp_w1_v, nsa_cmp_w2_v, conv_w_pw1, conv_b_pw1, conv_w_dw, conv_b_dw, conv_ln_g, conv_ln_b, conv_w_pw2, conv_b_pw2):
    h = jnp.concatenate([x_prompt.reshape(M_PROMPT, D_MODEL), x_sample.reshape(M_SAMPLE, D_MODEL)], axis=0)
    p_all = jnp.concatenate([p_prompt.reshape(DEPTH, M_PROMPT, PLE_DIM),
                             p_sample.reshape(DEPTH, M_SAMPLE, PLE_DIM)], axis=1).astype(BF16)
    pt_flat = page_table.reshape(-1)
    w_gate, w_up, w_down = ffn_w_gate.astype(BF16), ffn_w_up.astype(BF16), ffn_w_down.astype(BF16)
    w_ple_gate, w_ple = ple_w_gate.astype(BF16), ple_w_proj.astype(BF16)

    def residual_tail(h, layer, split=False):
        act = _dual_proj(h, _row(norm_ffn_pre[layer]), w_gate, w_up, layer, D_FF, 0, None, "swiglu", BF16,
                         f"ffn_up_{layer}")
        h = _proj_norm_res(act, w_down, layer, _row(norm_ffn_post[layer]), h, 4, f"ffn_down_{layer}")
        return _ple(h, p_all, w_ple_gate, w_ple, layer, f"ple_{layer}", split=split)

    n_main = Q_DIM + 6 * KV_DIM
    w_gates = jnp.pad(nsa_w_in[0][:, n_main:], ((0, 0), (0, HEAD_DIM - 3 * N_HEADS))).astype(BF16)
    cos, sin = _rope_tables()
    qkv, *kvp, kvs, gates = _in_proj(h, _row(norm_mix_pre[0]), nsa_w_in.astype(BF16), w_gates, cos, sin)

    gates_t = gates[:, :3 * N_HEADS].reshape(M_ALL, 3, N_KV, GQA_R).transpose(2, 0, 1, 3).reshape(N_KV, M_ALL, 3 * GQA_R)
    gates_t = jnp.pad(gates_t, ((0, 0), (0, 0), (0, HEAD_DIM - 3 * GQA_R)))

    kv_s = kvs[:, -M_SAMPLE * N_KV:].reshape(6, DEC_BATCH, DEC_SEQ, N_KV, HEAD_DIM)
    new_page = lambda c: jnp.pad(kv_s[c], ((0, 0), (0, PAGE_SIZE - DEC_SEQ), (0, 0), (0, 0))).reshape(-1, HEAD_DIM)

    wk, wv = _cmp_weights(nsa_cmp_w1_k[0]), _cmp_weights(nsa_cmp_w1_v[0])
    pe_k = jnp.broadcast_to(nsa_cmp_pe_k[0].reshape(1, -1), (8, CMP_BLOCK * HEAD_DIM))
    pe_v = jnp.broadcast_to(nsa_cmp_pe_v[0].reshape(1, -1), (8, CMP_BLOCK * HEAD_DIM))
    w1k, w1v = nsa_cmp_w1_k[0].astype(BF16), nsa_cmp_w1_v[0].astype(BF16)
    w2k, w2v = nsa_cmp_w2_k[0].astype(BF16), nsa_cmp_w2_v[0].astype(BF16)

    pages_prompt = SEQ // PAGE_SIZE
    ident = jnp.arange(BATCH * pages_prompt, dtype=jnp.int32)
    pk0, pk1, pv0, pv1 = _compress_stage1(ident, kvp[0][None], 0, kvp[1][None], 0, wk, wv, BATCH, pages_prompt,
                                          "nsa_cmp_prompt")
    zero_next = jnp.zeros((BATCH * 8, HEAD_DIM), F32)
    kc_p = _compress_finish(pk0, pk1, zero_next, pe_k, w1k, w2k, "nsa_cmp_fin_k_prompt")
    vc_p = _compress_finish(pv0, pv1, zero_next, pe_v, w1v, w2v, "nsa_cmp_fin_v_prompt")

    pool3 = lambda c: c.reshape(1, -1, HEAD_DIM)
    pk0, pk1, pv0, pv1 = _compress_stage1(pt_flat, pool3(cache_k_cmp), 0, pool3(cache_v_cmp), 0, wk, wv, DEC_BATCH,
                                          N_PAGES, "nsa_cmp_sample")
    _, pk1_n, _, pv1_n = _compress_stage1(jnp.arange(DEC_BATCH, dtype=jnp.int32), new_page(0)[None], 0,
                                          new_page(1)[None], 0, wk, wv, 1, DEC_BATCH, "nsa_cmp_sample_new")
    first_chunk = lambda pn: pn[0].reshape(DEC_BATCH, CHUNKS_PER_PAGE * 8, HEAD_DIM)[:, 0:8].reshape(-1, HEAD_DIM)
    kc_s = _compress_finish(pk0, pk1, first_chunk(pk1_n), pe_k, w1k, w2k, "nsa_cmp_fin_k_sample")
    vc_s = _compress_finish(pv0, pv1, first_chunk(pv1_n), pe_v, w1v, w2v, "nsa_cmp_fin_v_sample")

    pool2 = lambda c: c.reshape(-1, HEAD_DIM)
    kw_all = jnp.concatenate([cache_k_win[0], kv_s[4]], axis=1)
    vw_all = jnp.concatenate([cache_v_win[0], kv_s[5]], axis=1)
    band = lambda w: jnp.pad(w.transpose(0, 2, 1, 3),
                             ((0, 0), (0, 0), (0, WBAND_SAMPLE - w.shape[1]), (0, 0))).astype(BF16)
    q_s = qkv[M_PROMPT:, :Q_DIM].reshape(DEC_BATCH, DEC_SEQ, N_KV, GQA_R, HEAD_DIM).transpose(0, 2, 3, 1, 4)
    q_s = q_s.reshape(DEC_BATCH, N_KV, GQA_R * DEC_SEQ, HEAD_DIM)
    o_s = _attn_sample(pt_flat, q_s, gates_t[:, M_PROMPT:], kc_s, vc_s, band(kw_all), band(vw_all),
                       pool2(cache_k_sel), pool2(cache_v_sel), new_page(2), new_page(3))
    o_all = _attn_prompt(qkv, gates_t, kc_p, vc_p, jnp.pad(o_s.astype(BF16), ((M_PROMPT, 0), (0, 0))))

    h = _proj_norm_res(o_all, nsa_w_out.astype(BF16), 0, _row(norm_mix_post[0]), h, 2, "nsa_out_proj")
    h = residual_tail(h, 0)

    w_pw1 = conv_w_pw1.astype(BF16)
    u = _dual_proj(h, _row(norm_mix_pre[1]), w_pw1, w_pw1, 0, D_MODEL, D_MODEL, _row(conv_b_pw1[0]), "glu", F32,
                   "conv_pw1_glu")
    wdw = jnp.pad(conv_w_dw[0], ((0, HALO - CONV_W), (0, 0)))
    conv_args = (wdw, _row(conv_b_dw[0]), _row(conv_ln_g[0]), _row(conv_ln_b[0]), conv_w_pw2.astype(BF16),
                 _row(conv_b_pw2[0]), _row(norm_mix_post[1]))
    t_p = 256
    state = jnp.pad(state_conv[0], ((0, 0), (HALO - (CONV_W - 1), 0), (0, 0))).reshape(DEC_BATCH * HALO, D_MODEL)
    h_s = _conv_tail(u, state, *conv_args, h, t=DEC_SEQ, n_tiles=DEC_BATCH, row0=M_PROMPT, tiles_per_seq=1,
                     halo_is_state=True)
    h = _conv_tail(u, u, *conv_args, h, t=t_p, n_tiles=M_PROMPT // t_p, row0=0, tiles_per_seq=SEQ // t_p,
                   halo_is_state=False, out_init=jnp.pad(h_s, ((M_PROMPT, 0), (0, 0))))
    y_p, y_s = residual_tail(h, 1, split=True)

    y_prompt = y_p.reshape(BATCH, SEQ, D_MODEL)
    y_sample = y_s[-M_SAMPLE:].reshape(DEC_BATCH, DEC_SEQ, D_MODEL)
    kv_p = [a.reshape(1, BATCH, SEQ, N_KV, HEAD_DIM) for a in kvp]
    outs = [y_prompt, y_sample]
    for c in range(4):
        outs.append(kv_p[c])
        outs.append(kv_s[c][None])
    w_keep = min(WINDOW, SEQ)
    outs.append(kv_p[4][:, :, SEQ - w_keep:])
    outs.append(kw_all[None, :, -WINDOW:])
    outs.append(kv_p[5][:, :, SEQ - w_keep:])
    outs.append(vw_all[None, :, -WINDOW:])
    n_keep = CONV_W - 1
    outs.append(jnp.stack([u[b * SEQ + SEQ - n_keep:(b + 1) * SEQ] for b in range(BATCH)])[None])
    u_s = u[M_PROMPT:].reshape(DEC_BATCH, DEC_SEQ, D_MODEL)
    outs.append(jnp.concatenate([state_conv[0], u_s], axis=1)[None, :, -n_keep:])
    return tuple(outs)
```

```python
import functools
import math

import numpy as np
import jax
import jax.numpy as jnp
from jax import lax
from jax.experimental import pallas as pl
from jax.experimental.pallas import tpu as pltpu

F32 = jnp.float32
BF16 = jnp.bfloat16

D_MODEL = 2048
BATCH = 2
SEQ = 4096
DEPTH = 2
DEC_BATCH = 8
DEC_SEQ = 8
PAST_LEN = 16384
PAGE_SIZE = 128
N_HEADS = 16
HEAD_DIM = 128
N_KV = 4
GQA_R = 4
Q_DIM = 2048
KV_DIM = 512
CMP_BLOCK = 32
CMP_STRIDE = 16
SEL_BLOCK = 64
TOPK = 16
WINDOW = 512
ROPE_THETA = 10000.0
CONV_W = 31
D_FF = 5632
PLE_DIM = 256
RMS_EPS = 1e-6
LN_EPS = 1e-5
NEG = -1e30

M_PROMPT = BATCH * SEQ
M_SAMPLE = DEC_BATCH * DEC_SEQ
M_ALL = M_PROMPT + M_SAMPLE
N_PAGES = PAST_LEN // PAGE_SIZE
PAGE_ROWS = PAGE_SIZE * N_KV
CHUNKS_PER_PAGE = PAGE_SIZE // CMP_STRIDE
PAGES_PER_STEP = 16

TM = 688
TM_ROWS = 688
TN = 512
TK_ATT = 1024
TQ_PROMPT = 128
ATT_HEADS_PER_STEP = 4
WBAND_PROMPT = WINDOW + TQ_PROMPT
WBAND_SAMPLE = WINDOW + SEL_BLOCK
VMEM_LIMIT = 56 * 1024 * 1024
VMEM_LIMIT_ATTN = 60 * 1024 * 1024

Q_PRESCALE = HEAD_DIM ** -0.5 * math.log2(math.e)


def _cparams(sem, vmem_limit=VMEM_LIMIT):
    return pltpu.CompilerParams(dimension_semantics=sem, vmem_limit_bytes=vmem_limit)


def _rms_scale(x, g):
    ms = jnp.mean(x * x, axis=-1, keepdims=True)
    return x * lax.rsqrt(ms + RMS_EPS) * g


def _dot(a, b):
    return jnp.dot(a, b, preferred_element_type=F32)


def _dot_nt(a, b):
    return lax.dot_general(a, b, (((1,), (1,)), ((), ())), preferred_element_type=F32)


N_QKV_TILES = (Q_DIM + 6 * KV_DIM) // TN
N_Q_TILES = Q_DIM // TN


def _in_proj_kernel(x_ref, g_ref, w_ref, wg_ref, cos_ref, sin_ref, qkv_ref, *rest):
    kvp_refs, (kvs_ref, gate_ref, xn_ref) = rest[:6], rest[6:]
    j = pl.program_id(1)

    @pl.when(j == 0)
    def _():
        xn = _rms_scale(x_ref[...], g_ref[...]).astype(BF16)
        xn_ref[...] = xn
        gate_ref[...] = jax.nn.sigmoid(_dot(xn, wg_ref[...]))

    acc = _dot(xn_ref[...], w_ref[...])
    is_rope = (j < N_Q_TILES + 1) | (j == N_Q_TILES + 2) | (j == N_Q_TILES + 4)
    post = jnp.where(j < N_Q_TILES, Q_PRESCALE, 1.0)
    cos = jnp.where(is_rope, cos_ref[...], 1.0) * post
    sin = jnp.where(is_rope, sin_ref[...], 0.0) * post
    parts = []
    for h in range(TN // HEAD_DIM):
        a = acc[:, h * HEAD_DIM:(h + 1) * HEAD_DIM]
        parts.append(a * cos + pltpu.roll(a, HEAD_DIM // 2, axis=1) * sin)
    qkv_ref[...] = jnp.concatenate(parts, axis=1).astype(BF16)

    @pl.when(j >= N_Q_TILES)
    def _():
        for g in range(N_KV):
            kvs_ref[pl.ds(g, M_SAMPLE, stride=N_KV), :] = parts[g][TM - M_SAMPLE:]

    for c in range(6):
        @pl.when(j == N_Q_TILES + c)
        def _(c=c):
            for g in range(N_KV):
                kvp_refs[c][pl.ds(g, TM, stride=N_KV), :] = parts[g]


def _in_proj(x, g, w, wg, cos, sin):
    m = x.shape[0]
    assert m == M_ALL and (m // TM - 1) * TM + (TM - M_SAMPLE) == M_PROMPT
    return pl.pallas_call(
        _in_proj_kernel,
        grid=(m // TM, N_QKV_TILES),
        in_specs=[
            pl.BlockSpec((TM, D_MODEL), lambda i, j: (i, 0)),
            pl.BlockSpec((1, D_MODEL), lambda i, j: (0, 0)),
            pl.BlockSpec((None, D_MODEL, TN), lambda i, j: (0, 0, j)),
            pl.BlockSpec((D_MODEL, HEAD_DIM), lambda i, j: (0, 0)),
            pl.BlockSpec((TM, HEAD_DIM), lambda i, j: (i, 0)),
            pl.BlockSpec((TM, HEAD_DIM), lambda i, j: (i, 0)),
        ],
        out_specs=[
            pl.BlockSpec((TM, TN), lambda i, j: (i, j)),
        ] + [pl.BlockSpec((TM * N_KV, HEAD_DIM), lambda i, j: (i, 0))] * 6 + [
            pl.BlockSpec((None, M_SAMPLE * N_KV, HEAD_DIM), lambda i, j: (jnp.maximum(j - N_Q_TILES, 0), i, 0)),
            pl.BlockSpec((TM, HEAD_DIM), lambda i, j: (i, 0)),
        ],
        out_shape=[
            jax.ShapeDtypeStruct((m, Q_DIM + 6 * KV_DIM), BF16),
        ] + [jax.ShapeDtypeStruct((M_PROMPT * N_KV, HEAD_DIM), F32)] * 6 + [
            jax.ShapeDtypeStruct((6, (m // TM) * M_SAMPLE * N_KV, HEAD_DIM), F32),
            jax.ShapeDtypeStruct((m, HEAD_DIM), F32),
        ],
        scratch_shapes=[pltpu.VMEM((TM, D_MODEL), BF16)],
        compiler_params=_cparams(("parallel", "arbitrary")),
        name="nsa_in_proj",
    )(x, g, w, wg, cos, sin)


def _dual_kernel(*refs, mode, has_bias):
    if has_bias:
        x_ref, g_ref, wa_ref, wb_ref, ba_ref, bb_ref, o_ref, xn_ref = refs
    else:
        x_ref, g_ref, wa_ref, wb_ref, o_ref, xn_ref = refs

    @pl.when(pl.program_id(1) == 0)
    def _():
        xn_ref[...] = _rms_scale(x_ref[...], g_ref[...]).astype(BF16)

    xn = xn_ref[...]
    a = _dot(xn, wa_ref[...])
    b = _dot(xn, wb_ref[...])
    if has_bias:
        a = a + ba_ref[...]
        b = b + bb_ref[...]
    if mode == "swiglu":
        o = jax.nn.silu(a) * b
    else:
        o = a * jax.nn.sigmoid(b)
    o_ref[...] = o.astype(o_ref.dtype)


def _dual_proj(x, g, wa, wb, layer, n_out, b_col0, bias, mode, out_dtype, name):
    m = x.shape[0]
    nb = b_col0 // TN
    in_specs = [
        pl.BlockSpec((TM, D_MODEL), lambda i, j: (i, 0)),
        pl.BlockSpec((1, D_MODEL), lambda i, j: (0, 0)),
        pl.BlockSpec((None, D_MODEL, TN), lambda i, j: (layer, 0, j)),
        pl.BlockSpec((None, D_MODEL, TN), lambda i, j: (layer, 0, j + nb)),
    ]
    args = [x, g, wa, wb]
    if bias is not None:
        in_specs += [pl.BlockSpec((1, TN), lambda i, j: (0, j)), pl.BlockSpec((1, TN), lambda i, j: (0, j + nb))]
        args += [bias, bias]
    return pl.pallas_call(
        functools.partial(_dual_kernel, mode=mode, has_bias=bias is not None),
        grid=(m // TM, n_out // TN),
        in_specs=in_specs,
        out_specs=pl.BlockSpec((TM, TN), lambda i, j: (i, j)),
        out_shape=jax.ShapeDtypeStruct((m, n_out), out_dtype),
        scratch_shapes=[pltpu.VMEM((TM, D_MODEL), BF16)],
        compiler_params=_cparams(("parallel", "arbitrary")),
        name=name,
    )(*args)


def _proj_norm_res_kernel(a_ref, w_ref, g_ref, h_ref, o_ref, acc_ref, *, nk):
    k = pl.program_id(1)
    part = _dot(a_ref[...], w_ref[...])

    @pl.when(k == 0)
    def _():
        acc_ref[...] = part

    @pl.when(k > 0)
    def _():
        acc_ref[...] += part

    @pl.when(k == nk - 1)
    def _():
        o_ref[...] = h_ref[...] + _rms_scale(acc_ref[...], g_ref[...])


def _proj_norm_res(a, w, layer, g, h, nk, name):
    m, kdim = a.shape
    tk = kdim // nk
    return pl.pallas_call(
        functools.partial(_proj_norm_res_kernel, nk=nk),
        grid=(m // TM_ROWS, nk),
        in_specs=[
            pl.BlockSpec((TM_ROWS, tk), lambda i, k: (i, k)),
            pl.BlockSpec((None, tk, D_MODEL), lambda i, k: (layer, k, 0)),
            pl.BlockSpec((1, D_MODEL), lambda i, k: (0, 0)),
            pl.BlockSpec((TM_ROWS, D_MODEL), lambda i, k: (i, 0)),
        ],
        out_specs=pl.BlockSpec((TM_ROWS, D_MODEL), lambda i, k: (i, 0)),
        out_shape=jax.ShapeDtypeStruct((m, D_MODEL), F32),
        scratch_shapes=[pltpu.VMEM((TM_ROWS, D_MODEL), F32)],
        compiler_params=_cparams(("parallel", "arbitrary")),
        name=name,
    )(a, w, g, h)


def _ple_kernel(h_ref, hc_ref, p_ref, wg_ref, wp_ref, o_ref, *rest, split):
    hb_ref = rest[-1]

    @pl.when(pl.program_id(1) == 0)
    def _():
        hb_ref[...] = h_ref[...].astype(BF16)

    gate = jax.nn.sigmoid(_dot(hb_ref[...], wg_ref[...]))
    out = hc_ref[...] + gate * _dot(p_ref[...], wp_ref[...])
    o_ref[...] = out
    if split:
        rest[0][...] = out[TM - M_SAMPLE:]


def _ple(h, p, wg, wp, layer, name, split=False):
    m = h.shape[0]
    out_specs = pl.BlockSpec((TM, TN), lambda i, j: (i, j))
    out_shape = jax.ShapeDtypeStruct((m, D_MODEL), F32)
    if split:
        assert (m // TM - 1) * TM + (TM - M_SAMPLE) == M_PROMPT
        out_specs = [out_specs, pl.BlockSpec((M_SAMPLE, TN), lambda i, j: (i, j))]
        out_shape = [jax.ShapeDtypeStruct((M_PROMPT, D_MODEL), F32),
                     jax.ShapeDtypeStruct(((m // TM) * M_SAMPLE, D_MODEL), F32)]
    return pl.pallas_call(
        functools.partial(_ple_kernel, split=split),
        grid=(m // TM, D_MODEL // TN),
        in_specs=[
            pl.BlockSpec((TM, D_MODEL), lambda i, j: (i, 0)),
            pl.BlockSpec((TM, TN), lambda i, j: (i, j)),
            pl.BlockSpec((None, TM, PLE_DIM), lambda i, j: (layer, i, 0)),
            pl.BlockSpec((None, D_MODEL, TN), lambda i, j: (layer, 0, j)),
            pl.BlockSpec((None, PLE_DIM, TN), lambda i, j: (layer, 0, j)),
        ],
        out_specs=out_specs,
        out_shape=out_shape,
        scratch_shapes=[pltpu.VMEM((TM, D_MODEL), BF16)],
        compiler_params=_cparams(("parallel", "arbitrary")),
        name=name,
    )(h, h, p, wg, wp)


TILES_PER_CHUNK = CMP_STRIDE * N_KV // 8


def _compress_kernel(pt_ref, *refs, n):
    k_refs, v_refs = refs[0:n], refs[n:2 * n]
    wk_ref, wv_ref, pk0_ref, pk1_ref, pv0_ref, pv1_ref = refs[2 * n:2 * n + 6]
    for srcs, w_ref, o0_ref, o1_ref in ((k_refs, wk_ref, pk0_ref, pk1_ref), (v_refs, wv_ref, pv0_ref, pv1_ref)):
        acc = None
        for jj in range(TILES_PER_CHUNK // 2):
            halves = []
            for j in (2 * jj, 2 * jj + 1):
                rows = [srcs[p][(c * TILES_PER_CHUNK + j) * 8:(c * TILES_PER_CHUNK + j + 1) * 8, :]
                        for p in range(n) for c in range(CHUNKS_PER_PAGE)]
                halves.append(jnp.concatenate(rows, axis=0))
            term = _dot(jnp.concatenate(halves, axis=1).astype(BF16), w_ref[jj])
            acc = term if acc is None else acc + term
        width = 2 * HEAD_DIM
        folded = acc[:, :width] + pltpu.roll(acc[:, width:], acc.shape[0] - N_KV, axis=0)
        o0_ref[0] = folded[:, :HEAD_DIM]
        o1_ref[0] = folded[:, HEAD_DIM:]


def _compress_stage1(page_table, k_src, k_type, v_src, v_type, wk, wv, n_seq, pages_per_seq, name):
    n = min(PAGES_PER_STEP, pages_per_seq)
    groups = pages_per_seq // n
    n_chunks = pages_per_seq * CHUNKS_PER_PAGE

    def src_spec(slot, typ):
        return pl.BlockSpec(
            (None, PAGE_ROWS, HEAD_DIM), lambda b, t, pt: (typ, pt[b * pages_per_seq + t * n + slot], 0))

    w_spec = pl.BlockSpec(wk.shape, lambda b, t, pt: (0, 0, 0))
    out_spec = pl.BlockSpec((1, n * CHUNKS_PER_PAGE * 8, HEAD_DIM), lambda b, t, pt: (b, t, 0))
    out_sds = jax.ShapeDtypeStruct((n_seq, n_chunks * 8, HEAD_DIM), F32)
    return pl.pallas_call(
        functools.partial(_compress_kernel, n=n),
        grid_spec=pltpu.PrefetchScalarGridSpec(
            num_scalar_prefetch=1,
            grid=(n_seq, groups),
            in_specs=[src_spec(s, k_type) for s in range(n)] + [src_spec(s, v_type) for s in range(n)]
            + [w_spec, w_spec],
            out_specs=[out_spec] * 4,
        ),
        out_shape=[out_sds] * 4,
        compiler_params=_cparams(("parallel", "arbitrary")),
        name=name,
    )(page_table, *([k_src] * n), *([v_src] * n), wk, wv)


def _compress_finish_kernel(p0_ref, p1_ref, pn_ref, pe_ref, w1_ref, w2_ref, o_ref):
    c = o_ref.shape[2]
    bias = _dot(pe_ref[...].astype(BF16), w1_ref[...])[0:1]
    row = lax.broadcasted_iota(jnp.int32, (c, 1), 0)
    for g in range(N_KV):
        p0 = p0_ref[0, pl.ds(g, c, stride=8), :]
        p1 = p1_ref[0, pl.ds(g, c, stride=8), :]
        nxt = pltpu.roll(p1, c - 1, axis=0)
        nxt = jnp.where(row == c - 1, pn_ref[g:g + 1, :], nxt)
        hid = p0 + nxt + bias
        o_ref[0, g] = _dot(jax.nn.silu(hid).astype(BF16), w2_ref[...]).astype(o_ref.dtype)


def _compress_finish(p0, p1, p1_next, pe8, w1, w2, name):
    n_seq, rows, _ = p0.shape
    c = rows // 8
    seq_spec = pl.BlockSpec((1, rows, HEAD_DIM), lambda b: (b, 0, 0))
    return pl.pallas_call(
        _compress_finish_kernel,
        grid=(n_seq,),
        in_specs=[
            seq_spec, seq_spec,
            pl.BlockSpec((8, HEAD_DIM), lambda b: (b, 0)),
            pl.BlockSpec((8, CMP_BLOCK * HEAD_DIM), lambda b: (0, 0)),
            pl.BlockSpec((CMP_BLOCK * HEAD_DIM, HEAD_DIM), lambda b: (0, 0)),
            pl.BlockSpec((HEAD_DIM, HEAD_DIM), lambda b: (0, 0)),
        ],
        out_specs=pl.BlockSpec((1, N_KV, c, HEAD_DIM), lambda b: (b, 0, 0, 0)),
        out_shape=jax.ShapeDtypeStruct((n_seq, N_KV, c, HEAD_DIM), BF16),
        compiler_params=_cparams(("parallel",)),
        name=name,
    )(p0, p1, p1_next, pe8, w1, w2)


def _masked_softmax2(s, mask):
    s = jnp.where(mask, s, NEG)
    m = jnp.max(s, axis=-1, keepdims=True)
    e = jnp.where(mask, jnp.exp2(s - m), 0.0)
    return e / jnp.maximum(jnp.sum(e, axis=-1, keepdims=True), 1e-30)


def _split3(x):
    parts = []
    rem = x
    for _ in range(3):
        part = rem.astype(BF16)
        parts.append(part)
        rem = rem - part.astype(F32)
    return parts


def _sel_tile(kt, carry, qa, t_row, ks, vs, e_ref, e_row0, tk, masked, lanes=slice(0, HEAD_DIM)):
    m, acc = carry
    r0 = pl.multiple_of(kt * tk, tk)
    e0 = pl.multiple_of(kt * tk - e_row0, tk)
    e_tile = e_ref[pl.ds(e0, tk), :]
    ka = jnp.concatenate([ks[pl.ds(r0, tk), lanes], e_tile], axis=1)
    sc = _dot_nt(qa, ka)
    if masked:
        kpos = r0 + lax.broadcasted_iota(jnp.int32, (1, tk), 1)
        sc = jnp.where(kpos <= t_row, sc, NEG)
    m_new = jnp.maximum(m, jnp.max(sc, axis=-1, keepdims=True))
    alpha = jnp.exp2(m - m_new)
    p = jnp.exp2(sc - m_new)
    va = jnp.concatenate([vs[pl.ds(r0, tk), lanes], e_tile], axis=1)
    acc = alpha * acc + _dot(p.astype(BF16), va)
    return m_new, acc


def _sel_init(rows):
    return jnp.full((rows, 1), -jnp.inf, F32), jnp.zeros((rows, 2 * HEAD_DIM), F32)


def _gated_merge(gt, o_c, o_s, o_w, tq):
    outs = []
    for r in range(GQA_R):
        sl = slice(r * tq, (r + 1) * tq)
        outs.append(gt[:, r:r + 1] * o_c[sl] + gt[:, GQA_R + r:GQA_R + r + 1] * o_s[sl]
                    + gt[:, 2 * GQA_R + r:2 * GQA_R + r + 1] * o_w[sl])
    return jnp.concatenate(outs, axis=1)


def _cmp_to_sel_matrix(nb, nsp):
    n = np.arange(nb)[:, None]
    j = np.arange(nsp)[None, :]
    return ((n >= 4 * j) & (n <= 4 * j + 3)).astype(np.float32) + ((n >= 4 * j - 1) & (n <= 4 * j + 2)).astype(np.float32)


def _block_onehot(n_keys):
    k = np.arange(n_keys)[:, None] // SEL_BLOCK
    return jnp.asarray((k % 128 == np.arange(128)[None, :]).astype(np.float32), BF16)


def _window_bias_prompt():
    q = np.arange(TQ_PROMPT)[:, None]
    kb = np.arange(WBAND_PROMPT)[None, :]
    pats = [kb <= TQ_PROMPT * i + q for i in range(WINDOW // TQ_PROMPT)]
    pats.append((kb > q) & (kb <= q + WINDOW))
    return jnp.asarray(np.where(np.stack(pats), 0.0, NEG), F32)


def _attn_prompt_kernel(q_ref, gate_ref, kc_ref, vc_ref, ks_ref, vs_ref, kw_ref, vw_ref, mmt_ref, e_ref, wb_ref,
                        o_init_ref, o_ref, *, nb, ns):
    del o_init_ref
    tq = TQ_PROMPT
    rows = GQA_R * tq
    qi = pl.program_id(2)
    t0 = qi * tq
    t_row = t0 + (lax.broadcasted_iota(jnp.int32, (rows, 1), 0) & (tq - 1))
    cmp_end = lax.broadcasted_iota(jnp.int32, (1, nb), 1) * CMP_STRIDE + (CMP_BLOCK - 1)
    jb = lax.broadcasted_iota(jnp.int32, (ns, tq), 0)
    cur = lax.shift_right_logical(t0 + lax.broadcasted_iota(jnp.int32, (1, tq), 1), 6)
    forced = (jb == 0) | (jb == cur) | (jb == cur - 1)
    eligible = jb <= cur
    groups = ns // 8
    jgrp = [jb[8 * v:8 * v + 8] for v in range(groups)]
    eye = jnp.where(lax.broadcasted_iota(jnp.int32, (tq, tq), 0) == lax.broadcasted_iota(jnp.int32, (tq, tq), 1),
                    1.0, 0.0).astype(BF16)
    mmt = mmt_ref[...]

    heads = range(ATT_HEADS_PER_STEP)
    qs, o_cs, qas = [], [], []
    for h in heads:
        qq = q_ref[:, h * GQA_R * HEAD_DIM:(h + 1) * GQA_R * HEAD_DIM]
        q = jnp.concatenate([qq[:, r * HEAD_DIM:(r + 1) * HEAD_DIM] for r in range(GQA_R)], axis=0)

        s = _dot_nt(q, kc_ref[0, h])
        p_c = _masked_softmax2(s, cmp_end <= t_row)
        o_cs.append(_dot(p_c.astype(BF16), vc_ref[0, h]))
        imp = p_c[0:tq]
        for r in range(1, GQA_R):
            imp = imp + p_c[r * tq:(r + 1) * tq]

        blk_imp = None
        for part in _split3(imp):
            term = _dot_nt(mmt, part)
            blk_imp = term if blk_imp is None else blk_imp + term
        score = jnp.where(forced, jnp.inf, jnp.where(eligible, blk_imp, -jnp.inf))
        sgrp = [score[8 * v:8 * v + 8] for v in range(groups)]
        rank = [jnp.zeros((8, tq), F32) for _ in range(groups)]
        for i in range(ns):
            ci = score[i:i + 1]
            for v in range(groups):
                if 8 * v > i:
                    beats = ci >= sgrp[v]
                elif 8 * v + 7 <= i:
                    beats = ci > sgrp[v]
                else:
                    beats = (ci > sgrp[v]) | ((ci == sgrp[v]) & (jgrp[v] > i))
                rank[v] = rank[v] + jnp.where(beats, 1.0, 0.0)
        rank = jnp.concatenate(rank, axis=0)
        sel_t = jnp.where((rank < float(TOPK)) & eligible, 1.0, 0.0)
        sel_t = jnp.concatenate([sel_t, jnp.zeros((128 - ns, tq), F32)], axis=0).astype(BF16)
        sel = _dot_nt(eye, sel_t)
        negsel = jnp.where(sel > 0.5, 0.0, NEG).astype(BF16)
        qs.append(q)
        qas.append(jnp.concatenate([q, jnp.concatenate([negsel] * GQA_R, axis=0)], axis=1))

    n_tiles = lax.shift_right_logical(t0 + tq - 1, TK_ATT.bit_length() - 1) + 1

    def tiles(kt, carries, masked):
        return tuple(
            _sel_tile(kt, carries[h], qa=qas[h], t_row=t_row, ks=ks_ref, vs=vs_ref, e_ref=e_ref, e_row0=0,
                      tk=TK_ATT, masked=masked, lanes=slice(h * HEAD_DIM, (h + 1) * HEAD_DIM))
            for h in heads)

    carries = lax.fori_loop(0, n_tiles - 1, functools.partial(tiles, masked=False),
                            tuple(_sel_init(rows) for _ in heads))
    carries = tiles(n_tiles - 1, carries, masked=True)

    wstart = pl.multiple_of(jnp.maximum(t0 - WINDOW, 0), tq)
    bias = wb_ref[jnp.minimum(qi, WINDOW // tq)]
    bias = jnp.concatenate([bias] * GQA_R, axis=0)
    for h in heads:
        lanes = slice(h * HEAD_DIM, (h + 1) * HEAD_DIM)
        _, acc = carries[h]
        l = jnp.sum(acc[:, HEAD_DIM:], axis=-1, keepdims=True)
        o_s = acc[:, :HEAD_DIM] / jnp.maximum(l, 1e-30)
        s = _dot_nt(qs[h], kw_ref[pl.ds(wstart, WBAND_PROMPT), lanes]) + bias
        e = jnp.exp2(s - jnp.max(s, axis=-1, keepdims=True))
        p_w = e / jnp.sum(e, axis=-1, keepdims=True)
        o_w = _dot(p_w.astype(BF16), vw_ref[pl.ds(wstart, WBAND_PROMPT), lanes])
        o_ref[:, h * GQA_R * HEAD_DIM:(h + 1) * GQA_R * HEAD_DIM] = _gated_merge(
            gate_ref[h], o_cs[h], o_s, o_w, tq).astype(o_ref.dtype)


def _attn_prompt(qkv, gates_t, kc, vc, o_init):
    tq = TQ_PROMPT
    hp = ATT_HEADS_PER_STEP
    nq = SEQ // tq
    nb = SEQ // CMP_STRIDE
    ns = SEQ // SEL_BLOCK
    k_sel0, v_sel0 = Q_DIM + 2 * KV_DIM, Q_DIM + 3 * KV_DIM
    k_win0, v_win0 = Q_DIM + 4 * KV_DIM, Q_DIM + 5 * KV_DIM
    seq_spec = lambda base: pl.BlockSpec((SEQ, hp * HEAD_DIM), lambda b, g, i: (b, base // (hp * HEAD_DIM) + g))
    blk_spec = pl.BlockSpec((1, hp, nb, HEAD_DIM), lambda b, g, i: (b, g, 0, 0))
    mmt = jnp.asarray(_cmp_to_sel_matrix(nb, ns).T, BF16)
    wbias = _window_bias_prompt()
    return pl.pallas_call(
        functools.partial(_attn_prompt_kernel, nb=nb, ns=ns),
        grid=(BATCH, N_KV // hp, nq),
        in_specs=[
            pl.BlockSpec((tq, hp * GQA_R * HEAD_DIM), lambda b, g, i: (b * nq + i, g)),
            pl.BlockSpec((hp, tq, HEAD_DIM), lambda b, g, i: (g, b * nq + i, 0)),
            blk_spec, blk_spec,
            seq_spec(k_sel0), seq_spec(v_sel0), seq_spec(k_win0), seq_spec(v_win0),
            pl.BlockSpec((ns, nb), lambda b, g, i: (0, 0)),
            pl.BlockSpec((SEQ, 128), lambda b, g, i: (0, 0)),
            pl.BlockSpec(wbias.shape, lambda b, g, i: (0, 0, 0)),
            pl.BlockSpec(memory_space=pl.ANY),
        ],
        out_specs=pl.BlockSpec((tq, hp * GQA_R * HEAD_DIM), lambda b, g, i: (b * nq + i, g)),
        out_shape=jax.ShapeDtypeStruct((M_ALL, Q_DIM), BF16),
        input_output_aliases={11: 0},
        compiler_params=_cparams(("parallel", "parallel", "arbitrary"), VMEM_LIMIT_ATTN),
        name="nsa_attn_prompt",
    )(qkv, gates_t, kc, vc, qkv, qkv, qkv, qkv, mmt, _block_onehot(SEQ), wbias, o_init)


SAMPLE_NB = PAST_LEN // CMP_STRIDE
SAMPLE_NS = -(-(PAST_LEN + DEC_SEQ) // SEL_BLOCK)
SAMPLE_NSP = 384
SAMPLE_PAGES_PER_STEP = 16
KEYS_PER_STEP = SAMPLE_PAGES_PER_STEP * PAGE_SIZE
BLOCKS_PER_STEP = KEYS_PER_STEP // SEL_BLOCK
SAMPLE_STEPS = N_PAGES // SAMPLE_PAGES_PER_STEP


def _attn_sample_kernel(pt_ref, q_ref, gate_ref, kc_ref, vc_ref, kw_ref, vw_ref, mmat_ref, e_ref, *refs):
    n = SAMPLE_PAGES_PER_STEP
    k_refs, v_refs = refs[0:n], refs[n:2 * n]
    kn_ref, vn_ref, o_ref, qa_ref, m_ref, l_ref, acc_ref, oc_ref, ow_ref = refs[2 * n:]
    tq, nb, ns, nsp = DEC_SEQ, SAMPLE_NB, SAMPLE_NS, SAMPLE_NSP
    rows = GQA_R * tq
    t0 = PAST_LEN
    t = pl.program_id(1)
    t_row = t0 + (lax.broadcasted_iota(jnp.int32, (rows, 1), 0) & (tq - 1))

    @pl.when(t == 0)
    def _():
        cmp_end = lax.broadcasted_iota(jnp.int32, (1, nb), 1) * CMP_STRIDE + (CMP_BLOCK - 1)
        jb = lax.broadcasted_iota(jnp.int32, (tq, nsp), 1)
        cur = lax.shift_right_logical(t0 + lax.broadcasted_iota(jnp.int32, (tq, 1), 0), 6)
        forced = (jb == 0) | (jb == cur) | (jb == cur - 1)
        eligible = jb <= cur
        kpos = (PAST_LEN - WINDOW) + lax.broadcasted_iota(jnp.int32, (1, WBAND_SAMPLE), 1)
        win_mask = (kpos <= t_row) & (kpos > t_row - WINDOW)
        mmat = mmat_ref[...]
        for g in range(N_KV):
            q = q_ref[0, g]

            p_c = _masked_softmax2(_dot_nt(q, kc_ref[0, g]), cmp_end <= t_row)
            oc_ref[g] = _dot(p_c.astype(BF16), vc_ref[0, g])
            imp = p_c[0:tq]
            for r in range(1, GQA_R):
                imp = imp + p_c[r * tq:(r + 1) * tq]

            blk_imp = None
            for part in _split3(imp):
                term = _dot(part, mmat)
                blk_imp = term if blk_imp is None else blk_imp + term
            score = jnp.where(forced, jnp.inf, jnp.where(eligible, blk_imp, -jnp.inf))
            rank = jnp.zeros((tq, nsp), F32)
            for i in range(ns):
                ci = score[:, i:i + 1]
                beats = (ci > score) | ((ci == score) & (jb > i))
                rank = rank + jnp.where(beats, 1.0, 0.0)
            negsel = jnp.where((rank < float(TOPK)) & eligible, 0.0, NEG)
            negsel = jnp.concatenate([negsel] * GQA_R, axis=0).astype(BF16)
            pad = jnp.zeros((rows, HEAD_DIM - BLOCKS_PER_STEP), BF16)
            for st in range(SAMPLE_STEPS + 1):
                piece = negsel[:, st * BLOCKS_PER_STEP:(st + 1) * BLOCKS_PER_STEP]
                qa_ref[g, st] = jnp.concatenate([q, piece, pad], axis=1)

            p_w = _masked_softmax2(_dot_nt(q, kw_ref[0, g]), win_mask)
            ow_ref[g] = _dot(p_w.astype(BF16), vw_ref[0, g])

            m_ref[g] = jnp.full((rows, 1), -jnp.inf, F32)
            l_ref[g] = jnp.zeros((rows, 1), F32)
            acc_ref[g] = jnp.zeros((rows, HEAD_DIM), F32)

    def update(g, qa, ka, v, mask):
        sc = _dot_nt(qa, ka)
        if mask is not None:
            sc = jnp.where(mask, sc, NEG)
        m = m_ref[g]
        m_new = jnp.maximum(m, jnp.max(sc, axis=-1, keepdims=True))
        alpha = jnp.exp2(m - m_new)
        p = jnp.exp2(sc - m_new)
        l_ref[g] = alpha * l_ref[g] + jnp.sum(p, axis=-1, keepdims=True)
        acc_ref[g] = alpha * acc_ref[g] + _dot(p.astype(BF16), v)
        m_ref[g] = m_new

    def head_rows(page_refs, g):
        return jnp.concatenate([r[pl.ds(g, PAGE_SIZE, stride=N_KV), :] for r in page_refs], axis=0).astype(BF16)

    e_blk = e_ref[...]
    for g in range(N_KV):
        ka = jnp.concatenate([head_rows(k_refs, g), e_blk], axis=1)
        update(g, qa_ref[g, t], ka, head_rows(v_refs, g), None)

    @pl.when(t == SAMPLE_STEPS - 1)
    def _():
        kpos = t0 + lax.broadcasted_iota(jnp.int32, (1, PAGE_SIZE), 1)
        gt = gate_ref[...]
        for g in range(N_KV):
            ka = jnp.concatenate([head_rows([kn_ref], g), e_blk[0:PAGE_SIZE]], axis=1)
            update(g, qa_ref[g, SAMPLE_STEPS], ka, head_rows([vn_ref], g), kpos <= t_row)
            o_s = acc_ref[g] / jnp.maximum(l_ref[g], 1e-30)
            o_ref[:, g * GQA_R * HEAD_DIM:(g + 1) * GQA_R * HEAD_DIM] = _gated_merge(
                gt[g], oc_ref[g], o_s, ow_ref[g], tq)


def _attn_sample(page_table, q_s, gates_s, kc, vc, kw, vw, k_pool, v_pool, k_new, v_new):
    n = SAMPLE_PAGES_PER_STEP
    rows = GQA_R * DEC_SEQ
    seq_spec = lambda r: pl.BlockSpec((1, N_KV, r, HEAD_DIM), lambda b, t, pt: (b, 0, 0, 0))
    page = (PAGE_ROWS, HEAD_DIM)
    pool_spec = lambda slot: pl.BlockSpec(page, lambda b, t, pt: (pt[b * N_PAGES + t * n + slot], 0))
    new_spec = pl.BlockSpec(page, lambda b, t, pt: (b, 0))
    mmat = jnp.asarray(_cmp_to_sel_matrix(SAMPLE_NB, SAMPLE_NSP), BF16)
    return pl.pallas_call(
        _attn_sample_kernel,
        grid_spec=pltpu.PrefetchScalarGridSpec(
            num_scalar_prefetch=1,
            grid=(DEC_BATCH, SAMPLE_STEPS),
            in_specs=[
                seq_spec(rows),
                pl.BlockSpec((N_KV, DEC_SEQ, HEAD_DIM), lambda b, t, pt: (0, b, 0)),
                seq_spec(SAMPLE_NB), seq_spec(SAMPLE_NB),
                seq_spec(WBAND_SAMPLE), seq_spec(WBAND_SAMPLE),
                pl.BlockSpec((SAMPLE_NB, SAMPLE_NSP), lambda b, t, pt: (0, 0)),
                pl.BlockSpec((KEYS_PER_STEP, HEAD_DIM), lambda b, t, pt: (0, 0)),
            ] + [pool_spec(s) for s in range(n)] * 2 + [new_spec, new_spec],
            out_specs=pl.BlockSpec((DEC_SEQ, Q_DIM), lambda b, t, pt: (b, 0)),
            scratch_shapes=[
                pltpu.VMEM((N_KV, SAMPLE_STEPS + 1, rows, 2 * HEAD_DIM), BF16),
                pltpu.VMEM((N_KV, rows, 1), F32),
                pltpu.VMEM((N_KV, rows, 1), F32),
                pltpu.VMEM((N_KV, rows, HEAD_DIM), F32),
                pltpu.VMEM((N_KV, rows, HEAD_DIM), F32),
                pltpu.VMEM((N_KV, rows, HEAD_DIM), F32),
            ],
        ),
        out_shape=jax.ShapeDtypeStruct((M_SAMPLE, Q_DIM), F32),
        compiler_params=_cparams(("parallel", "arbitrary")),
        name="nsa_attn_sample",
    )(page_table, q_s, gates_s, kc, vc, kw, vw, mmat, _block_onehot(KEYS_PER_STEP),
      *([k_pool] * n), *([v_pool] * n), k_new, v_new)


HALO = 32


def _conv_kernel(u_ref, halo_ref, wdw_ref, bdw_ref, lng_ref, lnb_ref, w2_ref, b2_ref, g_ref, h_ref, *rest,
                 t, tiles_per_seq, halo_is_state):
    o_ref, buf_ref, y_ref = rest[-3:]
    i = pl.program_id(0)
    if halo_is_state:
        buf_ref[0:HALO, :] = halo_ref[...]
    else:
        first = (i % tiles_per_seq) == 0
        buf_ref[0:HALO, :] = jnp.where(first, 0.0, halo_ref[...])
    buf_ref[HALO:HALO + t, :] = u_ref[...]
    off = HALO - (CONV_W - 1)
    for c in range(D_MODEL // HEAD_DIM):
        cs = slice(c * HEAD_DIM, (c + 1) * HEAD_DIM)
        base = buf_ref[:, cs]
        w = wdw_ref[:, cs]
        y = bdw_ref[:, cs]
        n_rows = HALO + t
        for s in range(8):
            taps = [k for k in range(CONV_W) if (off + k) % 8 == s]
            shifted = base if s == 0 else pltpu.roll(base, n_rows - s, axis=0)
            for k in taps:
                a = off + k - s
                y = y + shifted[a:a + t] * w[k:k + 1]
        y_ref[:, cs] = y
    y = y_ref[...]
    mu = jnp.mean(y, axis=-1, keepdims=True)
    yc = y - mu
    var = jnp.mean(yc * yc, axis=-1, keepdims=True)
    z = yc * lax.rsqrt(var + LN_EPS) * lng_ref[...] + lnb_ref[...]
    mval = _dot(jax.nn.silu(z).astype(BF16), w2_ref[...]) + b2_ref[...]
    o_ref[...] = h_ref[...] + _rms_scale(mval, g_ref[...])


def _conv_tail(u, halo_src, wdw, bdw, lng, lnb, w2, b2, g, h, *, t, n_tiles, row0, tiles_per_seq, halo_is_state,
               out_init=None):
    blk0 = row0 // t
    extra_specs, extra_args, aliases = [], [], {}
    out_rows = n_tiles * t
    if out_init is not None:
        extra_specs, extra_args, aliases = [pl.BlockSpec(memory_space=pl.ANY)], [out_init], {10: 0}
        out_rows = out_init.shape[0]
    if halo_is_state:
        halo_spec = pl.BlockSpec((HALO, D_MODEL), lambda i: (i, 0))
    else:
        per = t // HALO
        halo_spec = pl.BlockSpec((HALO, D_MODEL), lambda i: (jnp.maximum((blk0 + i) * per - 1, 0), 0))
    vec = pl.BlockSpec((1, D_MODEL), lambda i: (0, 0))
    kern = functools.partial(_conv_kernel, t=t, tiles_per_seq=tiles_per_seq, halo_is_state=halo_is_state)
    return pl.pallas_call(
        kern,
        grid=(n_tiles,),
        in_specs=[
            pl.BlockSpec((t, D_MODEL), lambda i: (blk0 + i, 0)),
            halo_spec,
            pl.BlockSpec((HALO, D_MODEL), lambda i: (0, 0)),
            vec, vec, vec,
            pl.BlockSpec((None, D_MODEL, D_MODEL), lambda i: (0, 0, 0)),
            vec, vec,
            pl.BlockSpec((t, D_MODEL), lambda i: (blk0 + i, 0)),
        ] + extra_specs,
        out_specs=pl.BlockSpec((t, D_MODEL), lambda i: (i, 0)),
        out_shape=jax.ShapeDtypeStruct((out_rows, D_MODEL), F32),
        input_output_aliases=aliases,
        scratch_shapes=[pltpu.VMEM((HALO + t, D_MODEL), F32), pltpu.VMEM((t, D_MODEL), F32)],
        compiler_params=_cparams(("parallel",)),
        name="conv_tail_state" if halo_is_state else "conv_tail",
    )(u, halo_src, wdw, bdw, lng, lnb, w2, b2, g, h, *extra_args)


def _row(v):
    return v.reshape(1, -1).astype(F32)


def _rope_tables():
    half = HEAD_DIM // 2
    pos = np.concatenate([np.tile(np.arange(SEQ), BATCH), np.tile(PAST_LEN + np.arange(DEC_SEQ), DEC_BATCH)])
    inv = (ROPE_THETA ** (-np.arange(half, dtype=np.float64) / half)).astype(np.float32)
    ang = (pos.astype(np.float32)[:, None] * inv[None, :]).astype(np.float64)
    cos, sin = np.cos(ang).astype(np.float32), np.sin(ang).astype(np.float32)
    return jnp.asarray(np.concatenate([cos, cos], axis=1)), jnp.asarray(np.concatenate([-sin, sin], axis=1))


def _cmp_weights(w1):
    w1r = w1.reshape(CMP_BLOCK // CMP_STRIDE, CMP_STRIDE * HEAD_DIM, HEAD_DIM)
    per_pos = jnp.concatenate([w1r[0], w1r[1]], axis=1).reshape(CMP_STRIDE, HEAD_DIM, 2 * HEAD_DIM)
    per_tile = jnp.concatenate([per_pos[0::2], per_pos[1::2]], axis=2)
    return per_tile.reshape(TILES_PER_CHUNK // 2, 2 * HEAD_DIM, 4 * HEAD_DIM).astype(BF16)


def kernel(x_prompt, x_sample, p_prompt, p_sample, page_table, cache_k_cmp, cache_v_cmp, cache_k_sel, cache_v_sel, cache_k_win, cache_v_win, state_conv, norm_mix_pre, norm_mix_post, norm_ffn_pre, norm_ffn_post, ffn_w_gate, ffn_w_up, ffn_w_down, ple_w_proj, ple_w_gate, nsa_w_in, nsa_w_out, nsa_cmp_pe_k, nsa_cmp_w1_k, nsa_cmp_w2_k, nsa_cmp_pe_v, nsa_cmp_w1_v, nsa_cmp_w2_v, conv_w_pw1, conv_b_pw1, conv_w_dw, conv_b_dw, conv_ln_g, conv_ln_b, conv_w_pw2, conv_b_pw2):
    h = jnp.concatenate([x_prompt.reshape(M_PROMPT, D_MODEL), x_sample.reshape(M_SAMPLE, D_MODEL)], axis=0)
    p_all = jnp.concatenate([p_prompt.reshape(DEPTH, M_PROMPT, PLE_DIM),
                             p_sample.reshape(DEPTH, M_SAMPLE, PLE_DIM)], axis=1).astype(BF16)
    pt_flat = page_table.reshape(-1)
    w_gate, w_up, w_down = ffn_w_gate.astype(BF16), ffn_w_up.astype(BF16), ffn_w_down.astype(BF16)
    w_ple_gate, w_ple = ple_w_gate.astype(BF16), ple_w_proj.astype(BF16)

    def residual_tail(h, layer, split=False):
        act = _dual_proj(h, _row(norm_ffn_pre[layer]), w_gate, w_up, layer, D_FF, 0, None, "swiglu", BF16,
                         f"ffn_up_{layer}")
        h = _proj_norm_res(act, w_down, layer, _row(norm_ffn_post[layer]), h, 4, f"ffn_down_{layer}")
        return _ple(h, p_all, w_ple_gate, w_ple, layer, f"ple_{layer}", split=split)

    n_main = Q_DIM + 6 * KV_DIM
    w_gates = jnp.pad(nsa_w_in[0][:, n_main:], ((0, 0), (0, HEAD_DIM - 3 * N_HEADS))).astype(BF16)
    cos, sin = _rope_tables()
    qkv, *kvp, kvs, gates = _in_proj(h, _row(norm_mix_pre[0]), nsa_w_in.astype(BF16), w_gates, cos, sin)

    gates_t = gates[:, :3 * N_HEADS].reshape(M_ALL, 3, N_KV, GQA_R).transpose(2, 0, 1, 3).reshape(N_KV, M_ALL, 3 * GQA_R)
    gates_t = jnp.pad(gates_t, ((0, 0), (0, 0), (0, HEAD_DIM - 3 * GQA_R)))

    kv_s = kvs[:, -M_SAMPLE * N_KV:].reshape(6, DEC_BATCH, DEC_SEQ, N_KV, HEAD_DIM)
    new_page = lambda c: jnp.pad(kv_s[c], ((0, 0), (0, PAGE_SIZE - DEC_SEQ), (0, 0), (0, 0))).reshape(-1, HEAD_DIM)

    wk, wv = _cmp_weights(nsa_cmp_w1_k[0]), _cmp_weights(nsa_cmp_w1_v[0])
    pe_k = jnp.broadcast_to(nsa_cmp_pe_k[0].reshape(1, -1), (8, CMP_BLOCK * HEAD_DIM))
    pe_v = jnp.broadcast_to(nsa_cmp_pe_v[0].reshape(1, -1), (8, CMP_BLOCK * HEAD_DIM))
    w1k, w1v = nsa_cmp_w1_k[0].astype(BF16), nsa_cmp_w1_v[0].astype(BF16)
    w2k, w2v = nsa_cmp_w2_k[0].astype(BF16), nsa_cmp_w2_v[0].astype(BF16)

    pages_prompt = SEQ // PAGE_SIZE
    ident = jnp.arange(BATCH * pages_prompt, dtype=jnp.int32)
    pk0, pk1, pv0, pv1 = _compress_stage1(ident, kvp[0][None], 0, kvp[1][None], 0, wk, wv, BATCH, pages_prompt,
                                          "nsa_cmp_prompt")
    zero_next = jnp.zeros((BATCH * 8, HEAD_DIM), F32)
    kc_p = _compress_finish(pk0, pk1, zero_next, pe_k, w1k, w2k, "nsa_cmp_fin_k_prompt")
    vc_p = _compress_finish(pv0, pv1, zero_next, pe_v, w1v, w2v, "nsa_cmp_fin_v_prompt")

    pool3 = lambda c: c.reshape(1, -1, HEAD_DIM)
    pk0, pk1, pv0, pv1 = _compress_stage1(pt_flat, pool3(cache_k_cmp), 0, pool3(cache_v_cmp), 0, wk, wv, DEC_BATCH,
                                          N_PAGES, "nsa_cmp_sample")
    _, pk1_n, _, pv1_n = _compress_stage1(jnp.arange(DEC_BATCH, dtype=jnp.int32), new_page(0)[None], 0,
                                          new_page(1)[None], 0, wk, wv, 1, DEC_BATCH, "nsa_cmp_sample_new")
    first_chunk = lambda pn: pn[0].reshape(DEC_BATCH, CHUNKS_PER_PAGE * 8, HEAD_DIM)[:, 0:8].reshape(-1, HEAD_DIM)
    kc_s = _compress_finish(pk0, pk1, first_chunk(pk1_n), pe_k, w1k, w2k, "nsa_cmp_fin_k_sample")
    vc_s = _compress_finish(pv0, pv1, first_chunk(pv1_n), pe_v, w1v, w2v, "nsa_cmp_fin_v_sample")

    pool2 = lambda c: c.reshape(-1, HEAD_DIM)
    kw_all = jnp.concatenate([cache_k_win[0], kv_s[4]], axis=1)
    vw_all = jnp.concatenate([cache_v_win[0], kv_s[5]], axis=1)
    band = lambda w: jnp.pad(w.transpose(0, 2, 1, 3),
                             ((0, 0), (0, 0), (0, WBAND_SAMPLE - w.shape[1]), (0, 0))).astype(BF16)
    q_s = qkv[M_PROMPT:, :Q_DIM].reshape(DEC_BATCH, DEC_SEQ, N_KV, GQA_R, HEAD_DIM).transpose(0, 2, 3, 1, 4)
    q_s = q_s.reshape(DEC_BATCH, N_KV, GQA_R * DEC_SEQ, HEAD_DIM)
    o_s = _attn_sample(pt_flat, q_s, gates_t[:, M_PROMPT:], kc_s, vc_s, band(kw_all), band(vw_all),
                       pool2(cache_k_sel), pool2(cache_v_sel), new_page(2), new_page(3))
    o_all = _attn_prompt(qkv, gates_t, kc_p, vc_p, jnp.pad(o_s.astype(BF16), ((M_PROMPT, 0), (0, 0))))

    h = _proj_norm_res(o_all, nsa_w_out.astype(BF16), 0, _row(norm_mix_post[0]), h, 2, "nsa_out_proj")
    h = residual_tail(h, 0)

    w_pw1 = conv_w_pw1.astype(BF16)
    u = _dual_proj(h, _row(norm_mix_pre[1]), w_pw1, w_pw1, 0, D_MODEL, D_MODEL, _row(conv_b_pw1[0]), "glu", F32,
                   "conv_pw1_glu")
    wdw = jnp.pad(conv_w_dw[0], ((0, HALO - CONV_W), (0, 0)))
    conv_args = (wdw, _row(conv_b_dw[0]), _row(conv_ln_g[0]), _row(conv_ln_b[0]), conv_w_pw2.astype(BF16),
                 _row(conv_b_pw2[0]), _row(norm_mix_post[1]))
    t_p = 256
    state = jnp.pad(state_conv[0], ((0, 0), (HALO - (CONV_W - 1), 0), (0, 0))).reshape(DEC_BATCH * HALO, D_MODEL)
    h_s = _conv_tail(u, state, *conv_args, h, t=DEC_SEQ, n_tiles=DEC_BATCH, row0=M_PROMPT, tiles_per_seq=1,
                     halo_is_state=True)
    h = _conv_tail(u, u, *conv_args, h, t=t_p, n_tiles=M_PROMPT // t_p, row0=0, tiles_per_seq=SEQ // t_p,
                   halo_is_state=False, out_init=jnp.pad(h_s, ((M_PROMPT, 0), (0, 0))))
    y_p, y_s = residual_tail(h, 1, split=True)

    y_prompt = y_p.reshape(BATCH, SEQ, D_MODEL)
    y_sample = y_s[-M_SAMPLE:].reshape(DEC_BATCH, DEC_SEQ, D_MODEL)
    kv_p = [a.reshape(1, BATCH, SEQ, N_KV, HEAD_DIM) for a in kvp]
    outs = [y_prompt, y_sample]
    for c in range(4):
        outs.append(kv_p[c])
        outs.append(kv_s[c][None])
    w_keep = min(WINDOW, SEQ)
    outs.append(kv_p[4][:, :, SEQ - w_keep:])
    outs.append(kw_all[None, :, -WINDOW:])
    outs.append(kv_p[5][:, :, SEQ - w_keep:])
    outs.append(vw_all[None, :, -WINDOW:])
    n_keep = CONV_W - 1
    outs.append(jnp.stack([u[b * SEQ + SEQ - n_keep:(b + 1) * SEQ] for b in range(BATCH)])[None])
    u_s = u[M_PROMPT:].reshape(DEC_BATCH, DEC_SEQ, D_MODEL)
    outs.append(jnp.concatenate([state_conv[0], u_s], axis=1)[None, :, -n_keep:])
    return tuple(outs)
```
